```python
import math
import jax, jax.numpy as jnp
from jax import lax
import numpy as np

D_MODEL = 2048
BATCH = 1
SEQ = 8192
DEPTH = 2
DEC_BATCH = 8
DEC_SEQ = 32
PAST_LEN = 1024

CHUNK = 64
Q_BLOCK = 128
H_A = 4
DH_A = 128
DV_A = 2 * DH_A
H_B = 8
DH_B = 128
D_FF = 5632
ROPE_THETA = 10000.0
EPS = 1e-6
NEG_INF = -1e30
A_QK = 2 * H_A * DH_A
A_V = H_A * DV_A
B_QKV = H_B * DH_B
N_IN = 2 * A_QK + A_V + 3 * B_QKV + 2 * D_MODEL

kernel_name = "streaming_diffattn_stickbreak_hybrid_step"


def _rmsnorm(x, g):
    xf = x.astype(jnp.float32)
    y = xf * lax.rsqrt(jnp.mean(xf * xf, axis=-1, keepdims=True) + EPS)
    return (y * g.astype(jnp.float32)).astype(x.dtype)


def _rope(x, pos):
    d = x.shape[-1]
    inv = 1.0 / (ROPE_THETA ** (jnp.arange(0, d, 2, dtype=jnp.float32) / d))
    ang = pos.astype(jnp.float32)[:, None] * inv[None, :]
    cos = jnp.cos(ang)[None, :, None, :]
    sin = jnp.sin(ang)[None, :, None, :]
    xf = x.astype(jnp.float32)
    x1, x2 = xf[..., : d // 2], xf[..., d // 2:]
    return jnp.concatenate([x1 * cos - x2 * sin, x2 * cos + x1 * sin], axis=-1).astype(x.dtype)


def _swiglu_half(x, g, wg, wu, wd):
    h = _rmsnorm(x, g)
    return x + 0.5 * ((jax.nn.silu(h @ wg) * (h @ wu)) @ wd)


def _query_blocked(fn, q_pos, q):
    b, sq = q.shape[0], q.shape[1]
    qb = Q_BLOCK if sq % Q_BLOCK == 0 else sq
    nb = sq // qb
    if nb == 1:
        return fn((q_pos, q))
    qr = jnp.swapaxes(q.reshape(b, nb, qb, *q.shape[2:]), 0, 1)
    out = lax.map(fn, (q_pos.reshape(nb, qb), qr))
    out = jnp.swapaxes(out, 0, 1)
    return out.reshape(b, sq, *out.shape[3:])


def _diff_attention(q, k, v, q_pos, k_pos, lam):
    scale = DH_A ** -0.5

    def block(args):
        qp, qb = args
        s = jnp.einsum('bqhd,bkhd->bhqk', qb, k, preferred_element_type=jnp.float32) * scale
        mask = (k_pos[None, :] // CHUNK) <= (qp[:, None] // CHUNK)
        p = jax.nn.softmax(jnp.where(mask, s, NEG_INF), axis=-1)
        p = p.reshape(p.shape[0], H_A, 2, p.shape[2], p.shape[3])
        w = p[:, :, 0] - lam * p[:, :, 1]
        return jnp.einsum('bhqk,bkhe->bqhe', w.astype(v.dtype), v)

    return _query_blocked(block, q_pos, q)


def _stick_breaking(q, k, v, q_pos, k_pos):
    scale = DH_B ** -0.5

    def block(args):
        qp, qb = args
        z = jnp.einsum('bqhd,bkhd->bhqk', qb, k, preferred_element_type=jnp.float32) * scale
        mask = k_pos[None, :] < qp[:, None]
        log_keep = jnp.where(mask, jax.nn.log_sigmoid(-z), 0.0)
        after = lax.cumsum(log_keep, axis=3, reverse=True) - log_keep
        a = jnp.where(mask, jnp.exp(jax.nn.log_sigmoid(z) + after), 0.0)
        return jnp.einsum('bhqk,bkhd->bqhd', a.astype(v.dtype), v)

    return _query_blocked(block, q_pos, q)


def _mixer(x, pos, past, li, p):
    b, s, _ = x.shape
    h = _rmsnorm(x, p['mix_norm'])
    proj = h @ p['w_in']
    o1 = A_QK
    o2 = o1 + A_QK
    o3 = o2 + A_V
    o4 = o3 + B_QKV
    o5 = o4 + B_QKV
    o6 = o5 + B_QKV
    o7 = o6 + D_MODEL
    qa, ka, va, qb, kb, vb, ga, gb = jnp.split(proj, [o1, o2, o3, o4, o5, o6, o7], axis=-1)
    qa = _rope(_rmsnorm(qa.reshape(b, s, 2 * H_A, DH_A), p['a_q_norm']), pos)
    ka = _rope(_rmsnorm(ka.reshape(b, s, 2 * H_A, DH_A), p['a_k_norm']), pos)
    va = va.reshape(b, s, H_A, DV_A)
    qb = qb.reshape(b, s, H_B, DH_B)
    kb = kb.reshape(b, s, H_B, DH_B)
    vb = vb.reshape(b, s, H_B, DH_B)
    if past is None:
        ka_all, va_all, kb_all, vb_all = ka, va, kb, vb
        k_pos = pos
    else:
        pk_a, pv_a, pk_b, pv_b = past
        ka_all = jnp.concatenate([pk_a, ka], axis=1)
        va_all = jnp.concatenate([pv_a, va], axis=1)
        kb_all = jnp.concatenate([pk_b, kb], axis=1)
        vb_all = jnp.concatenate([pv_b, vb], axis=1)
        k_pos = jnp.arange(pk_a.shape[1] + s)
    lam_init = 0.8 - 0.6 * math.exp(-0.3 * li)
    f32 = jnp.float32
    lam = (jnp.exp(jnp.sum(p['a_lam_q1'].astype(f32) * p['a_lam_k1'].astype(f32)))
           - jnp.exp(jnp.sum(p['a_lam_q2'].astype(f32) * p['a_lam_k2'].astype(f32)))
           + lam_init)
    oa = _diff_attention(qa, ka_all, va_all, pos, k_pos, lam)
    oa = (_rmsnorm(oa, p['a_sub_norm']) * (1.0 - lam_init)).reshape(b, s, A_V)
    ob = _stick_breaking(qb, kb_all, vb_all, pos, k_pos).reshape(b, s, B_QKV)
    merged = (jax.nn.sigmoid(ga) * (oa @ p['w_branch_a'])
              + jax.nn.sigmoid(gb) * (ob @ p['w_branch_b']))
    return x + merged @ p['w_out'], (ka, va, kb, vb)


def _layer(x, pos, past, li, p):
    x = _swiglu_half(x, p['ffn1_norm'], p['ffn1_w_gate'], p['ffn1_w_up'], p['ffn1_w_down'])
    x, rows = _mixer(x, pos, past, li, p)
    x = _swiglu_half(x, p['ffn2_norm'], p['ffn2_w_gate'], p['ffn2_w_up'], p['ffn2_w_down'])
    return x, rows


def setup_inputs(seed: int = 0) -> dict:
    key = jax.random.key(seed)
    ks = iter(jax.random.split(key, 32))

    def nrm(shape, scale):
        return jax.random.normal(next(ks), shape, jnp.float32) * scale

    def gain(shape):
        return 1.0 + 0.02 * jax.random.normal(next(ks), shape, jnp.float32)

    d, f = D_MODEL, D_FF
    return {
        'x_prompt': nrm((BATCH, SEQ, d), 1.0),
        'x_sample': nrm((DEC_BATCH, DEC_SEQ, d), 1.0),
        'cache_a_k': nrm((DEPTH, DEC_BATCH, PAST_LEN, 2 * H_A, DH_A), 1.0),
        'cache_a_v': nrm((DEPTH, DEC_BATCH, PAST_LEN, H_A, DV_A), 1.0),
        'cache_b_k': nrm((DEPTH, DEC_BATCH, PAST_LEN, H_B, DH_B), 1.0),
        'cache_b_v': nrm((DEPTH, DEC_BATCH, PAST_LEN, H_B, DH_B), 1.0),
        'ffn1_norm': gain((DEPTH, d)),
        'ffn1_w_gate': nrm((DEPTH, d, f), d ** -0.5),
        'ffn1_w_up': nrm((DEPTH, d, f), d ** -0.5),
        'ffn1_w_down': nrm((DEPTH, f, d), f ** -0.5),
        'mix_norm': gain((DEPTH, d)),
        'w_in': nrm((DEPTH, d, N_IN), d ** -0.5),
        'a_q_norm': gain((DEPTH, DH_A)),
        'a_k_norm': gain((DEPTH, DH_A)),
        'a_lam_q1': nrm((DEPTH, DH_A), 0.1),
        'a_lam_k1': nrm((DEPTH, DH_A), 0.1),
        'a_lam_q2': nrm((DEPTH, DH_A), 0.1),
        'a_lam_k2': nrm((DEPTH, DH_A), 0.1),
        'a_sub_norm': gain((DEPTH, DV_A)),
        'w_branch_a': nrm((DEPTH, A_V, d), A_V ** -0.5),
        'w_branch_b': nrm((DEPTH, B_QKV, d), B_QKV ** -0.5),
        'w_out': nrm((DEPTH, d, d), d ** -0.5),
        'ffn2_norm': gain((DEPTH, d)),
        'ffn2_w_gate': nrm((DEPTH, d, f), d ** -0.5),
        'ffn2_w_up': nrm((DEPTH, d, f), d ** -0.5),
        'ffn2_w_down': nrm((DEPTH, f, d), f ** -0.5),
    }


def reference(x_prompt, x_sample, cache_a_k, cache_a_v, cache_b_k, cache_b_v,
              ffn1_norm, ffn1_w_gate, ffn1_w_up, ffn1_w_down,
              mix_norm, w_in, a_q_norm, a_k_norm,
              a_lam_q1, a_lam_k1, a_lam_q2, a_lam_k2, a_sub_norm,
              w_branch_a, w_branch_b, w_out,
              ffn2_norm, ffn2_w_gate, ffn2_w_up, ffn2_w_down):
    pos_p = jnp.arange(x_prompt.shape[1])
    pos_s = cache_a_k.shape[2] + jnp.arange(x_sample.shape[1])
    yp, ys = x_prompt, x_sample
    rows_p, rows_s = [], []
    for li in range(DEPTH):
        p = {
            'ffn1_norm': ffn1_norm[li], 'ffn1_w_gate': ffn1_w_gate[li],
            'ffn1_w_up': ffn1_w_up[li], 'ffn1_w_down': ffn1_w_down[li],
            'mix_norm': mix_norm[li], 'w_in': w_in[li],
            'a_q_norm': a_q_norm[li], 'a_k_norm': a_k_norm[li],
            'a_lam_q1': a_lam_q1[li], 'a_lam_k1': a_lam_k1[li],
            'a_lam_q2': a_lam_q2[li], 'a_lam_k2': a_lam_k2[li],
            'a_sub_norm': a_sub_norm[li],
            'w_branch_a': w_branch_a[li], 'w_branch_b': w_branch_b[li], 'w_out': w_out[li],
            'ffn2_norm': ffn2_norm[li], 'ffn2_w_gate': ffn2_w_gate[li],
            'ffn2_w_up': ffn2_w_up[li], 'ffn2_w_down': ffn2_w_down[li],
        }
        yp, rp = _layer(yp, pos_p, None, li, p)
        past = (cache_a_k[li], cache_a_v[li], cache_b_k[li], cache_b_v[li])
        ys, rs = _layer(ys, pos_s, past, li, p)
        rows_p.append(rp)
        rows_s.append(rs)
    new_a_k_prompt = jnp.stack([r[0] for r in rows_p], axis=0)
    new_a_v_prompt = jnp.stack([r[1] for r in rows_p], axis=0)
    new_b_k_prompt = jnp.stack([r[2] for r in rows_p], axis=0)
    new_b_v_prompt = jnp.stack([r[3] for r in rows_p], axis=0)
    new_a_k_sample = jnp.stack([r[0] for r in rows_s], axis=0)
    new_a_v_sample = jnp.stack([r[1] for r in rows_s], axis=0)
    new_b_k_sample = jnp.stack([r[2] for r in rows_s], axis=0)
    new_b_v_sample = jnp.stack([r[3] for r in rows_s], axis=0)
    return (yp, ys, new_a_k_prompt, new_a_v_prompt, new_b_k_prompt, new_b_v_prompt,
            new_a_k_sample, new_a_v_sample, new_b_k_sample, new_b_v_sample)
```

```python
import functools
import math

import numpy as np
import jax
import jax.numpy as jnp
from jax import lax
from jax.experimental import pallas as pl
from jax.experimental.pallas import tpu as pltpu

F32 = jnp.float32
BF16 = jnp.bfloat16

CHUNK = 64
ROPE_THETA = 10000.0
EPS = 1e-6
NEG_INF = -1e30
LANES = 128
VMEM_LIMIT = 56 * 1024 * 1024

NT_DIMS = (((1,), (1,)), ((), ()))


def _pick(n, candidates):
    for c in candidates:
        if n % c == 0:
            return c
    return n


def _params(*sem):
    return pltpu.CompilerParams(dimension_semantics=sem, vmem_limit_bytes=VMEM_LIMIT)


def _rmsnorm_rows(x_ref, g_ref, h_ref, copy_ref=None):
    rows = _pick(x_ref.shape[0], (64, 32, 16))
    g = g_ref[...]

    def chunk(c, carry):
        r = pl.ds(pl.multiple_of(c * rows, rows), rows)
        x = x_ref[r, :]
        ms = jnp.mean(x * x, axis=-1, keepdims=True)
        h_ref[r, :] = (x * lax.rsqrt(ms + EPS) * g).astype(BF16)
        if copy_ref is not None:
            copy_ref[r, :] = x
        return carry

    lax.fori_loop(0, x_ref.shape[0] // rows, chunk, 0)


def _ffn_kernel(x_ref, g_ref, wg_ref, wu_ref, wd_ref, o_ref, h_ref):
    @pl.when(pl.program_id(1) == 0)
    def _():
        _rmsnorm_rows(x_ref, g_ref, h_ref, copy_ref=o_ref)

    h = h_ref[...]
    g = jnp.dot(h, wg_ref[...], preferred_element_type=F32)
    u = jnp.dot(h, wu_ref[...], preferred_element_type=F32)
    a = (0.5 * g) * jax.nn.sigmoid(g) * u
    o_ref[...] += jnp.dot(a.astype(BF16), wd_ref[...], preferred_element_type=F32)


def _ffn(x, gain, wg, wu, wd, tm, tf):
    m, d = x.shape
    f = wg.shape[1]
    return pl.pallas_call(
        _ffn_kernel,
        grid=(m // tm, f // tf),
        in_specs=[
            pl.BlockSpec((tm, d), lambda i, j: (i, 0)),
            pl.BlockSpec((1, d), lambda i, j: (0, 0)),
            pl.BlockSpec((d, tf), lambda i, j: (0, j)),
            pl.BlockSpec((d, tf), lambda i, j: (0, j)),
            pl.BlockSpec((tf, d), lambda i, j: (j, 0)),
        ],
        out_specs=pl.BlockSpec((tm, d), lambda i, j: (i, 0)),
        out_shape=jax.ShapeDtypeStruct((m, d), F32),
        scratch_shapes=[pltpu.VMEM((tm, d), BF16)],
        compiler_params=_params("parallel", "arbitrary"),
        name="ffn",
    )(x, gain.reshape(1, d), wg, wu, wd)


def _proj_kernel(x_ref, g_ref, w_ref, cos_ref, sin_ref, qn_ref, kn_ref,
                 o32_ref, o16_ref, h_ref, *, n_heads, dh, scale_a, scale_b):
    j = pl.program_id(1)

    @pl.when(j == 0)
    def _():
        _rmsnorm_rows(x_ref, g_ref, h_ref)

    acc = jnp.dot(h_ref[...], w_ref[...], preferred_element_type=F32)

    def norm_rope(gain_ref, scale):
        cos = cos_ref[...]
        sin = sin_ref[...]
        outs = []
        for hd in range(n_heads):
            xh = acc[:, hd * dh:(hd + 1) * dh]
            ms = jnp.mean(xh * xh, axis=-1, keepdims=True)
            y = xh * lax.rsqrt(ms + EPS) * gain_ref[...]
            y = y * cos + pltpu.roll(y, dh // 2, 1) * sin
            if scale is not None:
                y = y * scale
            outs.append(y)
        return jnp.concatenate(outs, axis=1)

    @pl.when(j == 0)
    def _():
        o16_ref[...] = norm_rope(qn_ref, scale_a).astype(BF16)

    @pl.when(j == 1)
    def _():
        y = norm_rope(kn_ref, None)
        o32_ref[...] = y
        o16_ref[...] = y.astype(BF16)

    @pl.when((j == 2) | (j == 4) | (j == 5))
    def _():
        o32_ref[...] = acc
        o16_ref[...] = acc.astype(BF16)

    @pl.when(j == 3)
    def _():
        o16_ref[...] = (acc * scale_b).astype(BF16)

    @pl.when(j >= 6)
    def _():
        o32_ref[...] = jax.nn.sigmoid(acc)


def _proj(x, gain, w_in, cos, sin, qn, kn, tm, seg, n_heads, dh, scale_a, scale_b):
    m, d = x.shape
    n = w_in.shape[1]
    nj = n // seg
    assert nj == 10 and n_heads * dh == seg

    def o32_map(i, j):
        return (i, jnp.maximum(j - 1 - (j >= 3).astype(jnp.int32), 0))

    def o16_map(i, j):
        return (i, jnp.minimum(j, 5))

    kern = functools.partial(_proj_kernel, n_heads=n_heads, dh=dh,
                             scale_a=scale_a, scale_b=scale_b)
    return pl.pallas_call(
        kern,
        grid=(m // tm, nj),
        in_specs=[
            pl.BlockSpec((tm, d), lambda i, j: (i, 0)),
            pl.BlockSpec((1, d), lambda i, j: (0, 0)),
            pl.BlockSpec((d, seg), lambda i, j: (0, j)),
            pl.BlockSpec((tm, dh), lambda i, j: (i, 0)),
            pl.BlockSpec((tm, dh), lambda i, j: (i, 0)),
            pl.BlockSpec((1, dh), lambda i, j: (0, 0)),
            pl.BlockSpec((1, dh), lambda i, j: (0, 0)),
        ],
        out_specs=[
            pl.BlockSpec((tm, seg), o32_map),
            pl.BlockSpec((tm, seg), o16_map),
        ],
        out_shape=[
            jax.ShapeDtypeStruct((m, 8 * seg), F32),
            jax.ShapeDtypeStruct((m, 6 * seg), BF16),
        ],
        scratch_shapes=[pltpu.VMEM((tm, d), BF16)],
        compiler_params=_params("parallel", "arbitrary"),
        name="proj",
    )(x, gain.reshape(1, d), w_in, cos, sin, qn.reshape(1, dh), kn.reshape(1, dh))


def _lam(lam_ref, lam_init):
    v = lam_ref[...]
    s1 = jnp.sum(v[0:1] * v[1:2], axis=1, keepdims=True)
    s2 = jnp.sum(v[2:3] * v[3:4], axis=1, keepdims=True)
    return jnp.exp(s1) - jnp.exp(s2) + lam_init


def _diff_combine(o0, o1, lam, gsub, lam_init):
    o = o0 - lam * o1
    ms = jnp.mean(o * o, axis=-1, keepdims=True)
    return (o * lax.rsqrt(ms + EPS) * gsub) * (1.0 - lam_init)


def _wide(col, n):
    reps = n // LANES
    return col if reps == 1 else jnp.concatenate([col] * reps, axis=1)


def _sb_block(z, v, tri, run, mask):
    l = jnp.log(1.0 + jnp.exp(-jnp.abs(z)))
    lsig = jnp.minimum(z, 0.0) - l
    lk = lsig - z
    if mask is not None:
        lk = jnp.where(mask, lk, 0.0)
    hi = lk.astype(BF16)
    lo = (lk - hi.astype(F32)).astype(BF16)
    after = (jnp.dot(hi, tri, preferred_element_type=F32)
             + jnp.dot(lo, tri, preferred_element_type=F32))
    a = jnp.exp(lsig + after + _wide(run, z.shape[1]))
    if mask is not None:
        a = jnp.where(mask, a, 0.0)
    contrib = jnp.dot(a.astype(BF16), v, preferred_element_type=F32)
    return contrib, run + jnp.sum(lk, axis=1, keepdims=True)


def _attn_a_kernel(qi_ref, kj_ref, q_ref, k_ref, v_ref, lam_ref, gsub_ref, o_ref,
                   m_scr, l_scr, acc_scr, *, n_sub, dh, dv, lam_init):
    step = pl.program_id(0)
    qi = qi_ref[step]
    kj = kj_ref[step]
    tq = q_ref.shape[0]
    tk = k_ref.shape[0]

    @pl.when(kj == 0)
    def _():
        m_scr[...] = jnp.full(m_scr.shape, NEG_INF, F32)
        l_scr[...] = jnp.zeros(l_scr.shape, F32)
        acc_scr[...] = jnp.zeros(acc_scr.shape, F32)

    def body(masked):
        if masked:
            r = lax.broadcasted_iota(jnp.int32, (tq, tk), 0)
            c = lax.broadcasted_iota(jnp.int32, (tq, tk), 1)
            mask = (c // CHUNK) <= (r // CHUNK)
        for h in range(n_sub):
            q = q_ref[:, h * dh:(h + 1) * dh]
            k = k_ref[:, h * dh:(h + 1) * dh]
            v = v_ref[:, (h // 2) * dv:(h // 2 + 1) * dv]
            s = lax.dot_general(q, k, NT_DIMS, preferred_element_type=F32)
            if masked:
                s = jnp.where(mask, s, NEG_INF)
            m_prev = m_scr[h]
            m_new = jnp.maximum(m_prev, jnp.max(s, axis=1, keepdims=True))
            alpha = jnp.exp(m_prev - m_new)
            p = jnp.exp(s - _wide(m_new, tk))
            l_scr[h] = alpha * l_scr[h] + jnp.sum(p, axis=1, keepdims=True)
            m_scr[h] = m_new
            pv = jnp.dot(p.astype(BF16), v, preferred_element_type=F32)
            acc_scr[h] = _wide(alpha, dv) * acc_scr[h] + pv

    @pl.when(kj < qi)
    def _():
        body(False)

    @pl.when(kj == qi)
    def _():
        body(True)
        lam = _lam(lam_ref, lam_init)
        gsub = gsub_ref[...]
        for hd in range(n_sub // 2):
            o0 = acc_scr[2 * hd] / _wide(l_scr[2 * hd], dv)
            o1 = acc_scr[2 * hd + 1] / _wide(l_scr[2 * hd + 1], dv)
            o = _diff_combine(o0, o1, lam, gsub, lam_init)
            o_ref[:, hd * dv:(hd + 1) * dv] = o.astype(BF16)


def _tri_pairs(nq, descending):
    qi, kj = [], []
    for i in range(nq):
        ks = range(i, -1, -1) if descending else range(i + 1)
        for k in ks:
            qi.append(i)
            kj.append(k)
    return jnp.asarray(np.array(qi, np.int32)), jnp.asarray(np.array(kj, np.int32))


def _attn_a_prompt(qkv16, oa, lam_vecs, gsub, n_rows, tq, seg, n_sub, dh, dv, lam_init):
    nq = n_rows // tq
    qi, kj = _tri_pairs(nq, descending=False)
    kern = functools.partial(_attn_a_kernel, n_sub=n_sub, dh=dh, dv=dv, lam_init=lam_init)
    grid_spec = pltpu.PrefetchScalarGridSpec(
        num_scalar_prefetch=2,
        grid=(int(qi.shape[0]),),
        in_specs=[
            pl.BlockSpec((tq, seg), lambda s, qi, kj: (qi[s], 0)),
            pl.BlockSpec((tq, seg), lambda s, qi, kj: (kj[s], 1)),
            pl.BlockSpec((tq, seg), lambda s, qi, kj: (kj[s], 2)),
            pl.BlockSpec((4, dh), lambda s, qi, kj: (0, 0)),
            pl.BlockSpec((1, dv), lambda s, qi, kj: (0, 0)),
            pl.BlockSpec(memory_space=pl.ANY),
        ],
        out_specs=pl.BlockSpec((tq, seg), lambda s, qi, kj: (qi[s], 0)),
        scratch_shapes=[
            pltpu.VMEM((n_sub, tq, LANES), F32),
            pltpu.VMEM((n_sub, tq, LANES), F32),
            pltpu.VMEM((n_sub, tq, dv), F32),
        ],
    )

    def kern_alias(qi_ref, kj_ref, q_ref, k_ref, v_ref, lam_ref, gsub_ref, oa_in_ref, o_ref, *scr):
        del oa_in_ref
        kern(qi_ref, kj_ref, q_ref, k_ref, v_ref, lam_ref, gsub_ref, o_ref, *scr)

    return pl.pallas_call(
        kern_alias,
        grid_spec=grid_spec,
        out_shape=jax.ShapeDtypeStruct(oa.shape, BF16),
        input_output_aliases={7: 0},
        compiler_params=_params("arbitrary"),
        name="attn_a_prompt",
    )(qi, kj, qkv16, qkv16, qkv16, lam_vecs, gsub.reshape(1, dv), oa)


def _attn_b_kernel(qi_ref, kj_ref, q_ref, k_ref, v_ref, tri_ref, o_ref,
                   run_scr, acc_scr, *, n_heads, dh):
    step = pl.program_id(0)
    qi = qi_ref[step]
    kj = kj_ref[step]
    tq = q_ref.shape[0]
    tk = k_ref.shape[0]

    def body(masked):
        mask = None
        if masked:
            r = lax.broadcasted_iota(jnp.int32, (tq, tk), 0)
            c = lax.broadcasted_iota(jnp.int32, (tq, tk), 1)
            mask = c < r
        tri = tri_ref[...]
        for h in range(n_heads):
            q = q_ref[:, h * dh:(h + 1) * dh]
            k = k_ref[:, h * dh:(h + 1) * dh]
            v = v_ref[:, h * dh:(h + 1) * dh]
            z = lax.dot_general(q, k, NT_DIMS, preferred_element_type=F32)
            contrib, run = _sb_block(z, v, tri, run_scr[h], mask)
            run_scr[h] = run
            acc_scr[:, h * dh:(h + 1) * dh] += contrib

    @pl.when(kj == qi)
    def _():
        run_scr[...] = jnp.zeros(run_scr.shape, F32)
        acc_scr[...] = jnp.zeros(acc_scr.shape, F32)
        body(True)

    @pl.when(kj < qi)
    def _():
        body(False)

    @pl.when(kj == 0)
    def _():
        o_ref[...] = acc_scr[...].astype(BF16)


def _attn_b_prompt(qkv16, ob, tri, n_rows, tq, seg, n_heads, dh):
    nq = n_rows // tq
    qi, kj = _tri_pairs(nq, descending=True)
    kern = functools.partial(_attn_b_kernel, n_heads=n_heads, dh=dh)
    grid_spec = pltpu.PrefetchScalarGridSpec(
        num_scalar_prefetch=2,
        grid=(int(qi.shape[0]),),
        in_specs=[
            pl.BlockSpec((tq, seg), lambda s, qi, kj: (qi[s], 3)),
            pl.BlockSpec((tq, seg), lambda s, qi, kj: (kj[s], 4)),
            pl.BlockSpec((tq, seg), lambda s, qi, kj: (kj[s], 5)),
            pl.BlockSpec((tq, tq), lambda s, qi, kj: (0, 0)),
            pl.BlockSpec(memory_space=pl.ANY),
        ],
        out_specs=pl.BlockSpec((tq, seg), lambda s, qi, kj: (qi[s], 0)),
        scratch_shapes=[
            pltpu.VMEM((n_heads, tq, LANES), F32),
            pltpu.VMEM((tq, seg), F32),
        ],
    )

    def kern_alias(qi_ref, kj_ref, q_ref, k_ref, v_ref, tri_ref, ob_in_ref, o_ref, *scr):
        del ob_in_ref
        kern(qi_ref, kj_ref, q_ref, k_ref, v_ref, tri_ref, o_ref, *scr)

    return pl.pallas_call(
        kern_alias,
        grid_spec=grid_spec,
        out_shape=jax.ShapeDtypeStruct(ob.shape, BF16),
        input_output_aliases={6: 0},
        compiler_params=_params("arbitrary"),
        name="attn_b_prompt",
    )(qi, kj, qkv16, qkv16, qkv16, tri, ob)


def _pad_rows(x, rows):
    n = x.shape[0]
    if n == rows:
        return x
    return jnp.concatenate([x, jnp.zeros((rows - n, x.shape[1]), x.dtype)], axis=0)


def _attn_a_sample_kernel(q_ref, kn_ref, vn_ref, kp_ref, vp_ref, lam_ref, gsub_ref, o_ref,
                          *, n_sub, dh, dv, lam_init, past_len):
    sq = q_ref.shape[0]
    kp = kp_ref[...].astype(BF16)
    vp = vp_ref[...].astype(BF16)
    kn = _pad_rows(kn_ref[...], LANES)
    vn = _pad_rows(vn_ref[...], LANES)
    r = lax.broadcasted_iota(jnp.int32, (sq, LANES), 0)
    c = lax.broadcasted_iota(jnp.int32, (sq, LANES), 1)
    mask_new = ((past_len + c) // CHUNK <= (past_len + r) // CHUNK) & (c < sq)
    lam = _lam(lam_ref, lam_init)
    gsub = gsub_ref[...]
    outs = []
    for h in range(n_sub):
        q = q_ref[:, h * dh:(h + 1) * dh]
        sp = lax.dot_general(q, kp[:, h * dh:(h + 1) * dh], NT_DIMS, preferred_element_type=F32)
        sn = lax.dot_general(q, kn[:, h * dh:(h + 1) * dh], NT_DIMS, preferred_element_type=F32)
        sn = jnp.where(mask_new, sn, NEG_INF)
        m = jnp.maximum(jnp.max(sp, axis=1, keepdims=True), jnp.max(sn, axis=1, keepdims=True))
        pp = jnp.exp(sp - m)
        pn = jnp.exp(sn - m)
        l = jnp.sum(pp, axis=1, keepdims=True) + jnp.sum(pn, axis=1, keepdims=True)
        vs = slice((h // 2) * dv, (h // 2 + 1) * dv)
        o = (jnp.dot(pp.astype(BF16), vp[:, vs], preferred_element_type=F32)
             + jnp.dot(pn.astype(BF16), vn[:, vs], preferred_element_type=F32))
        outs.append(o / l)
    for hd in range(n_sub // 2):
        o = _diff_combine(outs[2 * hd], outs[2 * hd + 1], lam, gsub, lam_init)
        o_ref[:, hd * dv:(hd + 1) * dv] = o.astype(BF16)


def _attn_a_sample(qkv16, oa, cache_k, cache_v, li, lam_vecs, gsub, row0, sq, seg,
                   n_sub, dh, dv, lam_init):
    _, nb, past_len, _ = cache_k.shape
    blk0 = row0 // sq
    kern = functools.partial(_attn_a_sample_kernel, n_sub=n_sub, dh=dh, dv=dv,
                             lam_init=lam_init, past_len=past_len)

    def kern_alias(q_ref, kn_ref, vn_ref, kp_ref, vp_ref, lam_ref, gsub_ref, oa_in_ref, o_ref):
        del oa_in_ref
        kern(q_ref, kn_ref, vn_ref, kp_ref, vp_ref, lam_ref, gsub_ref, o_ref)

    return pl.pallas_call(
        kern_alias,
        grid=(nb,),
        in_specs=[
            pl.BlockSpec((sq, seg), lambda b: (blk0 + b, 0)),
            pl.BlockSpec((sq, seg), lambda b: (blk0 + b, 1)),
            pl.BlockSpec((sq, seg), lambda b: (blk0 + b, 2)),
            pl.BlockSpec((None, None, past_len, seg), lambda b: (li, b, 0, 0)),
            pl.BlockSpec((None, None, past_len, seg), lambda b: (li, b, 0, 0)),
            pl.BlockSpec((4, dh), lambda b: (0, 0)),
            pl.BlockSpec((1, dv), lambda b: (0, 0)),
            pl.BlockSpec(memory_space=pl.ANY),
        ],
        out_specs=pl.BlockSpec((sq, seg), lambda b: (blk0 + b, 0)),
        out_shape=jax.ShapeDtypeStruct(oa.shape, BF16),
        input_output_aliases={7: 0},
        compiler_params=_params("arbitrary"),
        name="attn_a_sample",
    )(qkv16, qkv16, qkv16, cache_k, cache_v, lam_vecs, gsub.reshape(1, dv), oa)


def _attn_b_sample_kernel(q_ref, kn_ref, vn_ref, kp_ref, vp_ref, tri_ref, o_ref,
                          *, n_heads, dh, tk):
    sq = q_ref.shape[0]
    past_len = kp_ref.shape[0]
    kp = kp_ref[...].astype(BF16)
    vp = vp_ref[...].astype(BF16)
    kn = _pad_rows(kn_ref[...], LANES)
    vn = _pad_rows(vn_ref[...], LANES)
    r = lax.broadcasted_iota(jnp.int32, (sq, LANES), 0)
    c = lax.broadcasted_iota(jnp.int32, (sq, LANES), 1)
    mask_new = c < r
    tri = tri_ref[...]
    tri_new = tri_ref[0:LANES, 0:LANES]
    for h in range(n_heads):
        hs = slice(h * dh, (h + 1) * dh)
        q = q_ref[:, hs]
        run = jnp.zeros((sq, LANES), F32)
        z = lax.dot_general(q, kn[:, hs], NT_DIMS, preferred_element_type=F32)
        acc, run = _sb_block(z, vn[:, hs], tri_new, run, mask_new)
        for blk in range(past_len // tk - 1, -1, -1):
            ks = slice(blk * tk, (blk + 1) * tk)
            z = lax.dot_general(q, kp[ks, hs], NT_DIMS, preferred_element_type=F32)
            contrib, run = _sb_block(z, vp[ks, hs], tri, run, None)
            acc = acc + contrib
        o_ref[:, hs] = acc.astype(BF16)


def _attn_b_sample(qkv16, ob, cache_k, cache_v, li, tri, row0, sq, seg, n_heads, dh):
    _, nb, past_len, _ = cache_k.shape
    tk = tri.shape[0]
    assert past_len % tk == 0 and tk >= LANES
    blk0 = row0 // sq
    kern = functools.partial(_attn_b_sample_kernel, n_heads=n_heads, dh=dh, tk=tk)

    def kern_alias(q_ref, kn_ref, vn_ref, kp_ref, vp_ref, tri_ref, ob_in_ref, o_ref):
        del ob_in_ref
        kern(q_ref, kn_ref, vn_ref, kp_ref, vp_ref, tri_ref, o_ref)

    return pl.pallas_call(
        kern_alias,
        grid=(nb,),
        in_specs=[
            pl.BlockSpec((sq, seg), lambda b: (blk0 + b, 3)),
            pl.BlockSpec((sq, seg), lambda b: (blk0 + b, 4)),
            pl.BlockSpec((sq, seg), lambda b: (blk0 + b, 5)),
            pl.BlockSpec((None, None, past_len, seg), lambda b: (li, b, 0, 0)),
            pl.BlockSpec((None, None, past_len, seg), lambda b: (li, b, 0, 0)),
            pl.BlockSpec((tk, tk), lambda b: (0, 0)),
            pl.BlockSpec(memory_space=pl.ANY),
        ],
        out_specs=pl.BlockSpec((sq, seg), lambda b: (blk0 + b, 0)),
        out_shape=jax.ShapeDtypeStruct(ob.shape, BF16),
        input_output_aliases={6: 0},
        compiler_params=_params("arbitrary"),
        name="attn_b_sample",
    )(qkv16, qkv16, qkv16, cache_k, cache_v, tri, ob)


def _merge_kernel(x_ref, oa_ref, ob_ref, ga_ref, gb_ref, wba_ref, wbb_ref, wo_ref, o_ref):
    @pl.when(pl.program_id(1) == 0)
    def _():
        o_ref[...] = x_ref[...]

    ta = jnp.dot(oa_ref[...], wba_ref[...], preferred_element_type=F32)
    tb = jnp.dot(ob_ref[...], wbb_ref[...], preferred_element_type=F32)
    merged = ga_ref[...] * ta + gb_ref[...] * tb
    o_ref[...] += jnp.dot(merged.astype(BF16), wo_ref[...], preferred_element_type=F32)


def _merge(x, oa, ob, proj32, wba, wbb, wo, tm, tn, gate_col0):
    m, d = x.shape
    ka = oa.shape[1]
    kb = ob.shape[1]
    nj = d // tn
    ga0 = gate_col0 // tn
    gb0 = ga0 + nj
    return pl.pallas_call(
        _merge_kernel,
        grid=(m // tm, nj),
        in_specs=[
            pl.BlockSpec((tm, d), lambda i, j: (i, 0)),
            pl.BlockSpec((tm, ka), lambda i, j: (i, 0)),
            pl.BlockSpec((tm, kb), lambda i, j: (i, 0)),
            pl.BlockSpec((tm, tn), lambda i, j: (i, ga0 + j)),
            pl.BlockSpec((tm, tn), lambda i, j: (i, gb0 + j)),
            pl.BlockSpec((ka, tn), lambda i, j: (0, j)),
            pl.BlockSpec((kb, tn), lambda i, j: (0, j)),
            pl.BlockSpec((tn, d), lambda i, j: (j, 0)),
        ],
        out_specs=pl.BlockSpec((tm, d), lambda i, j: (i, 0)),
        out_shape=jax.ShapeDtypeStruct((m, d), F32),
        compiler_params=_params("parallel", "arbitrary"),
        name="merge",
    )(x, oa, ob, proj32, proj32, wba, wbb, wo)


def _rope_tables(pos, dh):
    inv = 1.0 / (ROPE_THETA ** (jnp.arange(0, dh, 2, dtype=F32) / dh))
    ang = pos.astype(F32)[:, None] * inv[None, :]
    cos = jnp.cos(ang)
    sin = jnp.sin(ang)
    return jnp.concatenate([cos, cos], axis=1), jnp.concatenate([-sin, sin], axis=1)


def kernel(x_prompt, x_sample, cache_a_k, cache_a_v, cache_b_k, cache_b_v,
           ffn1_norm, ffn1_w_gate, ffn1_w_up, ffn1_w_down,
           mix_norm, w_in, a_q_norm, a_k_norm,
           a_lam_q1, a_lam_k1, a_lam_q2, a_lam_k2, a_sub_norm,
           w_branch_a, w_branch_b, w_out,
           ffn2_norm, ffn2_w_gate, ffn2_w_up, ffn2_w_down):
    bp, sp, d = x_prompt.shape
    bs, ss, _ = x_sample.shape
    depth, _, past_len, n_sub, dh_a = cache_a_k.shape
    h_a, dv_a = cache_a_v.shape[3:]
    h_b, dh_b = cache_b_k.shape[3:]
    seg = n_sub * dh_a
    assert bp == 1 and h_a * dv_a == seg and h_b * dh_b == seg and n_sub == 2 * h_a
    assert w_in.shape[2] == 6 * seg + 2 * d and d == 2 * seg

    n_p = bp * sp
    m = n_p + bs * ss
    tm = _pick(m, (768, 512, 384, 256, 128, 64, 32))
    tf = _pick(ffn1_w_gate.shape[2], (512, 256, 128))
    tq = _pick(sp, (256, 128))
    tn = _pick(d, (512, 256, 128))
    assert n_p % tq == 0 and n_p % ss == 0 and tq % CHUNK == 0

    x = jnp.concatenate([x_prompt.reshape(n_p, d), x_sample.reshape(bs * ss, d)], axis=0)
    pos = jnp.concatenate([jnp.arange(sp)] * bp + [past_len + jnp.arange(ss)] * bs)
    cos, sin = _rope_tables(pos, dh_a)
    tri = jnp.asarray(np.tril(np.ones((tq, tq), np.float32), -1), BF16)

    ck_a = cache_a_k.reshape(depth, bs, past_len, seg)
    cv_a = cache_a_v.reshape(depth, bs, past_len, seg)
    ck_b = cache_b_k.reshape(depth, bs, past_len, seg)
    cv_b = cache_b_v.reshape(depth, bs, past_len, seg)

    bf = lambda w: w.astype(BF16)
    rows = []
    for li in range(depth):
        lam_init = 0.8 - 0.6 * math.exp(-0.3 * li)
        lam_vecs = jnp.stack([a_lam_q1[li], a_lam_k1[li], a_lam_q2[li], a_lam_k2[li]], axis=0)

        x = _ffn(x, ffn1_norm[li], bf(ffn1_w_gate[li]), bf(ffn1_w_up[li]), bf(ffn1_w_down[li]),
                 tm, tf)
        proj32, qkv16 = _proj(x, mix_norm[li], bf(w_in[li]), cos, sin, a_q_norm[li], a_k_norm[li],
                              tm, seg, n_sub, dh_a, dh_a ** -0.5, dh_b ** -0.5)

        oa = jnp.zeros((m, seg), BF16)
        oa = _attn_a_prompt(qkv16, oa, lam_vecs, a_sub_norm[li], n_p, tq, seg,
                            n_sub, dh_a, dv_a, lam_init)
        oa = _attn_a_sample(qkv16, oa, ck_a, cv_a, li, lam_vecs, a_sub_norm[li], n_p, ss, seg,
                            n_sub, dh_a, dv_a, lam_init)
        ob = jnp.zeros((m, seg), BF16)
        ob = _attn_b_prompt(qkv16, ob, tri, n_p, tq, seg, h_b, dh_b)
        ob = _attn_b_sample(qkv16, ob, ck_b, cv_b, li, tri, n_p, ss, seg, h_b, dh_b)

        x = _merge(x, oa, ob, proj32, bf(w_branch_a[li]), bf(w_branch_b[li]), bf(w_out[li]),
                   tm, tn, 4 * seg)
        x = _ffn(x, ffn2_norm[li], bf(ffn2_w_gate[li]), bf(ffn2_w_up[li]), bf(ffn2_w_down[li]),
                 tm, tf)
        rows.append(proj32)

    def cache_rows(col, n_heads, dh):
        pr = jnp.stack([r[:n_p, col * seg:(col + 1) * seg] for r in rows], axis=0)
        sm = jnp.stack([r[n_p:, col * seg:(col + 1) * seg] for r in rows], axis=0)
        return (pr.reshape(depth, bp, sp, n_heads, dh), sm.reshape(depth, bs, ss, n_heads, dh))

    ak_p, ak_s = cache_rows(0, n_sub, dh_a)
    av_p, av_s = cache_rows(1, h_a, dv_a)
    bk_p, bk_s = cache_rows(2, h_b, dh_b)
    bv_p, bv_s = cache_rows(3, h_b, dh_b)
    return (x[:n_p].reshape(bp, sp, d), x[n_p:].reshape(bs, ss, d),
            ak_p, av_p, bk_p, bv_p, ak_s, av_s, bk_s, bv_s)
```

```python
import functools
import math

import numpy as np
import jax
import jax.numpy as jnp
from jax import lax
from jax.experimental import pallas as pl
from jax.experimental.pallas import tpu as pltpu

F32 = jnp.float32
BF16 = jnp.bfloat16

CHUNK = 64
ROPE_THETA = 10000.0
EPS = 1e-6
NEG_INF = -1e30
LOG2E = math.log2(math.e)
LANES = 128
VMEM_LIMIT = 56 * 1024 * 1024

NT_DIMS = (((1,), (1,)), ((), ()))


def _pick(n, candidates):
    for c in candidates:
        if n % c == 0:
            return c
    return n


def _params(*sem):
    return pltpu.CompilerParams(dimension_semantics=sem, vmem_limit_bytes=VMEM_LIMIT)


def _rmsnorm_rows(x_ref, g_ref, h_ref, copy_ref=None):
    rows = _pick(x_ref.shape[0], (64, 32, 16))
    g = g_ref[...]

    def chunk(c, carry):
        r = pl.ds(pl.multiple_of(c * rows, rows), rows)
        x = x_ref[r, :]
        ms = jnp.mean(x * x, axis=-1, keepdims=True)
        h_ref[r, :] = (x * lax.rsqrt(ms + EPS) * g).astype(BF16)
        if copy_ref is not None:
            copy_ref[r, :] = x
        return carry

    lax.fori_loop(0, x_ref.shape[0] // rows, chunk, 0)


def _ffn_kernel(x_ref, g_ref, wg_ref, wu_ref, wd_ref, o_ref, h_ref):
    @pl.when(pl.program_id(1) == 0)
    def _():
        _rmsnorm_rows(x_ref, g_ref, h_ref, copy_ref=o_ref)

    h = h_ref[...]
    g = jnp.dot(h, wg_ref[...], preferred_element_type=F32)
    u = jnp.dot(h, wu_ref[...], preferred_element_type=F32)
    a = (0.5 * g) * jax.nn.sigmoid(g) * u
    o_ref[...] += jnp.dot(a.astype(BF16), wd_ref[...], preferred_element_type=F32)


def _ffn(x, gain, wg, wu, wd, li, tm, tf):
    m, d = x.shape
    f = wg.shape[2]
    return pl.pallas_call(
        _ffn_kernel,
        grid=(m // tm, f // tf),
        in_specs=[
            pl.BlockSpec((tm, d), lambda i, j: (i, 0)),
            pl.BlockSpec((None, 1, d), lambda i, j: (li, 0, 0)),
            pl.BlockSpec((None, d, tf), lambda i, j: (li, 0, j)),
            pl.BlockSpec((None, d, tf), lambda i, j: (li, 0, j)),
            pl.BlockSpec((None, tf, d), lambda i, j: (li, j, 0)),
        ],
        out_specs=pl.BlockSpec((tm, d), lambda i, j: (i, 0)),
        out_shape=jax.ShapeDtypeStruct((m, d), F32),
        scratch_shapes=[pltpu.VMEM((tm, d), BF16)],
        compiler_params=_params("parallel", "arbitrary"),
        name="ffn",
    )(x, gain.reshape(gain.shape[0], 1, d), wg, wu, wd)


def _proj_kernel(x_ref, g_ref, w_ref, cos_ref, sin_ref, qn_ref, kn_ref,
                 o32_ref, o16_ref, h_ref, *, n_heads, dh, scale_a, scale_b):
    j = pl.program_id(1)

    @pl.when(j == 0)
    def _():
        _rmsnorm_rows(x_ref, g_ref, h_ref)

    acc = jnp.dot(h_ref[...], w_ref[...], preferred_element_type=F32)

    def norm_rope(gain_ref, scale):
        cos = cos_ref[...]
        sin = sin_ref[...]
        outs = []
        for hd in range(n_heads):
            xh = acc[:, hd * dh:(hd + 1) * dh]
            ms = jnp.mean(xh * xh, axis=-1, keepdims=True)
            y = xh * lax.rsqrt(ms + EPS) * gain_ref[...]
            y = y * cos + pltpu.roll(y, dh // 2, 1) * sin
            if scale is not None:
                y = y * scale
            outs.append(y)
        return jnp.concatenate(outs, axis=1)

    @pl.when(j == 0)
    def _():
        o16_ref[...] = norm_rope(qn_ref, scale_a).astype(BF16)

    @pl.when(j == 1)
    def _():
        y = norm_rope(kn_ref, None)
        o32_ref[...] = y
        o16_ref[...] = y.astype(BF16)

    @pl.when((j == 2) | (j == 4) | (j == 5))
    def _():
        o32_ref[...] = acc
        o16_ref[...] = acc.astype(BF16)

    @pl.when(j == 3)
    def _():
        o16_ref[...] = (acc * scale_b).astype(BF16)

    @pl.when(j >= 6)
    def _():
        o32_ref[...] = jax.nn.sigmoid(acc)


def _proj(x, gain, w_in, cos, sin, qn, kn, li, tm, seg, n_heads, dh, scale_a, scale_b):
    m, d = x.shape
    n = w_in.shape[2]
    nj = n // seg
    assert nj == 10 and n_heads * dh == seg

    def o32_map(i, j):
        return (i, jnp.maximum(j - 1 - (j >= 3).astype(jnp.int32), 0))

    def o16_map(i, j):
        return (i, jnp.minimum(j, 5))

    kern = functools.partial(_proj_kernel, n_heads=n_heads, dh=dh,
                             scale_a=scale_a, scale_b=scale_b)
    return pl.pallas_call(
        kern,
        grid=(m // tm, nj),
        in_specs=[
            pl.BlockSpec((tm, d), lambda i, j: (i, 0)),
            pl.BlockSpec((None, 1, d), lambda i, j: (li, 0, 0)),
            pl.BlockSpec((None, d, seg), lambda i, j: (li, 0, j)),
            pl.BlockSpec((tm, dh), lambda i, j: (i, 0)),
            pl.BlockSpec((tm, dh), lambda i, j: (i, 0)),
            pl.BlockSpec((None, 1, dh), lambda i, j: (li, 0, 0)),
            pl.BlockSpec((None, 1, dh), lambda i, j: (li, 0, 0)),
        ],
        out_specs=[
            pl.BlockSpec((tm, seg), o32_map),
            pl.BlockSpec((tm, seg), o16_map),
        ],
        out_shape=[
            jax.ShapeDtypeStruct((m, 8 * seg), F32),
            jax.ShapeDtypeStruct((m, 6 * seg), BF16),
        ],
        scratch_shapes=[pltpu.VMEM((tm, d), BF16)],
        compiler_params=_params("parallel", "arbitrary"),
        name="proj",
    )(x, gain.reshape(gain.shape[0], 1, d), w_in, cos, sin,
      qn.reshape(qn.shape[0], 1, dh), kn.reshape(kn.shape[0], 1, dh))


def _lam(lam_ref, lam_init):
    v = lam_ref[...]
    s1 = jnp.sum(v[0:1] * v[1:2], axis=1, keepdims=True)
    s2 = jnp.sum(v[2:3] * v[3:4], axis=1, keepdims=True)
    return jnp.exp(s1) - jnp.exp(s2) + lam_init


def _diff_combine(o0, o1, lam, gsub, lam_init):
    o = o0 - lam * o1
    ms = jnp.mean(o * o, axis=-1, keepdims=True)
    return (o * lax.rsqrt(ms + EPS) * gsub) * (1.0 - lam_init)


def _wide(col, n):
    reps = n // LANES
    return col if reps == 1 else jnp.concatenate([col] * reps, axis=1)


def _sb_logs(z, mask):
    l = jnp.log(1.0 + jnp.exp2(-jnp.abs(z))) * LOG2E
    lsig = jnp.minimum(z, 0.0) - l
    lk = lsig - z
    if mask is not None:
        lk = jnp.where(mask, lk, 0.0)
    return lsig, lk


def _split_bf16(x):
    hi = x.astype(BF16)
    lo = (x - hi.astype(F32)).astype(BF16)
    return hi, lo


def _sb_block(z, v, tri, run, mask):
    lsig, lk = _sb_logs(z, mask)
    hi, lo = _split_bf16(lk)
    after = (jnp.dot(hi, tri, preferred_element_type=F32)
             + jnp.dot(lo, tri, preferred_element_type=F32))
    a = jnp.exp2(lsig + after + _wide(run, z.shape[1]))
    if mask is not None:
        a = jnp.where(mask, a, 0.0)
    contrib = jnp.dot(a.astype(BF16), v, preferred_element_type=F32)
    return contrib, run + jnp.sum(lk, axis=1, keepdims=True)


def _tri_pairs(nq, descending):
    qi, kj = [], []
    for i in range(nq):
        ks = range(i, -1, -1) if descending else range(i + 1)
        for k in ks:
            qi.append(i)
            kj.append(k)
    return jnp.asarray(np.array(qi, np.int32)), jnp.asarray(np.array(kj, np.int32))


def _attn_a_kernel(qi_ref, kj_ref, q_ref, k_ref, v_ref, lam_ref, gsub_ref, o_ref,
                   m_scr, l_scr, acc_scr, s_scr, p_scr, al_scr, *, n_sub, dh, dv, lam_init):
    step = pl.program_id(0)
    qi = qi_ref[step]
    kj = kj_ref[step]
    tq = q_ref.shape[0]
    tk = k_ref.shape[0]

    @pl.when(kj == 0)
    def _():
        m_scr[...] = jnp.full(m_scr.shape, NEG_INF, F32)
        l_scr[...] = jnp.zeros(l_scr.shape, F32)
        acc_scr[...] = jnp.zeros(acc_scr.shape, F32)

    def body(masked):
        if masked:
            r = lax.broadcasted_iota(jnp.int32, (tq, tk), 0)
            c = lax.broadcasted_iota(jnp.int32, (tq, tk), 1)
            mask = (c // CHUNK) <= (r // CHUNK)
        for h in range(n_sub):
            q = q_ref[:, h * dh:(h + 1) * dh]
            k = k_ref[:, h * dh:(h + 1) * dh]
            s = lax.dot_general(q, k, NT_DIMS, preferred_element_type=F32)
            if masked:
                s = jnp.where(mask, s, NEG_INF)
            s_scr[h] = s
        for h in range(n_sub):
            s = s_scr[h]
            m_prev = m_scr[h]
            m_new = jnp.maximum(m_prev, jnp.max(s, axis=1, keepdims=True))
            alpha = jnp.exp2(m_prev - m_new)
            p = jnp.exp2(s - _wide(m_new, tk))
            l_scr[h] = alpha * l_scr[h] + jnp.sum(p, axis=1, keepdims=True)
            m_scr[h] = m_new
            al_scr[h] = alpha
            p_scr[h * tq:(h + 1) * tq, :] = p.astype(BF16)
        for hd in range(n_sub // 2):
            v = v_ref[:, hd * dv:(hd + 1) * dv]
            pv = jnp.dot(p_scr[2 * hd * tq:(2 * hd + 2) * tq, :], v, preferred_element_type=F32)
            for u in range(2):
                h = 2 * hd + u
                acc_scr[h] = _wide(al_scr[h], dv) * acc_scr[h] + pv[u * tq:(u + 1) * tq]

    @pl.when(kj < qi)
    def _():
        body(False)

    @pl.when(kj == qi)
    def _():
        body(True)
        lam = _lam(lam_ref, lam_init)
        gsub = gsub_ref[...]
        for hd in range(n_sub // 2):
            o0 = acc_scr[2 * hd] / _wide(l_scr[2 * hd], dv)
            o1 = acc_scr[2 * hd + 1] / _wide(l_scr[2 * hd + 1], dv)
            o = _diff_combine(o0, o1, lam, gsub, lam_init)
            o_ref[:, hd * dv:(hd + 1) * dv] = o.astype(BF16)


def _attn_a_prompt(qkv16, oa, lam_vecs, gsub, li, n_rows, tq, seg, n_sub, dh, dv, lam_init):
    nq = n_rows // tq
    qi, kj = _tri_pairs(nq, descending=False)
    kern = functools.partial(_attn_a_kernel, n_sub=n_sub, dh=dh, dv=dv, lam_init=lam_init)
    grid_spec = pltpu.PrefetchScalarGridSpec(
        num_scalar_prefetch=2,
        grid=(int(qi.shape[0]),),
        in_specs=[
            pl.BlockSpec((tq, seg), lambda s, qi, kj: (qi[s], 0)),
            pl.BlockSpec((tq, seg), lambda s, qi, kj: (kj[s], 1)),
            pl.BlockSpec((tq, seg), lambda s, qi, kj: (kj[s], 2)),
            pl.BlockSpec((None, 4, dh), lambda s, qi, kj: (li, 0, 0)),
            pl.BlockSpec((None, 1, dv), lambda s, qi, kj: (li, 0, 0)),
            pl.BlockSpec(memory_space=pl.ANY),
        ],
        out_specs=pl.BlockSpec((tq, seg), lambda s, qi, kj: (qi[s], 0)),
        scratch_shapes=[
            pltpu.VMEM((n_sub, tq, LANES), F32),
            pltpu.VMEM((n_sub, tq, LANES), F32),
            pltpu.VMEM((n_sub, tq, dv), F32),
            pltpu.VMEM((n_sub, tq, tq), F32),
            pltpu.VMEM((n_sub * tq, tq), BF16),
            pltpu.VMEM((n_sub, tq, LANES), F32),
        ],
    )

    def kern_alias(qi_ref, kj_ref, q_ref, k_ref, v_ref, lam_ref, gsub_ref, oa_in_ref, o_ref, *scr):
        del oa_in_ref
        kern(qi_ref, kj_ref, q_ref, k_ref, v_ref, lam_ref, gsub_ref, o_ref, *scr)

    return pl.pallas_call(
        kern_alias,
        grid_spec=grid_spec,
        out_shape=jax.ShapeDtypeStruct(oa.shape, BF16),
        input_output_aliases={7: 0},
        compiler_params=_params("arbitrary"),
        name="attn_a_prompt",
    )(qi, kj, qkv16, qkv16, qkv16, lam_vecs, gsub.reshape(gsub.shape[0], 1, dv), oa)


def _attn_b_kernel(qi_ref, kj_ref, q_ref, k_ref, v_ref, tri2_ref, o_ref,
                   run_scr, acc_scr, z_scr, hl_scr, *, n_heads, dh):
    step = pl.program_id(0)
    qi = qi_ref[step]
    kj = kj_ref[step]
    tq = q_ref.shape[0]
    tk = k_ref.shape[0]

    def body(masked):
        mask = None
        if masked:
            r = lax.broadcasted_iota(jnp.int32, (tq, tk), 0)
            c = lax.broadcasted_iota(jnp.int32, (tq, tk), 1)
            mask = c < r
        for h in range(n_heads):
            q = q_ref[:, h * dh:(h + 1) * dh]
            k = k_ref[:, h * dh:(h + 1) * dh]
            z_scr[h] = lax.dot_general(q, k, NT_DIMS, preferred_element_type=F32)
        for h in range(n_heads):
            lsig, lk = _sb_logs(z_scr[h], mask)
            hi, lo = _split_bf16(lk)
            hl_scr[h * tq:(h + 1) * tq, 0:tk] = hi
            hl_scr[h * tq:(h + 1) * tq, tk:2 * tk] = lo
            run = run_scr[h]
            z_scr[h] = lsig + _wide(run, tk)
            run_scr[h] = run + jnp.sum(lk, axis=1, keepdims=True)
        after = jnp.dot(hl_scr[...], tri2_ref[...], preferred_element_type=F32)
        for h in range(n_heads):
            a = jnp.exp2(z_scr[h] + after[h * tq:(h + 1) * tq])
            if masked:
                a = jnp.where(mask, a, 0.0)
            v = v_ref[:, h * dh:(h + 1) * dh]
            acc_scr[:, h * dh:(h + 1) * dh] += jnp.dot(a.astype(BF16), v,
                                                       preferred_element_type=F32)

    @pl.when(kj == qi)
    def _():
        run_scr[...] = jnp.zeros(run_scr.shape, F32)
        acc_scr[...] = jnp.zeros(acc_scr.shape, F32)
        body(True)

    @pl.when(kj < qi)
    def _():
        body(False)

    @pl.when(kj == 0)
    def _():
        o_ref[...] = acc_scr[...].astype(BF16)


def _attn_b_prompt(qkv16, ob, tri2, n_rows, tq, seg, n_heads, dh):
    nq = n_rows // tq
    qi, kj = _tri_pairs(nq, descending=True)
    kern = functools.partial(_attn_b_kernel, n_heads=n_heads, dh=dh)
    grid_spec = pltpu.PrefetchScalarGridSpec(
        num_scalar_prefetch=2,
        grid=(int(qi.shape[0]),),
        in_specs=[
            pl.BlockSpec((tq, seg), lambda s, qi, kj: (qi[s], 3)),
            pl.BlockSpec((tq, seg), lambda s, qi, kj: (kj[s], 4)),
            pl.BlockSpec((tq, seg), lambda s, qi, kj: (kj[s], 5)),
            pl.BlockSpec((2 * tq, tq), lambda s, qi, kj: (0, 0)),
            pl.BlockSpec(memory_space=pl.ANY),
        ],
        out_specs=pl.BlockSpec((tq, seg), lambda s, qi, kj: (qi[s], 0)),
        scratch_shapes=[
            pltpu.VMEM((n_heads, tq, LANES), F32),
            pltpu.VMEM((tq, seg), F32),
            pltpu.VMEM((n_heads, tq, tq), F32),
            pltpu.VMEM((n_heads * tq, 2 * tq), BF16),
        ],
    )

    def kern_alias(qi_ref, kj_ref, q_ref, k_ref, v_ref, tri2_ref, ob_in_ref, o_ref, *scr):
        del ob_in_ref
        kern(qi_ref, kj_ref, q_ref, k_ref, v_ref, tri2_ref, o_ref, *scr)

    return pl.pallas_call(
        kern_alias,
        grid_spec=grid_spec,
        out_shape=jax.ShapeDtypeStruct(ob.shape, BF16),
        input_output_aliases={6: 0},
        compiler_params=_params("arbitrary"),
        name="attn_b_prompt",
    )(qi, kj, qkv16, qkv16, qkv16, tri2, ob)


def _pad_rows(x, rows):
    n = x.shape[0]
    if n == rows:
        return x
    return jnp.concatenate([x, jnp.zeros((rows - n, x.shape[1]), x.dtype)], axis=0)


def _attn_a_sample_kernel(q_ref, kn_ref, vn_ref, kp_ref, vp_ref, lam_ref, gsub_ref, o_ref,
                          o0_scr, *, lam_init):
    h = pl.program_id(1)
    sq = q_ref.shape[0]
    past_len = kp_ref.shape[0]
    q = q_ref[...]
    kn = _pad_rows(kn_ref[...], LANES)
    vn = _pad_rows(vn_ref[...], LANES)
    r = lax.broadcasted_iota(jnp.int32, (sq, LANES), 0)
    c = lax.broadcasted_iota(jnp.int32, (sq, LANES), 1)
    mask_new = ((past_len + c) // CHUNK <= (past_len + r) // CHUNK) & (c < sq)
    sp = lax.dot_general(q, kp_ref[...], NT_DIMS, preferred_element_type=F32)
    sn = lax.dot_general(q, kn, NT_DIMS, preferred_element_type=F32)
    sn = jnp.where(mask_new, sn, NEG_INF)
    m = jnp.maximum(jnp.max(sp, axis=1, keepdims=True), jnp.max(sn, axis=1, keepdims=True))
    pp = jnp.exp2(sp - m)
    pn = jnp.exp2(sn - m)
    l = jnp.sum(pp, axis=1, keepdims=True) + jnp.sum(pn, axis=1, keepdims=True)
    o = (jnp.dot(pp.astype(BF16), vp_ref[...], preferred_element_type=F32)
         + jnp.dot(pn.astype(BF16), vn, preferred_element_type=F32)) / l

    @pl.when(h % 2 == 0)
    def _():
        o0_scr[...] = o

    @pl.when(h % 2 == 1)
    def _():
        lam = _lam(lam_ref, lam_init)
        o_ref[...] = _diff_combine(o0_scr[...], o, lam, gsub_ref[...], lam_init).astype(BF16)


def _attn_a_sample(qkv16, oa, cache_k, cache_v, li, lam_vecs, gsub, row0, sq, seg,
                   n_sub, dh, dv, lam_init):
    _, nb, past_len, _ = cache_k.shape
    blk0 = row0 // sq
    kern = functools.partial(_attn_a_sample_kernel, lam_init=lam_init)

    def kern_alias(q_ref, kn_ref, vn_ref, kp_ref, vp_ref, lam_ref, gsub_ref, oa_in_ref, o_ref,
                   *scr):
        del oa_in_ref
        kern(q_ref, kn_ref, vn_ref, kp_ref, vp_ref, lam_ref, gsub_ref, o_ref, *scr)

    return pl.pallas_call(
        kern_alias,
        grid=(nb, n_sub),
        in_specs=[
            pl.BlockSpec((sq, dh), lambda b, h: (blk0 + b, h)),
            pl.BlockSpec((sq, dh), lambda b, h: (blk0 + b, seg // dh + h)),
            pl.BlockSpec((sq, dv), lambda b, h: (blk0 + b, 2 * seg // dv + h // 2)),
            pl.BlockSpec((None, None, past_len, dh), lambda b, h: (li, b, 0, h)),
            pl.BlockSpec((None, None, past_len, dv), lambda b, h: (li, b, 0, h // 2)),
            pl.BlockSpec((None, 4, dh), lambda b, h: (li, 0, 0)),
            pl.BlockSpec((None, 1, dv), lambda b, h: (li, 0, 0)),
            pl.BlockSpec(memory_space=pl.ANY),
        ],
        out_specs=pl.BlockSpec((sq, dv), lambda b, h: (blk0 + b, h // 2)),
        out_shape=jax.ShapeDtypeStruct(oa.shape, BF16),
        scratch_shapes=[pltpu.VMEM((sq, dv), F32)],
        input_output_aliases={7: 0},
        compiler_params=_params("arbitrary", "arbitrary"),
        name="attn_a_sample",
    )(qkv16, qkv16, qkv16, cache_k, cache_v, lam_vecs, gsub.reshape(gsub.shape[0], 1, dv), oa)


def _attn_b_sample_kernel(q_ref, kn_ref, vn_ref, kp_ref, vp_ref, tri2_ref, o_ref, *, tk):
    sq = q_ref.shape[0]
    past_len = kp_ref.shape[0]
    q = q_ref[...]
    kn = _pad_rows(kn_ref[...], LANES)
    vn = _pad_rows(vn_ref[...], LANES)
    r = lax.broadcasted_iota(jnp.int32, (sq, LANES), 0)
    c = lax.broadcasted_iota(jnp.int32, (sq, LANES), 1)
    mask_new = c < r
    tri = tri2_ref[0:tk, :]
    tri_new = tri2_ref[0:LANES, 0:LANES]
    run = jnp.zeros((sq, LANES), F32)
    z = lax.dot_general(q, kn, NT_DIMS, preferred_element_type=F32)
    acc, run = _sb_block(z, vn, tri_new, run, mask_new)
    for blk in range(past_len // tk - 1, -1, -1):
        kp = kp_ref[blk * tk:(blk + 1) * tk, :]
        vp = vp_ref[blk * tk:(blk + 1) * tk, :]
        z = lax.dot_general(q, kp, NT_DIMS, preferred_element_type=F32)
        contrib, run = _sb_block(z, vp, tri, run, None)
        acc = acc + contrib
    o_ref[...] = acc.astype(BF16)


def _attn_b_sample(qkv16, ob, cache_k, cache_v, li, tri2, row0, sq, seg, n_heads, dh):
    _, nb, past_len, _ = cache_k.shape
    tk = tri2.shape[1]
    assert past_len % tk == 0 and tk >= LANES
    blk0 = row0 // sq
    cb = seg // dh
    kern = functools.partial(_attn_b_sample_kernel, tk=tk)

    def kern_alias(q_ref, kn_ref, vn_ref, kp_ref, vp_ref, tri2_ref, ob_in_ref, o_ref):
        del ob_in_ref
        kern(q_ref, kn_ref, vn_ref, kp_ref, vp_ref, tri2_ref, o_ref)

    return pl.pallas_call(
        kern_alias,
        grid=(nb, n_heads),
        in_specs=[
            pl.BlockSpec((sq, dh), lambda b, h: (blk0 + b, 3 * cb + h)),
            pl.BlockSpec((sq, dh), lambda b, h: (blk0 + b, 4 * cb + h)),
            pl.BlockSpec((sq, dh), lambda b, h: (blk0 + b, 5 * cb + h)),
            pl.BlockSpec((None, None, past_len, dh), lambda b, h: (li, b, 0, h)),
            pl.BlockSpec((None, None, past_len, dh), lambda b, h: (li, b, 0, h)),
            pl.BlockSpec((2 * tk, tk), lambda b, h: (0, 0)),
            pl.BlockSpec(memory_space=pl.ANY),
        ],
        out_specs=pl.BlockSpec((sq, dh), lambda b, h: (blk0 + b, h)),
        out_shape=jax.ShapeDtypeStruct(ob.shape, BF16),
        input_output_aliases={6: 0},
        compiler_params=_params("arbitrary", "arbitrary"),
        name="attn_b_sample",
    )(qkv16, qkv16, qkv16, cache_k, cache_v, tri2, ob)


def _merge_kernel(x_ref, oa_ref, ob_ref, ga_ref, gb_ref, wba_ref, wbb_ref, wo_ref, o_ref):
    @pl.when(pl.program_id(1) == 0)
    def _():
        o_ref[...] = x_ref[...]

    ta = jnp.dot(oa_ref[...], wba_ref[...], preferred_element_type=F32)
    tb = jnp.dot(ob_ref[...], wbb_ref[...], preferred_element_type=F32)
    merged = ga_ref[...] * ta + gb_ref[...] * tb
    o_ref[...] += jnp.dot(merged.astype(BF16), wo_ref[...], preferred_element_type=F32)


def _merge(x, oa, ob, proj32, wba, wbb, wo, li, tm, tn, gate_col0):
    m, d = x.shape
    ka = oa.shape[1]
    kb = ob.shape[1]
    nj = d // tn
    ga0 = gate_col0 // tn
    gb0 = ga0 + nj
    return pl.pallas_call(
        _merge_kernel,
        grid=(m // tm, nj),
        in_specs=[
            pl.BlockSpec((tm, d), lambda i, j: (i, 0)),
            pl.BlockSpec((tm, ka), lambda i, j: (i, 0)),
            pl.BlockSpec((tm, kb), lambda i, j: (i, 0)),
            pl.BlockSpec((tm, tn), lambda i, j: (i, ga0 + j)),
            pl.BlockSpec((tm, tn), lambda i, j: (i, gb0 + j)),
            pl.BlockSpec((None, ka, tn), lambda i, j: (li, 0, j)),
            pl.BlockSpec((None, kb, tn), lambda i, j: (li, 0, j)),
            pl.BlockSpec((None, tn, d), lambda i, j: (li, j, 0)),
        ],
        out_specs=pl.BlockSpec((tm, d), lambda i, j: (i, 0)),
        out_shape=jax.ShapeDtypeStruct((m, d), F32),
        compiler_params=_params("parallel", "arbitrary"),
        name="merge",
    )(x, oa, ob, proj32, proj32, wba, wbb, wo)


def _rope_tables(pos, dh):
    inv = 1.0 / (ROPE_THETA ** (jnp.arange(0, dh, 2, dtype=F32) / dh))
    ang = pos.astype(F32)[:, None] * inv[None, :]
    cos = jnp.cos(ang)
    sin = jnp.sin(ang)
    return jnp.concatenate([cos, cos], axis=1), jnp.concatenate([-sin, sin], axis=1)


def kernel(x_prompt, x_sample, cache_a_k, cache_a_v, cache_b_k, cache_b_v,
           ffn1_norm, ffn1_w_gate, ffn1_w_up, ffn1_w_down,
           mix_norm, w_in, a_q_norm, a_k_norm,
           a_lam_q1, a_lam_k1, a_lam_q2, a_lam_k2, a_sub_norm,
           w_branch_a, w_branch_b, w_out,
           ffn2_norm, ffn2_w_gate, ffn2_w_up, ffn2_w_down):
    bp, sp, d = x_prompt.shape
    bs, ss, _ = x_sample.shape
    depth, _, past_len, n_sub, dh_a = cache_a_k.shape
    h_a, dv_a = cache_a_v.shape[3:]
    h_b, dh_b = cache_b_k.shape[3:]
    seg = n_sub * dh_a
    assert bp == 1 and h_a * dv_a == seg and h_b * dh_b == seg and n_sub == 2 * h_a
    assert w_in.shape[2] == 6 * seg + 2 * d and d == 2 * seg

    n_p = bp * sp
    m = n_p + bs * ss
    tm = _pick(m, (768, 512, 384, 256, 128, 64, 32))
    tf = _pick(ffn1_w_gate.shape[2], (512, 256, 128))
    tq = _pick(sp, (256, 128))
    tn = _pick(d, (512, 256, 128))
    assert n_p % tq == 0 and n_p % ss == 0 and tq % CHUNK == 0

    x = jnp.concatenate([x_prompt.reshape(n_p, d), x_sample.reshape(bs * ss, d)], axis=0)
    pos = jnp.concatenate([jnp.arange(sp)] * bp + [past_len + jnp.arange(ss)] * bs)
    cos, sin = _rope_tables(pos, dh_a)
    tri = np.tril(np.ones((tq, tq), np.float32), -1)
    tri2 = jnp.asarray(np.concatenate([tri, tri], axis=0), BF16)
    lam_vecs = jnp.stack([a_lam_q1, a_lam_k1, a_lam_q2, a_lam_k2], axis=1)

    bf = lambda w: w.astype(BF16)
    f1g, f1u, f1d = bf(ffn1_w_gate), bf(ffn1_w_up), bf(ffn1_w_down)
    f2g, f2u, f2d = bf(ffn2_w_gate), bf(ffn2_w_up), bf(ffn2_w_down)
    w_in16, wba16, wbb16, wo16 = bf(w_in), bf(w_branch_a), bf(w_branch_b), bf(w_out)
    flat = lambda c: bf(c.reshape(depth, bs, past_len, seg))
    ck_a, cv_a, ck_b, cv_b = flat(cache_a_k), flat(cache_a_v), flat(cache_b_k), flat(cache_b_v)

    rows = []
    for li in range(depth):
        lam_init = 0.8 - 0.6 * math.exp(-0.3 * li)

        x = _ffn(x, ffn1_norm, f1g, f1u, f1d, li, tm, tf)
        proj32, qkv16 = _proj(x, mix_norm, w_in16, cos, sin, a_q_norm, a_k_norm, li,
                              tm, seg, n_sub, dh_a,
                              dh_a ** -0.5 * LOG2E, dh_b ** -0.5 * LOG2E)

        oa = jnp.zeros((m, seg), BF16)
        oa = _attn_a_prompt(qkv16, oa, lam_vecs, a_sub_norm, li, n_p, tq, seg,
                            n_sub, dh_a, dv_a, lam_init)
        oa = _attn_a_sample(qkv16, oa, ck_a, cv_a, li, lam_vecs, a_sub_norm, n_p, ss,
                            seg, n_sub, dh_a, dv_a, lam_init)
        ob = jnp.zeros((m, seg), BF16)
        ob = _attn_b_prompt(qkv16, ob, tri2, n_p, tq, seg, h_b, dh_b)
        ob = _attn_b_sample(qkv16, ob, ck_b, cv_b, li, tri2, n_p, ss, seg, h_b, dh_b)

        x = _merge(x, oa, ob, proj32, wba16, wbb16, wo16, li, tm, tn, 4 * seg)
        x = _ffn(x, ffn2_norm, f2g, f2u, f2d, li, tm, tf)
        rows.append(proj32)

    def cache_rows(col, n_heads, dh):
        pr = jnp.stack([r[:n_p, col * seg:(col + 1) * seg] for r in rows], axis=0)
        sm = jnp.stack([r[n_p:, col * seg:(col + 1) * seg] for r in rows], axis=0)
        return (pr.reshape(depth, bp, sp, n_heads, dh), sm.reshape(depth, bs, ss, n_heads, dh))

    ak_p, ak_s = cache_rows(0, n_sub, dh_a)
    av_p, av_s = cache_rows(1, h_a, dv_a)
    bk_p, bk_s = cache_rows(2, h_b, dh_b)
    bv_p, bv_s = cache_rows(3, h_b, dh_b)
    return (x[:n_p].reshape(bp, sp, d), x[n_p:].reshape(bs, ss, d),
            ak_p, av_p, bk_p, bv_p, ak_s, av_s, bk_s, bv_s)
```

```python
import functools
import math

import numpy as np
import jax
import jax.numpy as jnp
from jax import lax
from jax.experimental import pallas as pl
from jax.experimental.pallas import tpu as pltpu

F32 = jnp.float32
BF16 = jnp.bfloat16

CHUNK = 64
ROPE_THETA = 10000.0
EPS = 1e-6
NEG_INF = -1e30
LOG2E = math.log2(math.e)
LANES = 128
VMEM_LIMIT = 56 * 1024 * 1024

NT_DIMS = (((1,), (1,)), ((), ()))


def _pick(n, candidates):
    for c in candidates:
        if n % c == 0:
            return c
    return n


def _params(*sem):
    return pltpu.CompilerParams(dimension_semantics=sem, vmem_limit_bytes=VMEM_LIMIT)


def _rmsnorm_rows(x_ref, g_ref, h_ref, copy_ref=None):
    rows = _pick(x_ref.shape[0], (64, 32, 16))
    g = g_ref[...]

    def chunk(c, carry):
        r = pl.ds(pl.multiple_of(c * rows, rows), rows)
        x = x_ref[r, :]
        ms = jnp.mean(x * x, axis=-1, keepdims=True)
        h_ref[r, :] = (x * lax.rsqrt(ms + EPS) * g).astype(BF16)
        if copy_ref is not None:
            copy_ref[r, :] = x
        return carry

    lax.fori_loop(0, x_ref.shape[0] // rows, chunk, 0)


def _ffn_kernel(x_ref, g_ref, wg_ref, wu_ref, wd_ref, o_ref, h_ref):
    @pl.when(pl.program_id(1) == 0)
    def _():
        _rmsnorm_rows(x_ref, g_ref, h_ref, copy_ref=o_ref)

    h = h_ref[...]
    g = jnp.dot(h, wg_ref[...], preferred_element_type=F32)
    u = jnp.dot(h, wu_ref[...], preferred_element_type=F32)
    a = (0.5 * g) * jax.nn.sigmoid(g) * u
    o_ref[...] += jnp.dot(a.astype(BF16), wd_ref[...], preferred_element_type=F32)


def _ffn(x, gain, wg, wu, wd, li, tm, tf):
    m, d = x.shape
    f = wg.shape[2]
    return pl.pallas_call(
        _ffn_kernel,
        grid=(m // tm, f // tf),
        in_specs=[
            pl.BlockSpec((tm, d), lambda i, j: (i, 0)),
            pl.BlockSpec((None, 1, d), lambda i, j: (li, 0, 0)),
            pl.BlockSpec((None, d, tf), lambda i, j: (li, 0, j)),
            pl.BlockSpec((None, d, tf), lambda i, j: (li, 0, j)),
            pl.BlockSpec((None, tf, d), lambda i, j: (li, j, 0)),
        ],
        out_specs=pl.BlockSpec((tm, d), lambda i, j: (i, 0)),
        out_shape=jax.ShapeDtypeStruct((m, d), F32),
        scratch_shapes=[pltpu.VMEM((tm, d), BF16)],
        compiler_params=_params("parallel", "arbitrary"),
        name="ffn",
    )(x, gain.reshape(gain.shape[0], 1, d), wg, wu, wd)


def _proj_kernel(x_ref, g_ref, w_ref, cos_ref, sin_ref, qn_ref, kn_ref,
                 o32_ref, o16_ref, h_ref, *, n_heads, dh, scale_a, scale_b):
    j = pl.program_id(1)

    @pl.when(j == 0)
    def _():
        _rmsnorm_rows(x_ref, g_ref, h_ref)

    acc = jnp.dot(h_ref[...], w_ref[...], preferred_element_type=F32)

    def norm_rope(gain_ref, scale):
        cos = cos_ref[...]
        sin = sin_ref[...]
        outs = []
        for hd in range(n_heads):
            xh = acc[:, hd * dh:(hd + 1) * dh]
            ms = jnp.mean(xh * xh, axis=-1, keepdims=True)
            y = xh * lax.rsqrt(ms + EPS) * gain_ref[...]
            y = y * cos + pltpu.roll(y, dh // 2, 1) * sin
            if scale is not None:
                y = y * scale
            outs.append(y)
        return jnp.concatenate(outs, axis=1)

    @pl.when(j == 0)
    def _():
        o16_ref[...] = norm_rope(qn_ref, scale_a).astype(BF16)

    @pl.when(j == 1)
    def _():
        y = norm_rope(kn_ref, None)
        o32_ref[...] = y
        o16_ref[...] = y.astype(BF16)

    @pl.when((j == 2) | (j == 4) | (j == 5))
    def _():
        o32_ref[...] = acc
        o16_ref[...] = acc.astype(BF16)

    @pl.when(j == 3)
    def _():
        o16_ref[...] = (acc * scale_b).astype(BF16)

    @pl.when(j >= 6)
    def _():
        o32_ref[...] = jax.nn.sigmoid(acc)


def _proj(x, gain, w_in, cos, sin, qn, kn, li, tm, seg, n_heads, dh, scale_a, scale_b):
    m, d = x.shape
    n = w_in.shape[2]
    nj = n // seg
    assert nj == 10 and n_heads * dh == seg

    def o32_map(i, j):
        return (i, jnp.maximum(j - 1 - (j >= 3).astype(jnp.int32), 0))

    def o16_map(i, j):
        return (i, jnp.minimum(j, 5))

    kern = functools.partial(_proj_kernel, n_heads=n_heads, dh=dh,
                             scale_a=scale_a, scale_b=scale_b)
    return pl.pallas_call(
        kern,
        grid=(m // tm, nj),
        in_specs=[
            pl.BlockSpec((tm, d), lambda i, j: (i, 0)),
            pl.BlockSpec((None, 1, d), lambda i, j: (li, 0, 0)),
            pl.BlockSpec((None, d, seg), lambda i, j: (li, 0, j)),
            pl.BlockSpec((tm, dh), lambda i, j: (i, 0)),
            pl.BlockSpec((tm, dh), lambda i, j: (i, 0)),
            pl.BlockSpec((None, 1, dh), lambda i, j: (li, 0, 0)),
            pl.BlockSpec((None, 1, dh), lambda i, j: (li, 0, 0)),
        ],
        out_specs=[
            pl.BlockSpec((tm, seg), o32_map),
            pl.BlockSpec((tm, seg), o16_map),
        ],
        out_shape=[
            jax.ShapeDtypeStruct((m, 8 * seg), F32),
            jax.ShapeDtypeStruct((m, 6 * seg), BF16),
        ],
        scratch_shapes=[pltpu.VMEM((tm, d), BF16)],
        compiler_params=_params("parallel", "arbitrary"),
        name="proj",
    )(x, gain.reshape(gain.shape[0], 1, d), w_in, cos, sin,
      qn.reshape(qn.shape[0], 1, dh), kn.reshape(kn.shape[0], 1, dh))


def _lam(lam_ref, lam_init):
    v = lam_ref[...]
    s1 = jnp.sum(v[0:1] * v[1:2], axis=1, keepdims=True)
    s2 = jnp.sum(v[2:3] * v[3:4], axis=1, keepdims=True)
    return jnp.exp(s1) - jnp.exp(s2) + lam_init


def _diff_combine(o0, o1, lam, gsub, lam_init):
    o = o0 - lam * o1
    ms = jnp.mean(o * o, axis=-1, keepdims=True)
    return (o * lax.rsqrt(ms + EPS) * gsub) * (1.0 - lam_init)


def _wide(col, n):
    reps = n // LANES
    return col if reps == 1 else jnp.concatenate([col] * reps, axis=1)


def _sb_logs(zn, mask):
    l = jnp.log(1.0 + jnp.exp2(-jnp.abs(zn))) * LOG2E
    lk = jnp.minimum(zn, 0.0) - l
    lsig = lk - zn
    if mask is not None:
        lk = jnp.where(mask, lk, 0.0)
    return lsig, lk


def _sb_block(zn, v, tri, run, mask):
    lsig, lk = _sb_logs(zn, mask)
    after = jnp.dot(lk.astype(BF16), tri, preferred_element_type=F32)
    a = jnp.exp2(lsig + after)
    if mask is not None:
        a = jnp.where(mask, a, 0.0)
    contrib = jnp.dot(a.astype(BF16), v, preferred_element_type=F32) * jnp.exp2(run)
    return contrib, run + jnp.sum(lk, axis=1, keepdims=True)


def _skewed(stages, n):
    for t in range(n + len(stages) - 1):
        for s in reversed(range(len(stages))):
            if 0 <= t - s < n:
                stages[s](t - s)


FIRST, LAST, MASKED = 1, 2, 4


def _causal_steps(n_rows, tq, tk, descending):
    qi, kj, fl = [], [], []
    for i in range(n_rows // tq):
        jmax = ((i + 1) * tq - 1) // tk
        ks = list(range(jmax, -1, -1) if descending else range(jmax + 1))
        for n, j in enumerate(ks):
            flags = (FIRST if n == 0 else 0) | (LAST if n == len(ks) - 1 else 0)
            if (j + 1) * tk > i * tq:
                flags |= MASKED
            qi.append(i)
            kj.append(j)
            fl.append(flags)
    return tuple(jnp.asarray(np.array(a, np.int32)) for a in (qi, kj, fl))


def _attn_a_kernel(qi_ref, kj_ref, fl_ref, q_ref, k_ref, v_ref, lam_ref, gsub_ref, o_ref,
                   m_scr, l_scr, acc_scr, *, n_sub, dh, dv, lam_init):
    step = pl.program_id(0)
    flags = fl_ref[step]
    tq = q_ref.shape[0]
    tk = k_ref.shape[0]

    @pl.when((flags & FIRST) != 0)
    def _():
        m_scr[...] = jnp.full(m_scr.shape, NEG_INF, F32)
        l_scr[...] = jnp.zeros(l_scr.shape, F32)
        acc_scr[...] = jnp.zeros(acc_scr.shape, F32)

    def body(masked):
        if masked:
            off = (qi_ref[step] * tq - kj_ref[step] * tk) // CHUNK
            r = lax.broadcasted_iota(jnp.int32, (tq, tk), 0)
            c = lax.broadcasted_iota(jnp.int32, (tq, tk), 1)
            mask = (c // CHUNK - r // CHUNK) <= off
        sc, pb, al = {}, {}, {}

        def scores(hd):
            for h in (2 * hd, 2 * hd + 1):
                q = q_ref[:, h * dh:(h + 1) * dh]
                k = k_ref[:, h * dh:(h + 1) * dh]
                s = lax.dot_general(q, k, NT_DIMS, preferred_element_type=F32)
                if masked:
                    s = jnp.where(mask, s, NEG_INF)
                sc[h] = s

        def softmax_update(hd):
            for h in (2 * hd, 2 * hd + 1):
                s = sc.pop(h)
                m_prev = m_scr[h]
                m_new = jnp.maximum(m_prev, jnp.max(s, axis=1, keepdims=True))
                alpha = jnp.exp2(m_prev - m_new)
                p = jnp.exp2(s - _wide(m_new, tk))
                l_scr[h] = alpha * l_scr[h] + jnp.sum(p, axis=1, keepdims=True)
                m_scr[h] = m_new
                al[h] = alpha
                pb[h] = p.astype(BF16)

        def values(hd):
            v = v_ref[:, hd * dv:(hd + 1) * dv]
            p2 = jnp.concatenate([pb.pop(2 * hd), pb.pop(2 * hd + 1)], axis=0)
            pv = jnp.dot(p2, v, preferred_element_type=F32)
            for u in range(2):
                h = 2 * hd + u
                acc_scr[h] = _wide(al.pop(h), dv) * acc_scr[h] + pv[u * tq:(u + 1) * tq]

        _skewed((scores, softmax_update, values), n_sub // 2)

    @pl.when((flags & MASKED) == 0)
    def _():
        body(False)

    @pl.when((flags & MASKED) != 0)
    def _():
        body(True)

    @pl.when((flags & LAST) != 0)
    def _():
        lam = _lam(lam_ref, lam_init)
        gsub = gsub_ref[...]
        for hd in range(n_sub // 2):
            o0 = acc_scr[2 * hd] / _wide(l_scr[2 * hd], dv)
            o1 = acc_scr[2 * hd + 1] / _wide(l_scr[2 * hd + 1], dv)
            o = _diff_combine(o0, o1, lam, gsub, lam_init)
            o_ref[:, hd * dv:(hd + 1) * dv] = o.astype(BF16)


def _attn_a_prompt(qkv16, oa, lam_vecs, gsub, li, n_rows, tq, tk, seg, n_sub, dh, dv, lam_init):
    qi, kj, fl = _causal_steps(n_rows, tq, tk, descending=False)
    kern = functools.partial(_attn_a_kernel, n_sub=n_sub, dh=dh, dv=dv, lam_init=lam_init)
    grid_spec = pltpu.PrefetchScalarGridSpec(
        num_scalar_prefetch=3,
        grid=(int(qi.shape[0]),),
        in_specs=[
            pl.BlockSpec((tq, seg), lambda s, qi, kj, fl: (qi[s], 0)),
            pl.BlockSpec((tk, seg), lambda s, qi, kj, fl: (kj[s], 1)),
            pl.BlockSpec((tk, seg), lambda s, qi, kj, fl: (kj[s], 2)),
            pl.BlockSpec((None, 4, dh), lambda s, qi, kj, fl: (li, 0, 0)),
            pl.BlockSpec((None, 1, dv), lambda s, qi, kj, fl: (li, 0, 0)),
            pl.BlockSpec(memory_space=pl.ANY),
        ],
        out_specs=pl.BlockSpec((tq, seg), lambda s, qi, kj, fl: (qi[s], 0)),
        scratch_shapes=[
            pltpu.VMEM((n_sub, tq, LANES), F32),
            pltpu.VMEM((n_sub, tq, LANES), F32),
            pltpu.VMEM((n_sub, tq, dv), F32),
        ],
    )

    def kern_alias(qi_ref, kj_ref, fl_ref, q_ref, k_ref, v_ref, lam_ref, gsub_ref, oa_in_ref,
                   o_ref, *scr):
        del oa_in_ref
        kern(qi_ref, kj_ref, fl_ref, q_ref, k_ref, v_ref, lam_ref, gsub_ref, o_ref, *scr)

    return pl.pallas_call(
        kern_alias,
        grid_spec=grid_spec,
        out_shape=jax.ShapeDtypeStruct(oa.shape, BF16),
        input_output_aliases={8: 0},
        compiler_params=_params("arbitrary"),
        name="attn_a_prompt",
    )(qi, kj, fl, qkv16, qkv16, qkv16, lam_vecs, gsub.reshape(gsub.shape[0], 1, dv), oa)


def _attn_b_kernel(qi_ref, kj_ref, fl_ref, q_ref, k_ref, v_ref, tri_ref, o_ref,
                   run_scr, er_scr, acc_scr, live_ref, *, n_heads, dh):
    step = pl.program_id(0)
    flags = fl_ref[step]
    tq = q_ref.shape[0]
    tk = k_ref.shape[0]

    def body(masked):
        mask = None
        if masked:
            off = qi_ref[step] * tq - kj_ref[step] * tk
            r = lax.broadcasted_iota(jnp.int32, (tq, tk), 0)
            c = lax.broadcasted_iota(jnp.int32, (tq, tk), 1)
            mask = (c - r) < off
        tri = tri_ref[...]
        zn, lsig, lkb, er, after = {}, {}, {}, {}, {}

        def logits(h):
            q = q_ref[:, h * dh:(h + 1) * dh]
            k = k_ref[:, h * dh:(h + 1) * dh]
            zn[h] = lax.dot_general(q, k, NT_DIMS, preferred_element_type=F32)

        def logs(h):
            lsig[h], lk = _sb_logs(zn.pop(h), mask)
            lkb[h] = lk.astype(BF16)
            run = run_scr[h] + jnp.sum(lk, axis=1, keepdims=True)
            run_scr[h] = run
            er[h] = er_scr[h]
            er_scr[h] = jnp.exp2(run)

        def cumsum(h):
            after[h] = jnp.dot(lkb.pop(h), tri, preferred_element_type=F32)

        def weights_pv(h):
            a = jnp.exp2(lsig.pop(h) + after.pop(h))
            if masked:
                a = jnp.where(mask, a, 0.0)
            v = v_ref[:, h * dh:(h + 1) * dh]
            pv = jnp.dot(a.astype(BF16), v, preferred_element_type=F32)
            acc_scr[:, h * dh:(h + 1) * dh] += pv * _wide(er.pop(h), dh)

        _skewed((logits, logs, cumsum, weights_pv), n_heads)
        top = er_scr[0]
        for h in range(1, n_heads):
            top = jnp.maximum(top, er_scr[h])
        live_ref[0] = (jnp.max(top) > 0.0).astype(jnp.int32)

    @pl.when((flags & FIRST) != 0)
    def _():
        run_scr[...] = jnp.zeros(run_scr.shape, F32)
        er_scr[...] = jnp.ones(er_scr.shape, F32)
        acc_scr[...] = jnp.zeros(acc_scr.shape, F32)
        live_ref[0] = 1

    live = live_ref[0] != 0

    @pl.when(((flags & MASKED) != 0) & live)
    def _():
        body(True)

    @pl.when(((flags & MASKED) == 0) & live)
    def _():
        body(False)

    @pl.when((flags & LAST) != 0)
    def _():
        o_ref[...] = acc_scr[...].astype(BF16)


def _attn_b_prompt(qkv16, ob, tri, n_rows, tq, seg, n_heads, dh):
    tk = tri.shape[1]
    qi, kj, fl = _causal_steps(n_rows, tq, tk, descending=True)
    kern = functools.partial(_attn_b_kernel, n_heads=n_heads, dh=dh)
    grid_spec = pltpu.PrefetchScalarGridSpec(
        num_scalar_prefetch=3,
        grid=(int(qi.shape[0]),),
        in_specs=[
            pl.BlockSpec((tq, seg), lambda s, qi, kj, fl: (qi[s], 3)),
            pl.BlockSpec((tk, seg), lambda s, qi, kj, fl: (kj[s], 4)),
            pl.BlockSpec((tk, seg), lambda s, qi, kj, fl: (kj[s], 5)),
            pl.BlockSpec((tk, tk), lambda s, qi, kj, fl: (0, 0)),
            pl.BlockSpec(memory_space=pl.ANY),
        ],
        out_specs=pl.BlockSpec((tq, seg), lambda s, qi, kj, fl: (qi[s], 0)),
        scratch_shapes=[
            pltpu.VMEM((n_heads, tq, LANES), F32),
            pltpu.VMEM((n_heads, tq, LANES), F32),
            pltpu.VMEM((tq, seg), F32),
            pltpu.SMEM((1,), jnp.int32),
        ],
    )

    def kern_alias(qi_ref, kj_ref, fl_ref, q_ref, k_ref, v_ref, tri_ref, ob_in_ref, o_ref, *scr):
        del ob_in_ref
        kern(qi_ref, kj_ref, fl_ref, q_ref, k_ref, v_ref, tri_ref, o_ref, *scr)

    return pl.pallas_call(
        kern_alias,
        grid_spec=grid_spec,
        out_shape=jax.ShapeDtypeStruct(ob.shape, BF16),
        input_output_aliases={7: 0},
        compiler_params=_params("arbitrary"),
        name="attn_b_prompt",
    )(qi, kj, fl, qkv16, qkv16, qkv16, tri, ob)


def _pad_rows(x, rows):
    n = x.shape[0]
    if n == rows:
        return x
    return jnp.concatenate([x, jnp.zeros((rows - n, x.shape[1]), x.dtype)], axis=0)


def _attn_a_sample_kernel(q_ref, kn_ref, vn_ref, kp_ref, vp_ref, lam_ref, gsub_ref, o_ref,
                          *, n_sub, dh, dv, lam_init):
    sq = q_ref.shape[0]
    past_len = kp_ref.shape[0]
    kn = _pad_rows(kn_ref[...], LANES)
    vn = _pad_rows(vn_ref[...], LANES)
    r = lax.broadcasted_iota(jnp.int32, (sq, LANES), 0)
    c = lax.broadcasted_iota(jnp.int32, (sq, LANES), 1)
    mask_new = ((past_len + c) // CHUNK <= (past_len + r) // CHUNK) & (c < sq)
    lam = _lam(lam_ref, lam_init)
    gsub = gsub_ref[...]
    outs = []
    for h in range(n_sub):
        hs = slice(h * dh, (h + 1) * dh)
        vs = slice((h // 2) * dv, (h // 2 + 1) * dv)
        q = q_ref[:, hs]
        sp = lax.dot_general(q, kp_ref[:, hs], NT_DIMS, preferred_element_type=F32)
        sn = lax.dot_general(q, kn[:, hs], NT_DIMS, preferred_element_type=F32)
        sn = jnp.where(mask_new, sn, NEG_INF)
        m = jnp.maximum(jnp.max(sp, axis=1, keepdims=True), jnp.max(sn, axis=1, keepdims=True))
        pp = jnp.exp2(sp - m)
        pn = jnp.exp2(sn - m)
        l = jnp.sum(pp, axis=1, keepdims=True) + jnp.sum(pn, axis=1, keepdims=True)
        o = (jnp.dot(pp.astype(BF16), vp_ref[:, vs], preferred_element_type=F32)
             + jnp.dot(pn.astype(BF16), vn[:, vs], preferred_element_type=F32))
        outs.append(o / l)
    for hd in range(n_sub // 2):
        o = _diff_combine(outs[2 * hd], outs[2 * hd + 1], lam, gsub, lam_init)
        o_ref[:, hd * dv:(hd + 1) * dv] = o.astype(BF16)


def _attn_a_sample(qkv16, oa, cache_k, cache_v, li, lam_vecs, gsub, row0, sq, seg,
                   n_sub, dh, dv, lam_init):
    _, nb, past_len, _ = cache_k.shape
    blk0 = row0 // sq
    kern = functools.partial(_attn_a_sample_kernel, n_sub=n_sub, dh=dh, dv=dv,
                             lam_init=lam_init)

    def kern_alias(q_ref, kn_ref, vn_ref, kp_ref, vp_ref, lam_ref, gsub_ref, oa_in_ref, o_ref):
        del oa_in_ref
        kern(q_ref, kn_ref, vn_ref, kp_ref, vp_ref, lam_ref, gsub_ref, o_ref)

    return pl.pallas_call(
        kern_alias,
        grid=(nb,),
        in_specs=[
            pl.BlockSpec((sq, seg), lambda b: (blk0 + b, 0)),
            pl.BlockSpec((sq, seg), lambda b: (blk0 + b, 1)),
            pl.BlockSpec((sq, seg), lambda b: (blk0 + b, 2)),
            pl.BlockSpec((None, None, past_len, seg), lambda b: (li, b, 0, 0)),
            pl.BlockSpec((None, None, past_len, seg), lambda b: (li, b, 0, 0)),
            pl.BlockSpec((None, 4, dh), lambda b: (li, 0, 0)),
            pl.BlockSpec((None, 1, dv), lambda b: (li, 0, 0)),
            pl.BlockSpec(memory_space=pl.ANY),
        ],
        out_specs=pl.BlockSpec((sq, seg), lambda b: (blk0 + b, 0)),
        out_shape=jax.ShapeDtypeStruct(oa.shape, BF16),
        input_output_aliases={7: 0},
        compiler_params=_params("arbitrary"),
        name="attn_a_sample",
    )(qkv16, qkv16, qkv16, cache_k, cache_v, lam_vecs, gsub.reshape(gsub.shape[0], 1, dv), oa)


def _attn_b_sample_kernel(q_ref, kn_ref, vn_ref, kp_ref, vp_ref, tri_ref, o_ref, *, tk):
    sq = q_ref.shape[0]
    past_len = kp_ref.shape[0]
    q = q_ref[...]
    kn = _pad_rows(kn_ref[...], LANES)
    vn = _pad_rows(vn_ref[...], LANES)
    r = lax.broadcasted_iota(jnp.int32, (sq, LANES), 0)
    c = lax.broadcasted_iota(jnp.int32, (sq, LANES), 1)
    mask_new = c < r
    tri = tri_ref[...]
    tri_new = tri_ref[0:LANES, 0:LANES]
    run = jnp.zeros((sq, LANES), F32)
    z = lax.dot_general(q, kn, NT_DIMS, preferred_element_type=F32)
    acc, run = _sb_block(z, vn, tri_new, run, mask_new)
    for blk in range(past_len // tk - 1, -1, -1):
        kp = kp_ref[blk * tk:(blk + 1) * tk, :]
        vp = vp_ref[blk * tk:(blk + 1) * tk, :]
        z = lax.dot_general(q, kp, NT_DIMS, preferred_element_type=F32)
        contrib, run = _sb_block(z, vp, tri, run, None)
        acc = acc + contrib
    o_ref[...] = acc.astype(BF16)


def _attn_b_sample(qkv16, ob, cache_k, cache_v, li, tri, row0, sq, seg, n_heads, dh):
    _, nb, past_len, _ = cache_k.shape
    tk = tri.shape[1]
    assert past_len % tk == 0 and tk >= LANES
    blk0 = row0 // sq
    cb = seg // dh
    kern = functools.partial(_attn_b_sample_kernel, tk=tk)

    def kern_alias(q_ref, kn_ref, vn_ref, kp_ref, vp_ref, tri_ref, ob_in_ref, o_ref):
        del ob_in_ref
        kern(q_ref, kn_ref, vn_ref, kp_ref, vp_ref, tri_ref, o_ref)

    return pl.pallas_call(
        kern_alias,
        grid=(nb, n_heads),
        in_specs=[
            pl.BlockSpec((sq, dh), lambda b, h: (blk0 + b, 3 * cb + h)),
            pl.BlockSpec((sq, dh), lambda b, h: (blk0 + b, 4 * cb + h)),
            pl.BlockSpec((sq, dh), lambda b, h: (blk0 + b, 5 * cb + h)),
            pl.BlockSpec((None, None, past_len, dh), lambda b, h: (li, b, 0, h)),
            pl.BlockSpec((None, None, past_len, dh), lambda b, h: (li, b, 0, h)),
            pl.BlockSpec((tk, tk), lambda b, h: (0, 0)),
            pl.BlockSpec(memory_space=pl.ANY),
        ],
        out_specs=pl.BlockSpec((sq, dh), lambda b, h: (blk0 + b, h)),
        out_shape=jax.ShapeDtypeStruct(ob.shape, BF16),
        input_output_aliases={6: 0},
        compiler_params=_params("arbitrary", "arbitrary"),
        name="attn_b_sample",
    )(qkv16, qkv16, qkv16, cache_k, cache_v, tri, ob)


def _merge_kernel(x_ref, oa_ref, ob_ref, ga_ref, gb_ref, wba_ref, wbb_ref, wo_ref, o_ref):
    @pl.when(pl.program_id(1) == 0)
    def _():
        o_ref[...] = x_ref[...]

    ta = jnp.dot(oa_ref[...], wba_ref[...], preferred_element_type=F32)
    tb = jnp.dot(ob_ref[...], wbb_ref[...], preferred_element_type=F32)
    merged = ga_ref[...] * ta + gb_ref[...] * tb
    o_ref[...] += jnp.dot(merged.astype(BF16), wo_ref[...], preferred_element_type=F32)


def _merge(x, oa, ob, proj32, wba, wbb, wo, li, tm, tn, gate_col0):
    m, d = x.shape
    ka = oa.shape[1]
    kb = ob.shape[1]
    nj = d // tn
    ga0 = gate_col0 // tn
    gb0 = ga0 + nj
    return pl.pallas_call(
        _merge_kernel,
        grid=(m // tm, nj),
        in_specs=[
            pl.BlockSpec((tm, d), lambda i, j: (i, 0)),
            pl.BlockSpec((tm, ka), lambda i, j: (i, 0)),
            pl.BlockSpec((tm, kb), lambda i, j: (i, 0)),
            pl.BlockSpec((tm, tn), lambda i, j: (i, ga0 + j)),
            pl.BlockSpec((tm, tn), lambda i, j: (i, gb0 + j)),
            pl.BlockSpec((None, ka, tn), lambda i, j: (li, 0, j)),
            pl.BlockSpec((None, kb, tn), lambda i, j: (li, 0, j)),
            pl.BlockSpec((None, tn, d), lambda i, j: (li, j, 0)),
        ],
        out_specs=pl.BlockSpec((tm, d), lambda i, j: (i, 0)),
        out_shape=jax.ShapeDtypeStruct((m, d), F32),
        compiler_params=_params("parallel", "arbitrary"),
        name="merge",
    )(x, oa, ob, proj32, proj32, wba, wbb, wo)


def _rope_tables(pos, dh):
    inv = 1.0 / (ROPE_THETA ** (jnp.arange(0, dh, 2, dtype=F32) / dh))
    ang = pos.astype(F32)[:, None] * inv[None, :]
    cos = jnp.cos(ang)
    sin = jnp.sin(ang)
    return jnp.concatenate([cos, cos], axis=1), jnp.concatenate([-sin, sin], axis=1)


def kernel(x_prompt, x_sample, cache_a_k, cache_a_v, cache_b_k, cache_b_v,
           ffn1_norm, ffn1_w_gate, ffn1_w_up, ffn1_w_down,
           mix_norm, w_in, a_q_norm, a_k_norm,
           a_lam_q1, a_lam_k1, a_lam_q2, a_lam_k2, a_sub_norm,
           w_branch_a, w_branch_b, w_out,
           ffn2_norm, ffn2_w_gate, ffn2_w_up, ffn2_w_down):
    bp, sp, d = x_prompt.shape
    bs, ss, _ = x_sample.shape
    depth, _, past_len, n_sub, dh_a = cache_a_k.shape
    h_a, dv_a = cache_a_v.shape[3:]
    h_b, dh_b = cache_b_k.shape[3:]
    seg = n_sub * dh_a
    assert bp == 1 and h_a * dv_a == seg and h_b * dh_b == seg and n_sub == 2 * h_a
    assert w_in.shape[2] == 6 * seg + 2 * d and d == 2 * seg

    n_p = bp * sp
    m = n_p + bs * ss
    tm = _pick(m, (768, 512, 384, 256, 128, 64, 32))
    tf = _pick(ffn1_w_gate.shape[2], (512, 256, 128))
    tn = _pick(d, (512, 256, 128))
    tq_a, tk_a = _pick(sp, (256, 128)), _pick(sp, (512, 256, 128))
    tq_b, tk_b = _pick(sp, (256, 128)), _pick(math.gcd(sp, past_len), (256, 128))
    assert n_p % ss == 0 and min(tq_a, tk_a, tq_b, tk_b) % CHUNK == 0

    x = jnp.concatenate([x_prompt.reshape(n_p, d), x_sample.reshape(bs * ss, d)], axis=0)
    pos = jnp.concatenate([jnp.arange(sp)] * bp + [past_len + jnp.arange(ss)] * bs)
    cos, sin = _rope_tables(pos, dh_a)
    tri = jnp.asarray(np.tril(np.ones((tk_b, tk_b), np.float32), -1), BF16)
    lam_vecs = jnp.stack([a_lam_q1, a_lam_k1, a_lam_q2, a_lam_k2], axis=1)

    bf = lambda w: w.astype(BF16)
    f1g, f1u, f1d = bf(ffn1_w_gate), bf(ffn1_w_up), bf(ffn1_w_down)
    f2g, f2u, f2d = bf(ffn2_w_gate), bf(ffn2_w_up), bf(ffn2_w_down)
    w_in16, wba16, wbb16, wo16 = bf(w_in), bf(w_branch_a), bf(w_branch_b), bf(w_out)
    flat = lambda c: bf(c.reshape(depth, bs, past_len, seg))
    ck_a, cv_a, ck_b, cv_b = flat(cache_a_k), flat(cache_a_v), flat(cache_b_k), flat(cache_b_v)

    rows = []
    for li in range(depth):
        lam_init = 0.8 - 0.6 * math.exp(-0.3 * li)

        x = _ffn(x, ffn1_norm, f1g, f1u, f1d, li, tm, tf)
        proj32, qkv16 = _proj(x, mix_norm, w_in16, cos, sin, a_q_norm, a_k_norm, li,
                              tm, seg, n_sub, dh_a,
                              dh_a ** -0.5 * LOG2E, -(dh_b ** -0.5) * LOG2E)

        oa = jnp.zeros((m, seg), BF16)
        oa = _attn_a_prompt(qkv16, oa, lam_vecs, a_sub_norm, li, n_p, tq_a, tk_a, seg,
                            n_sub, dh_a, dv_a, lam_init)
        oa = _attn_a_sample(qkv16, oa, ck_a, cv_a, li, lam_vecs, a_sub_norm, n_p, ss,
                            seg, n_sub, dh_a, dv_a, lam_init)
        ob = jnp.zeros((m, seg), BF16)
        ob = _attn_b_prompt(qkv16, ob, tri, n_p, tq_b, seg, h_b, dh_b)
        ob = _attn_b_sample(qkv16, ob, ck_b, cv_b, li, tri, n_p, ss, seg, h_b, dh_b)

        x = _merge(x, oa, ob, proj32, wba16, wbb16, wo16, li, tm, tn, 4 * seg)
        x = _ffn(x, ffn2_norm, f2g, f2u, f2d, li, tm, tf)
        rows.append(proj32)

    def cache_rows(col, n_heads, dh):
        pr = jnp.stack([r[:n_p, col * seg:(col + 1) * seg] for r in rows], axis=0)
        sm = jnp.stack([r[n_p:, col * seg:(col + 1) * seg] for r in rows], axis=0)
        return (pr.reshape(depth, bp, sp, n_heads, dh), sm.reshape(depth, bs, ss, n_heads, dh))

    ak_p, ak_s = cache_rows(0, n_sub, dh_a)
    av_p, av_s = cache_rows(1, h_a, dv_a)
    bk_p, bk_s = cache_rows(2, h_b, dh_b)
    bv_p, bv_s = cache_rows(3, h_b, dh_b)
    return (x[:n_p].reshape(bp, sp, d), x[n_p:].reshape(bs, ss, d),
            ak_p, av_p, bk_p, bv_p, ak_s, av_s, bk_s, bv_s)
```

```python
import functools
import math

import numpy as np
import jax
import jax.numpy as jnp
from jax import lax
from jax.experimental import pallas as pl
from jax.experimental.pallas import tpu as pltpu

F32 = jnp.float32
BF16 = jnp.bfloat16

CHUNK = 64
ROPE_THETA = 10000.0
EPS = 1e-6
NEG_INF = -1e30
LOG2E = math.log2(math.e)
LANES = 128
VMEM_LIMIT = 56 * 1024 * 1024

NT_DIMS = (((1,), (1,)), ((), ()))


def _pick(n, candidates):
    for c in candidates:
        if n % c == 0:
            return c
    return n


def _params(*sem):
    return pltpu.CompilerParams(dimension_semantics=sem, vmem_limit_bytes=VMEM_LIMIT)


def _rmsnorm_rows(x_ref, g_ref, h_ref, copy_ref=None):
    rows = _pick(x_ref.shape[0], (64, 32, 16))
    g = g_ref[...]

    def chunk(c, carry):
        r = pl.ds(pl.multiple_of(c * rows, rows), rows)
        x = x_ref[r, :]
        ms = jnp.mean(x * x, axis=-1, keepdims=True)
        h_ref[r, :] = (x * lax.rsqrt(ms + EPS) * g).astype(BF16)
        if copy_ref is not None:
            copy_ref[r, :] = x
        return carry

    lax.fori_loop(0, x_ref.shape[0] // rows, chunk, 0)


def _ffn_kernel(x_ref, g_ref, wg_ref, wu_ref, wd_ref, o_ref, h_ref):
    @pl.when(pl.program_id(1) == 0)
    def _():
        _rmsnorm_rows(x_ref, g_ref, h_ref, copy_ref=o_ref)

    h = h_ref[...]
    g = jnp.dot(h, wg_ref[...], preferred_element_type=F32)
    u = jnp.dot(h, wu_ref[...], preferred_element_type=F32)
    a = (0.5 * g) * jax.nn.sigmoid(g) * u
    o_ref[...] += jnp.dot(a.astype(BF16), wd_ref[...], preferred_element_type=F32)


def _ffn(x, gain, wg, wu, wd, li, tm, tf):
    m, d = x.shape
    f = wg.shape[2]
    return pl.pallas_call(
        _ffn_kernel,
        grid=(m // tm, f // tf),
        in_specs=[
            pl.BlockSpec((tm, d), lambda i, j: (i, 0)),
            pl.BlockSpec((None, 1, d), lambda i, j: (li, 0, 0)),
            pl.BlockSpec((None, d, tf), lambda i, j: (li, 0, j)),
            pl.BlockSpec((None, d, tf), lambda i, j: (li, 0, j)),
            pl.BlockSpec((None, tf, d), lambda i, j: (li, j, 0)),
        ],
        out_specs=pl.BlockSpec((tm, d), lambda i, j: (i, 0)),
        out_shape=jax.ShapeDtypeStruct((m, d), F32),
        scratch_shapes=[pltpu.VMEM((tm, d), BF16)],
        compiler_params=_params("parallel", "arbitrary"),
        name="ffn",
    )(x, gain.reshape(gain.shape[0], 1, d), wg, wu, wd)


def _proj_kernel(x_ref, g_ref, w_ref, cos_ref, sin_ref, qn_ref, kn_ref,
                 o32_ref, o16_ref, h_ref, *, n_heads, dh, scale_a, scale_b):
    j = pl.program_id(1)

    @pl.when(j == 0)
    def _():
        _rmsnorm_rows(x_ref, g_ref, h_ref)

    acc = jnp.dot(h_ref[...], w_ref[...], preferred_element_type=F32)

    def norm_rope(gain_ref, scale):
        cos = cos_ref[...]
        sin = sin_ref[...]
        outs = []
        for hd in range(n_heads):
            xh = acc[:, hd * dh:(hd + 1) * dh]
            ms = jnp.mean(xh * xh, axis=-1, keepdims=True)
            y = xh * lax.rsqrt(ms + EPS) * gain_ref[...]
            y = y * cos + pltpu.roll(y, dh // 2, 1) * sin
            if scale is not None:
                y = y * scale
            outs.append(y)
        return jnp.concatenate(outs, axis=1)

    @pl.when(j == 0)
    def _():
        o16_ref[...] = norm_rope(qn_ref, scale_a).astype(BF16)

    @pl.when(j == 1)
    def _():
        y = norm_rope(kn_ref, None)
        o32_ref[...] = y
        o16_ref[...] = y.astype(BF16)

    @pl.when((j == 2) | (j == 4) | (j == 5))
    def _():
        o32_ref[...] = acc
        o16_ref[...] = acc.astype(BF16)

    @pl.when(j == 3)
    def _():
        o16_ref[...] = (acc * scale_b).astype(BF16)

    @pl.when(j >= 6)
    def _():
        o32_ref[...] = jax.nn.sigmoid(acc)


def _proj(x, gain, w_in, cos, sin, qn, kn, li, tm, seg, n_heads, dh, scale_a, scale_b):
    m, d = x.shape
    n = w_in.shape[2]
    nj = n // seg
    assert nj == 10 and n_heads * dh == seg

    def o32_map(i, j):
        return (i, jnp.maximum(j - 1 - (j >= 3).astype(jnp.int32), 0))

    def o16_map(i, j):
        return (i, jnp.minimum(j, 5))

    kern = functools.partial(_proj_kernel, n_heads=n_heads, dh=dh,
                             scale_a=scale_a, scale_b=scale_b)
    return pl.pallas_call(
        kern,
        grid=(m // tm, nj),
        in_specs=[
            pl.BlockSpec((tm, d), lambda i, j: (i, 0)),
            pl.BlockSpec((None, 1, d), lambda i, j: (li, 0, 0)),
            pl.BlockSpec((None, d, seg), lambda i, j: (li, 0, j)),
            pl.BlockSpec((tm, dh), lambda i, j: (i, 0)),
            pl.BlockSpec((tm, dh), lambda i, j: (i, 0)),
            pl.BlockSpec((None, 1, dh), lambda i, j: (li, 0, 0)),
            pl.BlockSpec((None, 1, dh), lambda i, j: (li, 0, 0)),
        ],
        out_specs=[
            pl.BlockSpec((tm, seg), o32_map),
            pl.BlockSpec((tm, seg), o16_map),
        ],
        out_shape=[
            jax.ShapeDtypeStruct((m, 8 * seg), F32),
            jax.ShapeDtypeStruct((m, 6 * seg), BF16),
        ],
        scratch_shapes=[pltpu.VMEM((tm, d), BF16)],
        compiler_params=_params("parallel", "arbitrary"),
        name="proj",
    )(x, gain.reshape(gain.shape[0], 1, d), w_in, cos, sin,
      qn.reshape(qn.shape[0], 1, dh), kn.reshape(kn.shape[0], 1, dh))


def _lam(lam_ref, lam_init):
    v = lam_ref[...]
    s1 = jnp.sum(v[0:1] * v[1:2], axis=1, keepdims=True)
    s2 = jnp.sum(v[2:3] * v[3:4], axis=1, keepdims=True)
    return jnp.exp(s1) - jnp.exp(s2) + lam_init


def _diff_combine(o0, o1, lam, gsub, lam_init):
    o = o0 - lam * o1
    ms = jnp.mean(o * o, axis=-1, keepdims=True)
    return (o * lax.rsqrt(ms + EPS) * gsub) * (1.0 - lam_init)


def _wide(col, n):
    reps = n // LANES
    return col if reps == 1 else jnp.concatenate([col] * reps, axis=1)


def _sb_logs(zn, mask):
    l = jnp.log(1.0 + jnp.exp2(-jnp.abs(zn))) * LOG2E
    lk = jnp.minimum(zn, 0.0) - l
    lsig = lk - zn
    if mask is not None:
        lk = jnp.where(mask, lk, 0.0)
    return lsig, lk


def _skewed(stages, n):
    for t in range(n + len(stages) - 1):
        for s in reversed(range(len(stages))):
            if 0 <= t - s < n:
                stages[s](t - s)


FIRST, LAST, MASKED = 1, 2, 4


def _causal_steps(n_rows, tq, tk, descending):
    qi, kj, fl = [], [], []
    for i in range(n_rows // tq):
        jmax = ((i + 1) * tq - 1) // tk
        ks = list(range(jmax, -1, -1) if descending else range(jmax + 1))
        for n, j in enumerate(ks):
            flags = (FIRST if n == 0 else 0) | (LAST if n == len(ks) - 1 else 0)
            if (j + 1) * tk > i * tq:
                flags |= MASKED
            qi.append(i)
            kj.append(j)
            fl.append(flags)
    return tuple(jnp.asarray(np.array(a, np.int32)) for a in (qi, kj, fl))


def _attn_a_kernel(qi_ref, kj_ref, fl_ref, q_ref, k_ref, v_ref, lam_ref, gsub_ref, o_ref,
                   m_scr, l_scr, acc_scr, *, n_sub, dh, dv, lam_init):
    step = pl.program_id(0)
    flags = fl_ref[step]
    tq = q_ref.shape[0]
    tk = k_ref.shape[0]

    @pl.when((flags & FIRST) != 0)
    def _():
        m_scr[...] = jnp.full(m_scr.shape, NEG_INF, F32)
        l_scr[...] = jnp.zeros(l_scr.shape, F32)
        acc_scr[...] = jnp.zeros(acc_scr.shape, F32)

    def body(masked):
        if masked:
            off = (qi_ref[step] * tq - kj_ref[step] * tk) // CHUNK
            r = lax.broadcasted_iota(jnp.int32, (tq, tk), 0)
            c = lax.broadcasted_iota(jnp.int32, (tq, tk), 1)
            mask = (c // CHUNK - r // CHUNK) <= off
        sc, pb, al = {}, {}, {}

        def scores(hd):
            for h in (2 * hd, 2 * hd + 1):
                q = q_ref[:, h * dh:(h + 1) * dh]
                k = k_ref[:, h * dh:(h + 1) * dh]
                s = lax.dot_general(q, k, NT_DIMS, preferred_element_type=F32)
                if masked:
                    s = jnp.where(mask, s, NEG_INF)
                sc[h] = s

        def softmax_update(hd):
            for h in (2 * hd, 2 * hd + 1):
                s = sc.pop(h)
                m_prev = m_scr[h]
                m_new = jnp.maximum(m_prev, jnp.max(s, axis=1, keepdims=True))
                alpha = jnp.exp2(m_prev - m_new)
                p = jnp.exp2(s - _wide(m_new, tk))
                l_scr[h] = alpha * l_scr[h] + jnp.sum(p, axis=1, keepdims=True)
                m_scr[h] = m_new
                al[h] = alpha
                pb[h] = p.astype(BF16)

        def values(hd):
            v = v_ref[:, hd * dv:(hd + 1) * dv]
            p2 = jnp.concatenate([pb.pop(2 * hd), pb.pop(2 * hd + 1)], axis=0)
            pv = jnp.dot(p2, v, preferred_element_type=F32)
            for u in range(2):
                h = 2 * hd + u
                acc_scr[h] = _wide(al.pop(h), dv) * acc_scr[h] + pv[u * tq:(u + 1) * tq]

        _skewed((scores, softmax_update, values), n_sub // 2)

    @pl.when((flags & MASKED) == 0)
    def _():
        body(False)

    @pl.when((flags & MASKED) != 0)
    def _():
        body(True)

    @pl.when((flags & LAST) != 0)
    def _():
        lam = _lam(lam_ref, lam_init)
        gsub = gsub_ref[...]
        for hd in range(n_sub // 2):
            o0 = acc_scr[2 * hd] / _wide(l_scr[2 * hd], dv)
            o1 = acc_scr[2 * hd + 1] / _wide(l_scr[2 * hd + 1], dv)
            o = _diff_combine(o0, o1, lam, gsub, lam_init)
            o_ref[:, hd * dv:(hd + 1) * dv] = o.astype(BF16)


def _attn_a_prompt(qkv16, oa, lam_vecs, gsub, li, n_rows, tq, tk, seg, n_sub, dh, dv, lam_init):
    qi, kj, fl = _causal_steps(n_rows, tq, tk, descending=False)
    kern = functools.partial(_attn_a_kernel, n_sub=n_sub, dh=dh, dv=dv, lam_init=lam_init)
    grid_spec = pltpu.PrefetchScalarGridSpec(
        num_scalar_prefetch=3,
        grid=(int(qi.shape[0]),),
        in_specs=[
            pl.BlockSpec((tq, seg), lambda s, qi, kj, fl: (qi[s], 0)),
            pl.BlockSpec((tk, seg), lambda s, qi, kj, fl: (kj[s], 1)),
            pl.BlockSpec((tk, seg), lambda s, qi, kj, fl: (kj[s], 2)),
            pl.BlockSpec((None, 4, dh), lambda s, qi, kj, fl: (li, 0, 0)),
            pl.BlockSpec((None, 1, dv), lambda s, qi, kj, fl: (li, 0, 0)),
            pl.BlockSpec(memory_space=pl.ANY),
        ],
        out_specs=pl.BlockSpec((tq, seg), lambda s, qi, kj, fl: (qi[s], 0)),
        scratch_shapes=[
            pltpu.VMEM((n_sub, tq, LANES), F32),
            pltpu.VMEM((n_sub, tq, LANES), F32),
            pltpu.VMEM((n_sub, tq, dv), F32),
        ],
    )

    def kern_alias(qi_ref, kj_ref, fl_ref, q_ref, k_ref, v_ref, lam_ref, gsub_ref, oa_in_ref,
                   o_ref, *scr):
        del oa_in_ref
        kern(qi_ref, kj_ref, fl_ref, q_ref, k_ref, v_ref, lam_ref, gsub_ref, o_ref, *scr)

    return pl.pallas_call(
        kern_alias,
        grid_spec=grid_spec,
        out_shape=jax.ShapeDtypeStruct(oa.shape, BF16),
        input_output_aliases={8: 0},
        compiler_params=_params("arbitrary"),
        name="attn_a_prompt",
    )(qi, kj, fl, qkv16, qkv16, qkv16, lam_vecs, gsub.reshape(gsub.shape[0], 1, dv), oa)


def _sb_heads(q_ref, k_of, v_of, tri, mask, run_scr, er_scr, acc_scr, n_heads, dh):
    zn, lsig, lkb, er, after = {}, {}, {}, {}, {}

    def logits(h):
        q = q_ref[:, h * dh:(h + 1) * dh]
        zn[h] = lax.dot_general(q, k_of(h), NT_DIMS, preferred_element_type=F32)

    def logs(h):
        lsig[h], lk = _sb_logs(zn.pop(h), mask)
        lkb[h] = lk.astype(BF16)
        run = run_scr[h] + jnp.sum(lk, axis=1, keepdims=True)
        run_scr[h] = run
        er[h] = er_scr[h]
        er_scr[h] = jnp.exp2(run)

    def cumsum(h):
        after[h] = jnp.dot(lkb.pop(h), tri, preferred_element_type=F32)

    def weights_pv(h):
        a = jnp.exp2(lsig.pop(h) + after.pop(h))
        if mask is not None:
            a = jnp.where(mask, a, 0.0)
        pv = jnp.dot(a.astype(BF16), v_of(h), preferred_element_type=F32)
        acc_scr[:, h * dh:(h + 1) * dh] += pv * _wide(er.pop(h), dh)

    _skewed((logits, logs, cumsum, weights_pv), n_heads)
    top = er_scr[0]
    for h in range(1, n_heads):
        top = jnp.maximum(top, er_scr[h])
    return jnp.max(top) > 0.0


def _sb_earlier_blocks(first_block, live, block_fn):
    def cond(carry):
        j, alive = carry
        return (j >= 0) & (alive != 0)

    def step(carry):
        j, _ = carry
        return j - 1, block_fn(j).astype(jnp.int32)

    lax.while_loop(cond, step, (first_block, live.astype(jnp.int32)))


def _attn_b_kernel(q_ref, k_ref, v_ref, tri_ref, o_ref, run_scr, er_scr, acc_scr,
                   *, n_heads, dh):
    qi = pl.program_id(0)
    tq = q_ref.shape[0]
    tk = tri_ref.shape[0]
    tri = tri_ref[...]
    run_scr[...] = jnp.zeros(run_scr.shape, F32)
    er_scr[...] = jnp.ones(er_scr.shape, F32)
    acc_scr[...] = jnp.zeros(acc_scr.shape, F32)

    def block(j, mask):
        rows = pl.ds(pl.multiple_of(j * tk, tk), tk)
        return _sb_heads(q_ref,
                         lambda h: k_ref[rows, h * dh:(h + 1) * dh],
                         lambda h: v_ref[rows, h * dh:(h + 1) * dh],
                         tri, mask, run_scr, er_scr, acc_scr, n_heads, dh)

    r = lax.broadcasted_iota(jnp.int32, (tq, tk), 0)
    c = lax.broadcasted_iota(jnp.int32, (tq, tk), 1)
    live = block(qi, c < r)
    _sb_earlier_blocks(qi - 1, live, lambda j: block(j, None))
    o_ref[...] = acc_scr[...].astype(BF16)


def _attn_b_prompt(qkv16, ob, tri, n_rows, tq, seg, n_heads, dh):
    assert tri.shape[0] == tq
    kern = functools.partial(_attn_b_kernel, n_heads=n_heads, dh=dh)
    resident = pl.Buffered(1)

    def kern_alias(q_ref, k_ref, v_ref, tri_ref, ob_in_ref, o_ref, *scr):
        del ob_in_ref
        kern(q_ref, k_ref, v_ref, tri_ref, o_ref, *scr)

    return pl.pallas_call(
        kern_alias,
        grid=(n_rows // tq,),
        in_specs=[
            pl.BlockSpec((tq, seg), lambda i: (i, 3)),
            pl.BlockSpec((n_rows, seg), lambda i: (0, 4), pipeline_mode=resident),
            pl.BlockSpec((n_rows, seg), lambda i: (0, 5), pipeline_mode=resident),
            pl.BlockSpec((tq, tq), lambda i: (0, 0), pipeline_mode=resident),
            pl.BlockSpec(memory_space=pl.ANY),
        ],
        out_specs=pl.BlockSpec((tq, seg), lambda i: (i, 0)),
        out_shape=jax.ShapeDtypeStruct(ob.shape, BF16),
        scratch_shapes=[
            pltpu.VMEM((n_heads, tq, LANES), F32),
            pltpu.VMEM((n_heads, tq, LANES), F32),
            pltpu.VMEM((tq, seg), F32),
        ],
        input_output_aliases={4: 0},
        compiler_params=_params("arbitrary"),
        name="attn_b_prompt",
    )(qkv16, qkv16, qkv16, tri, ob)


def _pad_rows(x, rows):
    n = x.shape[0]
    if n == rows:
        return x
    return jnp.concatenate([x, jnp.zeros((rows - n, x.shape[1]), x.dtype)], axis=0)


def _attn_a_sample_kernel(q_ref, kn_ref, vn_ref, kp_ref, vp_ref, lam_ref, gsub_ref, o_ref,
                          *, n_sub, dh, dv, lam_init):
    sq = q_ref.shape[0]
    past_len = kp_ref.shape[0]
    kn = _pad_rows(kn_ref[...], LANES)
    vn = _pad_rows(vn_ref[...], LANES)
    r = lax.broadcasted_iota(jnp.int32, (sq, LANES), 0)
    c = lax.broadcasted_iota(jnp.int32, (sq, LANES), 1)
    mask_new = ((past_len + c) // CHUNK <= (past_len + r) // CHUNK) & (c < sq)
    lam = _lam(lam_ref, lam_init)
    gsub = gsub_ref[...]
    outs = []
    for h in range(n_sub):
        hs = slice(h * dh, (h + 1) * dh)
        vs = slice((h // 2) * dv, (h // 2 + 1) * dv)
        q = q_ref[:, hs]
        sp = lax.dot_general(q, kp_ref[:, hs], NT_DIMS, preferred_element_type=F32)
        sn = lax.dot_general(q, kn[:, hs], NT_DIMS, preferred_element_type=F32)
        sn = jnp.where(mask_new, sn, NEG_INF)
        m = jnp.maximum(jnp.max(sp, axis=1, keepdims=True), jnp.max(sn, axis=1, keepdims=True))
        pp = jnp.exp2(sp - m)
        pn = jnp.exp2(sn - m)
        l = jnp.sum(pp, axis=1, keepdims=True) + jnp.sum(pn, axis=1, keepdims=True)
        o = (jnp.dot(pp.astype(BF16), vp_ref[:, vs], preferred_element_type=F32)
             + jnp.dot(pn.astype(BF16), vn[:, vs], preferred_element_type=F32))
        outs.append(o / l)
    for hd in range(n_sub // 2):
        o = _diff_combine(outs[2 * hd], outs[2 * hd + 1], lam, gsub, lam_init)
        o_ref[:, hd * dv:(hd + 1) * dv] = o.astype(BF16)


def _attn_a_sample(qkv16, oa, cache_k, cache_v, li, lam_vecs, gsub, row0, sq, seg,
                   n_sub, dh, dv, lam_init):
    _, nb, past_len, _ = cache_k.shape
    blk0 = row0 // sq
    kern = functools.partial(_attn_a_sample_kernel, n_sub=n_sub, dh=dh, dv=dv,
                             lam_init=lam_init)

    def kern_alias(q_ref, kn_ref, vn_ref, kp_ref, vp_ref, lam_ref, gsub_ref, oa_in_ref, o_ref):
        del oa_in_ref
        kern(q_ref, kn_ref, vn_ref, kp_ref, vp_ref, lam_ref, gsub_ref, o_ref)

    return pl.pallas_call(
        kern_alias,
        grid=(nb,),
        in_specs=[
            pl.BlockSpec((sq, seg), lambda b: (blk0 + b, 0)),
            pl.BlockSpec((sq, seg), lambda b: (blk0 + b, 1)),
            pl.BlockSpec((sq, seg), lambda b: (blk0 + b, 2)),
            pl.BlockSpec((None, None, past_len, seg), lambda b: (li, b, 0, 0)),
            pl.BlockSpec((None, None, past_len, seg), lambda b: (li, b, 0, 0)),
            pl.BlockSpec((None, 4, dh), lambda b: (li, 0, 0)),
            pl.BlockSpec((None, 1, dv), lambda b: (li, 0, 0)),
            pl.BlockSpec(memory_space=pl.ANY),
        ],
        out_specs=pl.BlockSpec((sq, seg), lambda b: (blk0 + b, 0)),
        out_shape=jax.ShapeDtypeStruct(oa.shape, BF16),
        input_output_aliases={7: 0},
        compiler_params=_params("arbitrary"),
        name="attn_a_sample",
    )(qkv16, qkv16, qkv16, cache_k, cache_v, lam_vecs, gsub.reshape(gsub.shape[0], 1, dv), oa)


def _attn_b_sample_kernel(q_ref, kn_ref, vn_ref, kp_ref, vp_ref, tri_ref, o_ref,
                          run_scr, er_scr, acc_scr, *, n_heads, dh):
    sq = q_ref.shape[0]
    past_len = kp_ref.shape[0]
    tk = tri_ref.shape[0]
    run_scr[...] = jnp.zeros(run_scr.shape, F32)
    er_scr[...] = jnp.ones(er_scr.shape, F32)
    acc_scr[...] = jnp.zeros(acc_scr.shape, F32)
    kn = _pad_rows(kn_ref[...], LANES)
    vn = _pad_rows(vn_ref[...], LANES)
    r = lax.broadcasted_iota(jnp.int32, (sq, LANES), 0)
    c = lax.broadcasted_iota(jnp.int32, (sq, LANES), 1)
    live = _sb_heads(q_ref,
                     lambda h: kn[:, h * dh:(h + 1) * dh],
                     lambda h: vn[:, h * dh:(h + 1) * dh],
                     tri_ref[0:LANES, 0:LANES], c < r, run_scr, er_scr, acc_scr, n_heads, dh)

    def block(j):
        rows = pl.ds(pl.multiple_of(j * tk, tk), tk)
        return _sb_heads(q_ref,
                         lambda h: kp_ref[rows, h * dh:(h + 1) * dh],
                         lambda h: vp_ref[rows, h * dh:(h + 1) * dh],
                         tri_ref[...], None, run_scr, er_scr, acc_scr, n_heads, dh)

    _sb_earlier_blocks(past_len // tk - 1, live, block)
    o_ref[...] = acc_scr[...].astype(BF16)


def _attn_b_sample(qkv16, ob, cache_k, cache_v, li, tri, row0, sq, seg, n_heads, dh):
    _, nb, past_len, _ = cache_k.shape
    tk = tri.shape[1]
    assert past_len % tk == 0 and tk >= LANES
    blk0 = row0 // sq
    kern = functools.partial(_attn_b_sample_kernel, n_heads=n_heads, dh=dh)

    def kern_alias(q_ref, kn_ref, vn_ref, kp_ref, vp_ref, tri_ref, ob_in_ref, o_ref, *scr):
        del ob_in_ref
        kern(q_ref, kn_ref, vn_ref, kp_ref, vp_ref, tri_ref, o_ref, *scr)

    return pl.pallas_call(
        kern_alias,
        grid=(nb,),
        in_specs=[
            pl.BlockSpec((sq, seg), lambda b: (blk0 + b, 3)),
            pl.BlockSpec((sq, seg), lambda b: (blk0 + b, 4)),
            pl.BlockSpec((sq, seg), lambda b: (blk0 + b, 5)),
            pl.BlockSpec((None, None, past_len, seg), lambda b: (li, b, 0, 0)),
            pl.BlockSpec((None, None, past_len, seg), lambda b: (li, b, 0, 0)),
            pl.BlockSpec((tk, tk), lambda b: (0, 0)),
            pl.BlockSpec(memory_space=pl.ANY),
        ],
        out_specs=pl.BlockSpec((sq, seg), lambda b: (blk0 + b, 0)),
        out_shape=jax.ShapeDtypeStruct(ob.shape, BF16),
        scratch_shapes=[
            pltpu.VMEM((n_heads, sq, LANES), F32),
            pltpu.VMEM((n_heads, sq, LANES), F32),
            pltpu.VMEM((sq, seg), F32),
        ],
        input_output_aliases={6: 0},
        compiler_params=_params("arbitrary"),
        name="attn_b_sample",
    )(qkv16, qkv16, qkv16, cache_k, cache_v, tri, ob)


def _merge_kernel(x_ref, oa_ref, ob_ref, ga_ref, gb_ref, wba_ref, wbb_ref, wo_ref, o_ref):
    @pl.when(pl.program_id(1) == 0)
    def _():
        o_ref[...] = x_ref[...]

    ta = jnp.dot(oa_ref[...], wba_ref[...], preferred_element_type=F32)
    tb = jnp.dot(ob_ref[...], wbb_ref[...], preferred_element_type=F32)
    merged = ga_ref[...] * ta + gb_ref[...] * tb
    o_ref[...] += jnp.dot(merged.astype(BF16), wo_ref[...], preferred_element_type=F32)


def _merge(x, oa, ob, proj32, wba, wbb, wo, li, tm, tn, gate_col0):
    m, d = x.shape
    ka = oa.shape[1]
    kb = ob.shape[1]
    nj = d // tn
    ga0 = gate_col0 // tn
    gb0 = ga0 + nj
    return pl.pallas_call(
        _merge_kernel,
        grid=(m // tm, nj),
        in_specs=[
            pl.BlockSpec((tm, d), lambda i, j: (i, 0)),
            pl.BlockSpec((tm, ka), lambda i, j: (i, 0)),
            pl.BlockSpec((tm, kb), lambda i, j: (i, 0)),
            pl.BlockSpec((tm, tn), lambda i, j: (i, ga0 + j)),
            pl.BlockSpec((tm, tn), lambda i, j: (i, gb0 + j)),
            pl.BlockSpec((None, ka, tn), lambda i, j: (li, 0, j)),
            pl.BlockSpec((None, kb, tn), lambda i, j: (li, 0, j)),
            pl.BlockSpec((None, tn, d), lambda i, j: (li, j, 0)),
        ],
        out_specs=pl.BlockSpec((tm, d), lambda i, j: (i, 0)),
        out_shape=jax.ShapeDtypeStruct((m, d), F32),
        compiler_params=_params("parallel", "arbitrary"),
        name="merge",
    )(x, oa, ob, proj32, proj32, wba, wbb, wo)


def _rope_tables(pos, dh):
    inv = 1.0 / (ROPE_THETA ** (jnp.arange(0, dh, 2, dtype=F32) / dh))
    ang = pos.astype(F32)[:, None] * inv[None, :]
    cos = jnp.cos(ang)
    sin = jnp.sin(ang)
    return jnp.concatenate([cos, cos], axis=1), jnp.concatenate([-sin, sin], axis=1)


def kernel(x_prompt, x_sample, cache_a_k, cache_a_v, cache_b_k, cache_b_v,
           ffn1_norm, ffn1_w_gate, ffn1_w_up, ffn1_w_down,
           mix_norm, w_in, a_q_norm, a_k_norm,
           a_lam_q1, a_lam_k1, a_lam_q2, a_lam_k2, a_sub_norm,
           w_branch_a, w_branch_b, w_out,
           ffn2_norm, ffn2_w_gate, ffn2_w_up, ffn2_w_down):
    bp, sp, d = x_prompt.shape
    bs, ss, _ = x_sample.shape
    depth, _, past_len, n_sub, dh_a = cache_a_k.shape
    h_a, dv_a = cache_a_v.shape[3:]
    h_b, dh_b = cache_b_k.shape[3:]
    seg = n_sub * dh_a
    assert bp == 1 and h_a * dv_a == seg and h_b * dh_b == seg and n_sub == 2 * h_a
    assert w_in.shape[2] == 6 * seg + 2 * d and d == 2 * seg

    n_p = bp * sp
    m = n_p + bs * ss
    tm = _pick(m, (768, 512, 384, 256, 128, 64, 32))
    tf = _pick(ffn1_w_gate.shape[2], (512, 256, 128))
    tn = _pick(d, (512, 256, 128))
    tq_a, tk_a = _pick(sp, (256, 128)), _pick(sp, (512, 256, 128))
    tq_b = tk_b = _pick(math.gcd(sp, past_len), (256, 128))
    assert n_p % ss == 0 and min(tq_a, tk_a, tq_b) % CHUNK == 0

    x = jnp.concatenate([x_prompt.reshape(n_p, d), x_sample.reshape(bs * ss, d)], axis=0)
    pos = jnp.concatenate([jnp.arange(sp)] * bp + [past_len + jnp.arange(ss)] * bs)
    cos, sin = _rope_tables(pos, dh_a)
    tri = jnp.asarray(np.tril(np.ones((tk_b, tk_b), np.float32), -1), BF16)
    lam_vecs = jnp.stack([a_lam_q1, a_lam_k1, a_lam_q2, a_lam_k2], axis=1)

    bf = lambda w: w.astype(BF16)
    f1g, f1u, f1d = bf(ffn1_w_gate), bf(ffn1_w_up), bf(ffn1_w_down)
    f2g, f2u, f2d = bf(ffn2_w_gate), bf(ffn2_w_up), bf(ffn2_w_down)
    w_in16, wba16, wbb16, wo16 = bf(w_in), bf(w_branch_a), bf(w_branch_b), bf(w_out)
    flat = lambda c: bf(c.reshape(depth, bs, past_len, seg))
    ck_a, cv_a, ck_b, cv_b = flat(cache_a_k), flat(cache_a_v), flat(cache_b_k), flat(cache_b_v)

    rows = []
    for li in range(depth):
        lam_init = 0.8 - 0.6 * math.exp(-0.3 * li)

        x = _ffn(x, ffn1_norm, f1g, f1u, f1d, li, tm, tf)
        proj32, qkv16 = _proj(x, mix_norm, w_in16, cos, sin, a_q_norm, a_k_norm, li,
                              tm, seg, n_sub, dh_a,
                              dh_a ** -0.5 * LOG2E, -(dh_b ** -0.5) * LOG2E)

        oa = jnp.zeros((m, seg), BF16)
        oa = _attn_a_prompt(qkv16, oa, lam_vecs, a_sub_norm, li, n_p, tq_a, tk_a, seg,
                            n_sub, dh_a, dv_a, lam_init)
        oa = _attn_a_sample(qkv16, oa, ck_a, cv_a, li, lam_vecs, a_sub_norm, n_p, ss,
                            seg, n_sub, dh_a, dv_a, lam_init)
        ob = jnp.zeros((m, seg), BF16)
        ob = _attn_b_prompt(qkv16, ob, tri, n_p, tq_b, seg, h_b, dh_b)
        ob = _attn_b_sample(qkv16, ob, ck_b, cv_b, li, tri, n_p, ss, seg, h_b, dh_b)

        x = _merge(x, oa, ob, proj32, wba16, wbb16, wo16, li, tm, tn, 4 * seg)
        x = _ffn(x, ffn2_norm, f2g, f2u, f2d, li, tm, tf)
        rows.append(proj32)

    def cache_rows(col, n_heads, dh):
        pr = jnp.stack([r[:n_p, col * seg:(col + 1) * seg] for r in rows], axis=0)
        sm = jnp.stack([r[n_p:, col * seg:(col + 1) * seg] for r in rows], axis=0)
        return (pr.reshape(depth, bp, sp, n_heads, dh), sm.reshape(depth, bs, ss, n_heads, dh))

    ak_p, ak_s = cache_rows(0, n_sub, dh_a)
    av_p, av_s = cache_rows(1, h_a, dv_a)
    bk_p, bk_s = cache_rows(2, h_b, dh_b)
    bv_p, bv_s = cache_rows(3, h_b, dh_b)
    return (x[:n_p].reshape(bp, sp, d), x[n_p:].reshape(bs, ss, d),
            ak_p, av_p, bk_p, bv_p, ak_s, av_s, bk_s, bv_s)
```

```python
import functools
import math

import numpy as np
import jax
import jax.numpy as jnp
from jax import lax
from jax.experimental import pallas as pl
from jax.experimental.pallas import tpu as pltpu

F32 = jnp.float32
BF16 = jnp.bfloat16

CHUNK = 64
ROPE_THETA = 10000.0
EPS = 1e-6
NEG_INF = -1e30
LOG2E = math.log2(math.e)
LANES = 128
VMEM_LIMIT = 56 * 1024 * 1024

NT_DIMS = (((1,), (1,)), ((), ()))


def _pick(n, candidates):
    for c in candidates:
        if n % c == 0:
            return c
    return n


def _params(*sem):
    return pltpu.CompilerParams(dimension_semantics=sem, vmem_limit_bytes=VMEM_LIMIT)


def _rmsnorm_rows(x_ref, g_ref, h_ref, copy_ref=None):
    rows = _pick(x_ref.shape[0], (64, 32, 16))
    g = g_ref[...]

    def chunk(c, carry):
        r = pl.ds(pl.multiple_of(c * rows, rows), rows)
        x = x_ref[r, :]
        ms = jnp.mean(x * x, axis=-1, keepdims=True)
        h_ref[r, :] = (x * lax.rsqrt(ms + EPS) * g).astype(BF16)
        if copy_ref is not None:
            copy_ref[r, :] = x
        return carry

    lax.fori_loop(0, x_ref.shape[0] // rows, chunk, 0)


def _ffn_kernel(x_ref, g_ref, wg_ref, wu_ref, wd_ref, o_ref, h_ref):
    @pl.when(pl.program_id(1) == 0)
    def _():
        _rmsnorm_rows(x_ref, g_ref, h_ref, copy_ref=o_ref)

    h = h_ref[...]
    g = jnp.dot(h, wg_ref[...], preferred_element_type=F32)
    u = jnp.dot(h, wu_ref[...], preferred_element_type=F32)
    a = (0.5 * g) * jax.nn.sigmoid(g) * u
    o_ref[...] += jnp.dot(a.astype(BF16), wd_ref[...], preferred_element_type=F32)


def _ffn(x, gain, wg, wu, wd, li, tm, tf):
    m, d = x.shape
    f = wg.shape[2]
    return pl.pallas_call(
        _ffn_kernel,
        grid=(m // tm, f // tf),
        in_specs=[
            pl.BlockSpec((tm, d), lambda i, j: (i, 0)),
            pl.BlockSpec((None, 1, d), lambda i, j: (li, 0, 0)),
            pl.BlockSpec((None, d, tf), lambda i, j: (li, 0, j)),
            pl.BlockSpec((None, d, tf), lambda i, j: (li, 0, j)),
            pl.BlockSpec((None, tf, d), lambda i, j: (li, j, 0)),
        ],
        out_specs=pl.BlockSpec((tm, d), lambda i, j: (i, 0)),
        out_shape=jax.ShapeDtypeStruct((m, d), F32),
        scratch_shapes=[pltpu.VMEM((tm, d), BF16)],
        compiler_params=_params("parallel", "arbitrary"),
        name="ffn",
    )(x, gain.reshape(gain.shape[0], 1, d), wg, wu, wd)


def _store_heads(y, kv_ref, n_heads, dh):
    for h in range(n_heads):
        kv_ref[pl.ds(h, y.shape[0], stride=n_heads), :] = y[:, h * dh:(h + 1) * dh]


def _proj_kernel(x_ref, g_ref, w_ref, cos_ref, sin_ref, qn_ref, kn_ref, *refs,
                 n_heads, dh, scale_a, scale_b):
    o32_ref, o16_ref, ka_ref, kb_ref, vb_ref, h_ref = refs[-6:]
    j = pl.program_id(1)

    @pl.when(j == 0)
    def _():
        _rmsnorm_rows(x_ref, g_ref, h_ref)

    acc = jnp.dot(h_ref[...], w_ref[...], preferred_element_type=F32)

    def norm_rope(gain_ref, scale):
        cos = cos_ref[...]
        sin = sin_ref[...]
        outs = []
        for hd in range(n_heads):
            xh = acc[:, hd * dh:(hd + 1) * dh]
            ms = jnp.mean(xh * xh, axis=-1, keepdims=True)
            y = xh * lax.rsqrt(ms + EPS) * gain_ref[...]
            y = y * cos + pltpu.roll(y, dh // 2, 1) * sin
            if scale is not None:
                y = y * scale
            outs.append(y)
        return jnp.concatenate(outs, axis=1)

    @pl.when(j == 0)
    def _():
        o16_ref[...] = norm_rope(qn_ref, scale_a).astype(BF16)

    @pl.when(j == 1)
    def _():
        y = norm_rope(kn_ref, None)
        o16_ref[...] = y.astype(BF16)
        _store_heads(y, ka_ref, n_heads, dh)

    @pl.when(j == 2)
    def _():
        o32_ref[...] = acc
        o16_ref[...] = acc.astype(BF16)

    @pl.when(j == 3)
    def _():
        o16_ref[...] = (acc * scale_b).astype(BF16)

    @pl.when(j == 4)
    def _():
        o16_ref[...] = acc.astype(BF16)
        _store_heads(acc, kb_ref, n_heads, dh)

    @pl.when(j == 5)
    def _():
        o16_ref[...] = acc.astype(BF16)
        _store_heads(acc, vb_ref, n_heads, dh)

    @pl.when(j >= 6)
    def _():
        o32_ref[...] = jax.nn.sigmoid(acc)


def _proj(x, gain, w_in, cos, sin, qn, kn, shared, kv, li, depth, row0, n_rows, tm,
          seg, n_heads, dh, scale_a, scale_b):
    m, d = x.shape
    nj = w_in.shape[2] // seg
    assert nj == 10 and n_heads * dh == seg and n_rows % tm == 0 and row0 % tm == 0
    blk0 = row0 // tm

    def o32_map(i, j):
        return (blk0 + i, jnp.where(j >= 6, j - 5, 0))

    def o16_map(i, j):
        return (blk0 + i, jnp.minimum(j, 5))

    kern = functools.partial(_proj_kernel, n_heads=n_heads, dh=dh,
                             scale_a=scale_a, scale_b=scale_b)
    once = pl.Buffered(1)
    kv_spec = pl.BlockSpec((None, tm * n_heads, dh), lambda i, j: (li, i, 0), pipeline_mode=once)
    kv_shape = jax.ShapeDtypeStruct((depth, n_rows * n_heads, dh), F32)
    in_specs = [
        pl.BlockSpec((tm, d), lambda i, j: (blk0 + i, 0), pipeline_mode=once),
        pl.BlockSpec((None, 1, d), lambda i, j: (li, 0, 0)),
        pl.BlockSpec((None, d, seg), lambda i, j: (li, 0, j)),
        pl.BlockSpec((tm, dh), lambda i, j: (blk0 + i, 0)),
        pl.BlockSpec((tm, dh), lambda i, j: (blk0 + i, 0)),
        pl.BlockSpec((None, 1, dh), lambda i, j: (li, 0, 0)),
        pl.BlockSpec((None, 1, dh), lambda i, j: (li, 0, 0)),
    ]
    args = [x, gain.reshape(gain.shape[0], 1, d), w_in, cos, sin,
            qn.reshape(qn.shape[0], 1, dh), kn.reshape(kn.shape[0], 1, dh)]
    aliases = {}
    for out0, group in ((0, shared), (2, kv)):
        for k, a in enumerate(group or ()):
            in_specs.append(pl.BlockSpec(memory_space=pl.ANY))
            aliases[len(args)] = out0 + k
            args.append(a)
    return pl.pallas_call(
        kern,
        grid=(n_rows // tm, nj),
        in_specs=in_specs,
        out_specs=[
            pl.BlockSpec((tm, seg), o32_map),
            pl.BlockSpec((tm, seg), o16_map),
            kv_spec, kv_spec, kv_spec,
        ],
        out_shape=[
            jax.ShapeDtypeStruct((m, 5 * seg), F32),
            jax.ShapeDtypeStruct((m, 6 * seg), BF16),
            kv_shape, kv_shape, kv_shape,
        ],
        scratch_shapes=[pltpu.VMEM((tm, d), BF16)],
        input_output_aliases=aliases,
        compiler_params=_params("arbitrary", "arbitrary"),
        name="proj",
    )(*args)


def _lam(lam_ref, lam_init):
    v = lam_ref[...]
    s1 = jnp.sum(v[0:1] * v[1:2], axis=1, keepdims=True)
    s2 = jnp.sum(v[2:3] * v[3:4], axis=1, keepdims=True)
    return jnp.exp(s1) - jnp.exp(s2) + lam_init


def _diff_combine(o0, o1, lam, gsub, lam_init):
    o = o0 - lam * o1
    ms = jnp.mean(o * o, axis=-1, keepdims=True)
    return (o * lax.rsqrt(ms + EPS) * gsub) * (1.0 - lam_init)


def _wide(col, n):
    reps = n // LANES
    return col if reps == 1 else jnp.concatenate([col] * reps, axis=1)


def _sb_logs(zn, mask):
    l = jnp.log(1.0 + jnp.exp2(-jnp.abs(zn))) * LOG2E
    lk = jnp.minimum(zn, 0.0) - l
    lsig = lk - zn
    if mask is not None:
        lk = jnp.where(mask, lk, 0.0)
    return lsig, lk


def _skewed(stages, n):
    for t in range(n + len(stages) - 1):
        for s in reversed(range(len(stages))):
            if 0 <= t - s < n:
                stages[s](t - s)


FIRST, LAST, MASKED = 1, 2, 4


def _causal_steps(n_rows, tq, tk, descending):
    qi, kj, fl = [], [], []
    for i in range(n_rows // tq):
        jmax = ((i + 1) * tq - 1) // tk
        ks = list(range(jmax, -1, -1) if descending else range(jmax + 1))
        for n, j in enumerate(ks):
            flags = (FIRST if n == 0 else 0) | (LAST if n == len(ks) - 1 else 0)
            if (j + 1) * tk > i * tq:
                flags |= MASKED
            qi.append(i)
            kj.append(j)
            fl.append(flags)
    return tuple(jnp.asarray(np.array(a, np.int32)) for a in (qi, kj, fl))


def _attn_a_kernel(qi_ref, kj_ref, fl_ref, q_ref, k_ref, v_ref, lam_ref, gsub_ref, o_ref,
                   m_scr, l_scr, acc_scr, *, n_sub, dh, dv, lam_init):
    step = pl.program_id(0)
    flags = fl_ref[step]
    tq = q_ref.shape[0]
    tk = k_ref.shape[0]

    @pl.when((flags & FIRST) != 0)
    def _():
        m_scr[...] = jnp.full(m_scr.shape, NEG_INF, F32)
        l_scr[...] = jnp.zeros(l_scr.shape, F32)
        acc_scr[...] = jnp.zeros(acc_scr.shape, F32)

    def body(masked):
        if masked:
            off = (qi_ref[step] * tq - kj_ref[step] * tk) // CHUNK
            r = lax.broadcasted_iota(jnp.int32, (tq, tk), 0)
            c = lax.broadcasted_iota(jnp.int32, (tq, tk), 1)
            mask = (c // CHUNK - r // CHUNK) <= off
        sc, pb, al = {}, {}, {}

        def scores(hd):
            for h in (2 * hd, 2 * hd + 1):
                q = q_ref[:, h * dh:(h + 1) * dh]
                k = k_ref[:, h * dh:(h + 1) * dh]
                s = lax.dot_general(q, k, NT_DIMS, preferred_element_type=F32)
                if masked:
                    s = jnp.where(mask, s, NEG_INF)
                sc[h] = s

        def softmax_update(hd):
            for h in (2 * hd, 2 * hd + 1):
                s = sc.pop(h)
                m_prev = m_scr[h]
                m_new = jnp.maximum(m_prev, jnp.max(s, axis=1, keepdims=True))
                alpha = jnp.exp2(m_prev - m_new)
                p = jnp.exp2(s - _wide(m_new, tk))
                l_scr[h] = alpha * l_scr[h] + jnp.sum(p, axis=1, keepdims=True)
                m_scr[h] = m_new
                al[h] = alpha
                pb[h] = p.astype(BF16)

        def values(hd):
            v = v_ref[:, hd * dv:(hd + 1) * dv]
            p2 = jnp.concatenate([pb.pop(2 * hd), pb.pop(2 * hd + 1)], axis=0)
            pv = jnp.dot(p2, v, preferred_element_type=F32)
            for u in range(2):
                h = 2 * hd + u
                acc_scr[h] = _wide(al.pop(h), dv) * acc_scr[h] + pv[u * tq:(u + 1) * tq]

        _skewed((scores, softmax_update, values), n_sub // 2)

    @pl.when((flags & MASKED) == 0)
    def _():
        body(False)

    @pl.when((flags & MASKED) != 0)
    def _():
        body(True)

    @pl.when((flags & LAST) != 0)
    def _():
        lam = _lam(lam_ref, lam_init)
        gsub = gsub_ref[...]
        for hd in range(n_sub // 2):
            o0 = acc_scr[2 * hd] / _wide(l_scr[2 * hd], dv)
            o1 = acc_scr[2 * hd + 1] / _wide(l_scr[2 * hd + 1], dv)
            o = _diff_combine(o0, o1, lam, gsub, lam_init)
            o_ref[:, hd * dv:(hd + 1) * dv] = o.astype(BF16)


def _attn_a_prompt(qkv16, oa, lam_vecs, gsub, li, n_rows, tq, tk, seg, n_sub, dh, dv, lam_init):
    qi, kj, fl = _causal_steps(n_rows, tq, tk, descending=False)
    kern = functools.partial(_attn_a_kernel, n_sub=n_sub, dh=dh, dv=dv, lam_init=lam_init)
    grid_spec = pltpu.PrefetchScalarGridSpec(
        num_scalar_prefetch=3,
        grid=(int(qi.shape[0]),),
        in_specs=[
            pl.BlockSpec((tq, seg), lambda s, qi, kj, fl: (qi[s], 0)),
            pl.BlockSpec((tk, seg), lambda s, qi, kj, fl: (kj[s], 1)),
            pl.BlockSpec((tk, seg), lambda s, qi, kj, fl: (kj[s], 2)),
            pl.BlockSpec((None, 4, dh), lambda s, qi, kj, fl: (li, 0, 0)),
            pl.BlockSpec((None, 1, dv), lambda s, qi, kj, fl: (li, 0, 0)),
            pl.BlockSpec(memory_space=pl.ANY),
        ],
        out_specs=pl.BlockSpec((tq, seg), lambda s, qi, kj, fl: (qi[s], 0)),
        scratch_shapes=[
            pltpu.VMEM((n_sub, tq, LANES), F32),
            pltpu.VMEM((n_sub, tq, LANES), F32),
            pltpu.VMEM((n_sub, tq, dv), F32),
        ],
    )

    def kern_alias(qi_ref, kj_ref, fl_ref, q_ref, k_ref, v_ref, lam_ref, gsub_ref, oa_in_ref,
                   o_ref, *scr):
        del oa_in_ref
        kern(qi_ref, kj_ref, fl_ref, q_ref, k_ref, v_ref, lam_ref, gsub_ref, o_ref, *scr)

    return pl.pallas_call(
        kern_alias,
        grid_spec=grid_spec,
        out_shape=jax.ShapeDtypeStruct(oa.shape, BF16),
        input_output_aliases={8: 0},
        compiler_params=_params("arbitrary"),
        name="attn_a_prompt",
    )(qi, kj, fl, qkv16, qkv16, qkv16, lam_vecs, gsub.reshape(gsub.shape[0], 1, dv), oa)


def _sb_heads(q_ref, k_of, v_of, tri, mask, run_scr, er_scr, acc_scr, n_heads, dh):
    zn, lsig, lkb, er, after = {}, {}, {}, {}, {}

    def logits(h):
        q = q_ref[:, h * dh:(h + 1) * dh]
        zn[h] = lax.dot_general(q, k_of(h), NT_DIMS, preferred_element_type=F32)

    def logs(h):
        lsig[h], lk = _sb_logs(zn.pop(h), mask)
        lkb[h] = lk.astype(BF16)
        run = run_scr[h] + jnp.sum(lk, axis=1, keepdims=True)
        run_scr[h] = run
        er[h] = er_scr[h]
        er_scr[h] = jnp.exp2(run)

    def cumsum(h):
        after[h] = jnp.dot(lkb.pop(h), tri, preferred_element_type=F32)

    def weights_pv(h):
        a = jnp.exp2(lsig.pop(h) + after.pop(h))
        if mask is not None:
            a = jnp.where(mask, a, 0.0)
        pv = jnp.dot(a.astype(BF16), v_of(h), preferred_element_type=F32)
        acc_scr[:, h * dh:(h + 1) * dh] += pv * _wide(er.pop(h), dh)

    _skewed((logits, logs, cumsum, weights_pv), n_heads)
    top = er_scr[0]
    for h in range(1, n_heads):
        top = jnp.maximum(top, er_scr[h])
    return jnp.max(top) > 0.0


def _sb_earlier_blocks(first_block, live, block_fn):
    def cond(carry):
        j, alive = carry
        return (j >= 0) & (alive != 0)

    def step(carry):
        j, _ = carry
        return j - 1, block_fn(j).astype(jnp.int32)

    lax.while_loop(cond, step, (first_block, live.astype(jnp.int32)))


def _attn_b_kernel(q_ref, k_ref, v_ref, tri_ref, o_ref, run_scr, er_scr, acc_scr,
                   *, n_heads, dh):
    qi = pl.program_id(0)
    tq = q_ref.shape[0]
    tk = tri_ref.shape[0]
    tri = tri_ref[...]
    run_scr[...] = jnp.zeros(run_scr.shape, F32)
    er_scr[...] = jnp.ones(er_scr.shape, F32)
    acc_scr[...] = jnp.zeros(acc_scr.shape, F32)

    def block(j, mask):
        rows = pl.ds(pl.multiple_of(j * tk, tk), tk)
        return _sb_heads(q_ref,
                         lambda h: k_ref[rows, h * dh:(h + 1) * dh],
                         lambda h: v_ref[rows, h * dh:(h + 1) * dh],
                         tri, mask, run_scr, er_scr, acc_scr, n_heads, dh)

    r = lax.broadcasted_iota(jnp.int32, (tq, tk), 0)
    c = lax.broadcasted_iota(jnp.int32, (tq, tk), 1)
    live = block(qi, c < r)
    _sb_earlier_blocks(qi - 1, live, lambda j: block(j, None))
    o_ref[...] = acc_scr[...].astype(BF16)


def _attn_b_prompt(qkv16, ob, tri, n_rows, tq, seg, n_heads, dh):
    assert tri.shape[0] == tq
    kern = functools.partial(_attn_b_kernel, n_heads=n_heads, dh=dh)
    resident = pl.Buffered(1)

    def kern_alias(q_ref, k_ref, v_ref, tri_ref, ob_in_ref, o_ref, *scr):
        del ob_in_ref
        kern(q_ref, k_ref, v_ref, tri_ref, o_ref, *scr)

    return pl.pallas_call(
        kern_alias,
        grid=(n_rows // tq,),
        in_specs=[
            pl.BlockSpec((tq, seg), lambda i: (i, 3)),
            pl.BlockSpec((n_rows, seg), lambda i: (0, 4), pipeline_mode=resident),
            pl.BlockSpec((n_rows, seg), lambda i: (0, 5), pipeline_mode=resident),
            pl.BlockSpec((tq, tq), lambda i: (0, 0), pipeline_mode=resident),
            pl.BlockSpec(memory_space=pl.ANY),
        ],
        out_specs=pl.BlockSpec((tq, seg), lambda i: (i, 0)),
        out_shape=jax.ShapeDtypeStruct(ob.shape, BF16),
        scratch_shapes=[
            pltpu.VMEM((n_heads, tq, LANES), F32),
            pltpu.VMEM((n_heads, tq, LANES), F32),
            pltpu.VMEM((tq, seg), F32),
        ],
        input_output_aliases={4: 0},
        compiler_params=_params("arbitrary"),
        name="attn_b_prompt",
    )(qkv16, qkv16, qkv16, tri, ob)


def _pad_rows(x, rows):
    n = x.shape[0]
    if n == rows:
        return x
    return jnp.concatenate([x, jnp.zeros((rows - n, x.shape[1]), x.dtype)], axis=0)


def _attn_a_sample_kernel(q_ref, kn_ref, vn_ref, kp_ref, vp_ref, lam_ref, gsub_ref, o_ref,
                          *, n_sub, dh, dv, lam_init):
    sq = q_ref.shape[0]
    past_len = kp_ref.shape[0] // n_sub
    kn = _pad_rows(kn_ref[...], LANES)
    vn = _pad_rows(vn_ref[...], LANES)
    r = lax.broadcasted_iota(jnp.int32, (sq, LANES), 0)
    c = lax.broadcasted_iota(jnp.int32, (sq, LANES), 1)
    mask_new = ((past_len + c) // CHUNK <= (past_len + r) // CHUNK) & (c < sq)
    lam = _lam(lam_ref, lam_init)
    gsub = gsub_ref[...]
    outs = []
    for h in range(n_sub):
        hs = slice(h * dh, (h + 1) * dh)
        vs = slice((h // 2) * dv, (h // 2 + 1) * dv)
        q = q_ref[:, hs]
        kp = kp_ref[pl.ds(h, past_len, stride=n_sub), :].astype(BF16)
        sp = lax.dot_general(q, kp, NT_DIMS, preferred_element_type=F32)
        sn = lax.dot_general(q, kn[:, hs], NT_DIMS, preferred_element_type=F32)
        sn = jnp.where(mask_new, sn, NEG_INF)
        m = jnp.maximum(jnp.max(sp, axis=1, keepdims=True), jnp.max(sn, axis=1, keepdims=True))
        pp = jnp.exp2(sp - m)
        pn = jnp.exp2(sn - m)
        l = jnp.sum(pp, axis=1, keepdims=True) + jnp.sum(pn, axis=1, keepdims=True)
        o = (jnp.dot(pp.astype(BF16), vp_ref[:, vs], preferred_element_type=F32)
             + jnp.dot(pn.astype(BF16), vn[:, vs], preferred_element_type=F32))
        outs.append(o / l)
    for hd in range(n_sub // 2):
        o = _diff_combine(outs[2 * hd], outs[2 * hd + 1], lam, gsub, lam_init)
        o_ref[:, hd * dv:(hd + 1) * dv] = o.astype(BF16)


def _attn_a_sample(qkv16, oa, cache_k, cache_v, li, lam_vecs, gsub, row0, sq, seg,
                   n_sub, dh, dv, lam_init):
    _, nb, past_len, _ = cache_v.shape
    blk0 = row0 // sq
    kern = functools.partial(_attn_a_sample_kernel, n_sub=n_sub, dh=dh, dv=dv,
                             lam_init=lam_init)

    def kern_alias(q_ref, kn_ref, vn_ref, kp_ref, vp_ref, lam_ref, gsub_ref, oa_in_ref, o_ref):
        del oa_in_ref
        kern(q_ref, kn_ref, vn_ref, kp_ref, vp_ref, lam_ref, gsub_ref, o_ref)

    return pl.pallas_call(
        kern_alias,
        grid=(nb,),
        in_specs=[
            pl.BlockSpec((sq, seg), lambda b: (blk0 + b, 0)),
            pl.BlockSpec((sq, seg), lambda b: (blk0 + b, 1)),
            pl.BlockSpec((sq, seg), lambda b: (blk0 + b, 2)),
            pl.BlockSpec((None, None, past_len * n_sub, dh), lambda b: (li, b, 0, 0)),
            pl.BlockSpec((None, None, past_len, seg), lambda b: (li, b, 0, 0)),
            pl.BlockSpec((None, 4, dh), lambda b: (li, 0, 0)),
            pl.BlockSpec((None, 1, dv), lambda b: (li, 0, 0)),
            pl.BlockSpec(memory_space=pl.ANY),
        ],
        out_specs=pl.BlockSpec((sq, seg), lambda b: (blk0 + b, 0)),
        out_shape=jax.ShapeDtypeStruct(oa.shape, BF16),
        input_output_aliases={7: 0},
        compiler_params=_params("arbitrary"),
        name="attn_a_sample",
    )(qkv16, qkv16, qkv16, cache_k, cache_v, lam_vecs, gsub.reshape(gsub.shape[0], 1, dv), oa)


def _attn_b_sample_kernel(q_ref, kn_ref, vn_ref, kp_ref, vp_ref, tri_ref, o_ref,
                          run_scr, er_scr, acc_scr, *, n_heads, dh):
    sq = q_ref.shape[0]
    past_len = kp_ref.shape[0] // n_heads
    tk = tri_ref.shape[0]
    run_scr[...] = jnp.zeros(run_scr.shape, F32)
    er_scr[...] = jnp.ones(er_scr.shape, F32)
    acc_scr[...] = jnp.zeros(acc_scr.shape, F32)
    kn = _pad_rows(kn_ref[...], LANES)
    vn = _pad_rows(vn_ref[...], LANES)
    r = lax.broadcasted_iota(jnp.int32, (sq, LANES), 0)
    c = lax.broadcasted_iota(jnp.int32, (sq, LANES), 1)
    live = _sb_heads(q_ref,
                     lambda h: kn[:, h * dh:(h + 1) * dh],
                     lambda h: vn[:, h * dh:(h + 1) * dh],
                     tri_ref[0:LANES, 0:LANES], c < r, run_scr, er_scr, acc_scr, n_heads, dh)

    def block(j):
        row0 = pl.multiple_of(j * (tk * n_heads), tk * n_heads)

        def head_rows(ref, h):
            return ref[pl.ds(row0 + h, tk, stride=n_heads), :].astype(BF16)

        return _sb_heads(q_ref,
                         lambda h: head_rows(kp_ref, h),
                         lambda h: head_rows(vp_ref, h),
                         tri_ref[...], None, run_scr, er_scr, acc_scr, n_heads, dh)

    _sb_earlier_blocks(past_len // tk - 1, live, block)
    o_ref[...] = acc_scr[...].astype(BF16)


def _attn_b_sample(qkv16, ob, cache_k, cache_v, li, tri, row0, sq, seg, n_heads, dh):
    _, nb, cache_rows, _ = cache_k.shape
    past_len = cache_rows // n_heads
    tk = tri.shape[1]
    assert past_len % tk == 0 and tk >= LANES
    blk0 = row0 // sq
    kern = functools.partial(_attn_b_sample_kernel, n_heads=n_heads, dh=dh)

    def kern_alias(q_ref, kn_ref, vn_ref, kp_ref, vp_ref, tri_ref, ob_in_ref, o_ref, *scr):
        del ob_in_ref
        kern(q_ref, kn_ref, vn_ref, kp_ref, vp_ref, tri_ref, o_ref, *scr)

    return pl.pallas_call(
        kern_alias,
        grid=(nb,),
        in_specs=[
            pl.BlockSpec((sq, seg), lambda b: (blk0 + b, 3)),
            pl.BlockSpec((sq, seg), lambda b: (blk0 + b, 4)),
            pl.BlockSpec((sq, seg), lambda b: (blk0 + b, 5)),
            pl.BlockSpec((None, None, past_len * n_heads, dh), lambda b: (li, b, 0, 0)),
            pl.BlockSpec((None, None, past_len * n_heads, dh), lambda b: (li, b, 0, 0)),
            pl.BlockSpec((tk, tk), lambda b: (0, 0)),
            pl.BlockSpec(memory_space=pl.ANY),
        ],
        out_specs=pl.BlockSpec((sq, seg), lambda b: (blk0 + b, 0)),
        out_shape=jax.ShapeDtypeStruct(ob.shape, BF16),
        scratch_shapes=[
            pltpu.VMEM((n_heads, sq, LANES), F32),
            pltpu.VMEM((n_heads, sq, LANES), F32),
            pltpu.VMEM((sq, seg), F32),
        ],
        input_output_aliases={6: 0},
        compiler_params=_params("arbitrary"),
        name="attn_b_sample",
    )(qkv16, qkv16, qkv16, cache_k, cache_v, tri, ob)


def _merge_kernel(x_ref, oa_ref, ob_ref, ga_ref, gb_ref, wba_ref, wbb_ref, wo_ref, o_ref):
    @pl.when(pl.program_id(1) == 0)
    def _():
        o_ref[...] = x_ref[...]

    ta = jnp.dot(oa_ref[...], wba_ref[...], preferred_element_type=F32)
    tb = jnp.dot(ob_ref[...], wbb_ref[...], preferred_element_type=F32)
    merged = ga_ref[...] * ta + gb_ref[...] * tb
    o_ref[...] += jnp.dot(merged.astype(BF16), wo_ref[...], preferred_element_type=F32)


def _merge(x, oa, ob, proj32, wba, wbb, wo, li, tm, tn, gate_col0):
    m, d = x.shape
    ka = oa.shape[1]
    kb = ob.shape[1]
    nj = d // tn
    ga0 = gate_col0 // tn
    gb0 = ga0 + nj
    return pl.pallas_call(
        _merge_kernel,
        grid=(m // tm, nj),
        in_specs=[
            pl.BlockSpec((tm, d), lambda i, j: (i, 0)),
            pl.BlockSpec((tm, ka), lambda i, j: (i, 0)),
            pl.BlockSpec((tm, kb), lambda i, j: (i, 0)),
            pl.BlockSpec((tm, tn), lambda i, j: (i, ga0 + j)),
            pl.BlockSpec((tm, tn), lambda i, j: (i, gb0 + j)),
            pl.BlockSpec((None, ka, tn), lambda i, j: (li, 0, j)),
            pl.BlockSpec((None, kb, tn), lambda i, j: (li, 0, j)),
            pl.BlockSpec((None, tn, d), lambda i, j: (li, j, 0)),
        ],
        out_specs=pl.BlockSpec((tm, d), lambda i, j: (i, 0)),
        out_shape=jax.ShapeDtypeStruct((m, d), F32),
        compiler_params=_params("parallel", "arbitrary"),
        name="merge",
    )(x, oa, ob, proj32, proj32, wba, wbb, wo)


def _rope_tables(pos, dh):
    inv = 1.0 / (ROPE_THETA ** (jnp.arange(0, dh, 2, dtype=F32) / dh))
    ang = pos.astype(F32)[:, None] * inv[None, :]
    cos = jnp.cos(ang)
    sin = jnp.sin(ang)
    return jnp.concatenate([cos, cos], axis=1), jnp.concatenate([-sin, sin], axis=1)


def kernel(x_prompt, x_sample, cache_a_k, cache_a_v, cache_b_k, cache_b_v,
           ffn1_norm, ffn1_w_gate, ffn1_w_up, ffn1_w_down,
           mix_norm, w_in, a_q_norm, a_k_norm,
           a_lam_q1, a_lam_k1, a_lam_q2, a_lam_k2, a_sub_norm,
           w_branch_a, w_branch_b, w_out,
           ffn2_norm, ffn2_w_gate, ffn2_w_up, ffn2_w_down):
    bp, sp, d = x_prompt.shape
    bs, ss, _ = x_sample.shape
    depth, _, past_len, n_sub, dh_a = cache_a_k.shape
    h_a, dv_a = cache_a_v.shape[3:]
    h_b, dh_b = cache_b_k.shape[3:]
    seg = n_sub * dh_a
    assert bp == 1 and h_a * dv_a == seg and h_b * dh_b == seg and n_sub == 2 * h_a
    assert w_in.shape[2] == 6 * seg + 2 * d and d == 2 * seg

    n_p = bp * sp
    m = n_p + bs * ss
    tm = _pick(m, (768, 512, 384, 256, 128, 64, 32))
    tf = _pick(ffn1_w_gate.shape[2], (512, 256, 128))
    tn = _pick(d, (512, 256, 128))
    tm_p = _pick(n_p, (512, 256, 128, 64, 32))
    assert n_p % (m - n_p) == 0
    tq_a, tk_a = _pick(sp, (256, 128)), _pick(sp, (512, 256, 128))
    tq_b = tk_b = _pick(math.gcd(sp, past_len), (256, 128))
    assert n_p % ss == 0 and min(tq_a, tk_a, tq_b) % CHUNK == 0

    x = jnp.concatenate([x_prompt.reshape(n_p, d), x_sample.reshape(bs * ss, d)], axis=0)
    pos = jnp.concatenate([jnp.arange(sp)] * bp + [past_len + jnp.arange(ss)] * bs)
    cos, sin = _rope_tables(pos, dh_a)
    tri = jnp.asarray(np.tril(np.ones((tk_b, tk_b), np.float32), -1), BF16)
    lam_vecs = jnp.stack([a_lam_q1, a_lam_k1, a_lam_q2, a_lam_k2], axis=1)

    bf = lambda w: w.astype(BF16)
    f1g, f1u, f1d = bf(ffn1_w_gate), bf(ffn1_w_up), bf(ffn1_w_down)
    f2g, f2u, f2d = bf(ffn2_w_gate), bf(ffn2_w_up), bf(ffn2_w_down)
    w_in16, wba16, wbb16, wo16 = bf(w_in), bf(w_branch_a), bf(w_branch_b), bf(w_out)
    rows_of = lambda c: c.reshape(depth, bs, past_len * c.shape[3], c.shape[4])
    ck_a, ck_b, cv_b = rows_of(cache_a_k), rows_of(cache_b_k), rows_of(cache_b_v)
    cv_a = bf(cache_a_v.reshape(depth, bs, past_len, seg))
    scales = (dh_a ** -0.5 * LOG2E, -(dh_b ** -0.5) * LOG2E)

    rows, kv_p, kv_s = [], None, None
    for li in range(depth):
        lam_init = 0.8 - 0.6 * math.exp(-0.3 * li)

        x = _ffn(x, ffn1_norm, f1g, f1u, f1d, li, tm, tf)
        proj32, qkv16, *kv_p = _proj(x, mix_norm, w_in16, cos, sin, a_q_norm, a_k_norm,
                                     None, kv_p, li, depth, 0, n_p, tm_p, seg, n_sub, dh_a,
                                     *scales)
        proj32, qkv16, *kv_s = _proj(x, mix_norm, w_in16, cos, sin, a_q_norm, a_k_norm,
                                     (proj32, qkv16), kv_s, li, depth, n_p, m - n_p, m - n_p,
                                     seg, n_sub, dh_a, *scales)

        oa = jnp.zeros((m, seg), BF16)
        oa = _attn_a_prompt(qkv16, oa, lam_vecs, a_sub_norm, li, n_p, tq_a, tk_a, seg,
                            n_sub, dh_a, dv_a, lam_init)
        oa = _attn_a_sample(qkv16, oa, ck_a, cv_a, li, lam_vecs, a_sub_norm, n_p, ss,
                            seg, n_sub, dh_a, dv_a, lam_init)
        ob = jnp.zeros((m, seg), BF16)
        ob = _attn_b_prompt(qkv16, ob, tri, n_p, tq_b, seg, h_b, dh_b)
        ob = _attn_b_sample(qkv16, ob, ck_b, cv_b, li, tri, n_p, ss, seg, h_b, dh_b)

        x = _merge(x, oa, ob, proj32, wba16, wbb16, wo16, li, tm, tn, seg)
        x = _ffn(x, ffn2_norm, f2g, f2u, f2d, li, tm, tf)
        rows.append(proj32)

    av_p = jnp.stack([r[:n_p, :seg] for r in rows], axis=0).reshape(depth, bp, sp, h_a, dv_a)
    av_s = jnp.stack([r[n_p:, :seg] for r in rows], axis=0).reshape(depth, bs, ss, h_a, dv_a)
    (ak_p, bk_p, bv_p), (ak_s, bk_s, bv_s) = kv_p, kv_s
    prompt = lambda a, nh, dh: a.reshape(depth, bp, sp, nh, dh)
    sample = lambda a, nh, dh: a.reshape(depth, bs, ss, nh, dh)
    ak_p, ak_s = prompt(ak_p, n_sub, dh_a), sample(ak_s, n_sub, dh_a)
    bk_p, bk_s = prompt(bk_p, h_b, dh_b), sample(bk_s, h_b, dh_b)
    bv_p, bv_s = prompt(bv_p, h_b, dh_b), sample(bv_s, h_b, dh_b)
    return (x[:n_p].reshape(bp, sp, d), x[n_p:].reshape(bs, ss, d),
            ak_p, av_p, bk_p, bv_p, ak_s, av_s, bk_s, bv_s)
```

```python
import functools
import math

import numpy as np
import jax
import jax.numpy as jnp
from jax import lax
from jax.experimental import pallas as pl
from jax.experimental.pallas import tpu as pltpu

F32 = jnp.float32
BF16 = jnp.bfloat16

CHUNK = 64
ROPE_THETA = 10000.0
EPS = 1e-6
NEG_INF = -1e30
LOG2E = math.log2(math.e)
LANES = 128
VMEM_LIMIT = 56 * 1024 * 1024

NT_DIMS = (((1,), (1,)), ((), ()))


def _pick(n, candidates):
    for c in candidates:
        if n % c == 0:
            return c
    return n


def _params(*sem):
    return pltpu.CompilerParams(dimension_semantics=sem, vmem_limit_bytes=VMEM_LIMIT)


def _rmsnorm_rows(x_ref, g_ref, h_ref, copy_ref=None):
    rows = _pick(x_ref.shape[0], (64, 32, 16))
    g = g_ref[...]

    def chunk(c, carry):
        r = pl.ds(pl.multiple_of(c * rows, rows), rows)
        x = x_ref[r, :]
        ms = jnp.mean(x * x, axis=-1, keepdims=True)
        h_ref[r, :] = (x * lax.rsqrt(ms + EPS) * g).astype(BF16)
        if copy_ref is not None:
            copy_ref[r, :] = x
        return carry

    lax.fori_loop(0, x_ref.shape[0] // rows, chunk, 0)


def _ffn_kernel(x_ref, g_ref, wg_ref, wu_ref, wd_ref, o_ref, h_ref):
    @pl.when(pl.program_id(1) == 0)
    def _():
        _rmsnorm_rows(x_ref, g_ref, h_ref, copy_ref=o_ref)

    h = h_ref[...]
    g = jnp.dot(h, wg_ref[...].astype(BF16), preferred_element_type=F32)
    u = jnp.dot(h, wu_ref[...].astype(BF16), preferred_element_type=F32)
    a = (0.5 * g) * jax.nn.sigmoid(g) * u
    o_ref[...] += jnp.dot(a.astype(BF16), wd_ref[...].astype(BF16),
                          preferred_element_type=F32)


def _ffn(x, gain, wg, wu, wd, li, tm, tf):
    m, d = x.shape
    f = wg.shape[2]
    once = pl.Buffered(1)
    return pl.pallas_call(
        _ffn_kernel,
        grid=(m // tm, f // tf),
        in_specs=[
            pl.BlockSpec((tm, d), lambda i, j: (i, 0), pipeline_mode=once),
            pl.BlockSpec((None, 1, d), lambda i, j: (li, 0, 0)),
            pl.BlockSpec((None, d, tf), lambda i, j: (li, 0, j)),
            pl.BlockSpec((None, d, tf), lambda i, j: (li, 0, j)),
            pl.BlockSpec((None, tf, d), lambda i, j: (li, j, 0)),
        ],
        out_specs=pl.BlockSpec((tm, d), lambda i, j: (i, 0), pipeline_mode=once),
        out_shape=jax.ShapeDtypeStruct((m, d), F32),
        scratch_shapes=[pltpu.VMEM((tm, d), BF16)],
        compiler_params=_params("parallel", "arbitrary"),
        name="ffn",
    )(x, gain.reshape(gain.shape[0], 1, d), wg, wu, wd)


def _store_heads(y, kv_ref, h0, n_heads, dh):
    for hh in range(y.shape[1] // dh):
        kv_ref[pl.ds(h0 + hh, y.shape[0], stride=n_heads), :] = y[:, hh * dh:(hh + 1) * dh]


def _proj_kernel(x_ref, g_ref, w_ref, cos_ref, sin_ref, qn_ref, kn_ref, *refs,
                 n_heads, dh, scale_a, scale_b):
    o32_ref, o16_ref, ka_ref, kb_ref, vb_ref, h_ref = refs[-6:]
    j = pl.program_id(1)
    seg = w_ref.shape[1]
    cw = 2 * dh

    @pl.when(j == 0)
    def _():
        _rmsnorm_rows(x_ref, g_ref, h_ref)

    def sweep(epilogue):
        accs = {}

        def matmul(c):
            accs[c] = jnp.dot(h_ref[...], w_ref[:, c * cw:(c + 1) * cw],
                              preferred_element_type=F32)

        def finish(c):
            epilogue(c, slice(c * cw, (c + 1) * cw), accs.pop(c))

        _skewed((matmul, finish), seg // cw)

    def norm_rope(acc, gain_ref, scale):
        cos = cos_ref[...]
        sin = sin_ref[...]
        outs = []
        for hh in range(acc.shape[1] // dh):
            xh = acc[:, hh * dh:(hh + 1) * dh]
            ms = jnp.mean(xh * xh, axis=-1, keepdims=True)
            y = xh * lax.rsqrt(ms + EPS) * gain_ref[...]
            y = y * cos + pltpu.roll(y, dh // 2, 1) * sin
            if scale is not None:
                y = y * scale
            outs.append(y)
        return jnp.concatenate(outs, axis=1)

    def qa(c, cols, acc):
        o16_ref[:, cols] = norm_rope(acc, qn_ref, scale_a).astype(BF16)

    def ka(c, cols, acc):
        y = norm_rope(acc, kn_ref, None)
        o16_ref[:, cols] = y.astype(BF16)
        _store_heads(y, ka_ref, c * (cw // dh), n_heads, dh)

    def va(c, cols, acc):
        o32_ref[:, cols] = acc
        o16_ref[:, cols] = acc.astype(BF16)

    def qb(c, cols, acc):
        o16_ref[:, cols] = (acc * scale_b).astype(BF16)

    def kv_b(kv_ref):
        def epilogue(c, cols, acc):
            o16_ref[:, cols] = acc.astype(BF16)
            _store_heads(acc, kv_ref, c * (cw // dh), n_heads, dh)
        return epilogue

    def gate(c, cols, acc):
        o32_ref[:, cols] = jax.nn.sigmoid(acc)

    for jj, epilogue in enumerate((qa, ka, va, qb, kv_b(kb_ref), kv_b(vb_ref))):
        pl.when(j == jj)(functools.partial(sweep, epilogue))
    pl.when(j >= 6)(functools.partial(sweep, gate))


def _proj(x, gain, w_in, cos, sin, qn, kn, shared, kv, li, depth, row0, n_rows, tm,
          seg, n_heads, dh, scale_a, scale_b):
    m, d = x.shape
    nj = w_in.shape[2] // seg
    assert nj == 10 and n_heads * dh == seg and n_rows % tm == 0 and row0 % tm == 0
    blk0 = row0 // tm

    def o32_map(i, j):
        return (blk0 + i, jnp.where(j >= 6, j - 5, 0))

    def o16_map(i, j):
        return (blk0 + i, jnp.minimum(j, 5))

    kern = functools.partial(_proj_kernel, n_heads=n_heads, dh=dh,
                             scale_a=scale_a, scale_b=scale_b)
    once = pl.Buffered(1)
    kv_spec = pl.BlockSpec((None, tm * n_heads, dh), lambda i, j: (li, i, 0), pipeline_mode=once)
    kv_shape = jax.ShapeDtypeStruct((depth, n_rows * n_heads, dh), F32)
    in_specs = [
        pl.BlockSpec((tm, d), lambda i, j: (blk0 + i, 0), pipeline_mode=once),
        pl.BlockSpec((None, 1, d), lambda i, j: (li, 0, 0)),
        pl.BlockSpec((None, d, seg), lambda i, j: (li, 0, j)),
        pl.BlockSpec((tm, dh), lambda i, j: (blk0 + i, 0)),
        pl.BlockSpec((tm, dh), lambda i, j: (blk0 + i, 0)),
        pl.BlockSpec((None, 1, dh), lambda i, j: (li, 0, 0)),
        pl.BlockSpec((None, 1, dh), lambda i, j: (li, 0, 0)),
    ]
    args = [x, gain.reshape(gain.shape[0], 1, d), w_in, cos, sin,
            qn.reshape(qn.shape[0], 1, dh), kn.reshape(kn.shape[0], 1, dh)]
    aliases = {}
    for out0, group in ((0, shared), (2, kv)):
        for k, a in enumerate(group or ()):
            in_specs.append(pl.BlockSpec(memory_space=pl.ANY))
            aliases[len(args)] = out0 + k
            args.append(a)
    return pl.pallas_call(
        kern,
        grid=(n_rows // tm, nj),
        in_specs=in_specs,
        out_specs=[
            pl.BlockSpec((tm, seg), o32_map),
            pl.BlockSpec((tm, seg), o16_map),
            kv_spec, kv_spec, kv_spec,
        ],
        out_shape=[
            jax.ShapeDtypeStruct((m, 5 * seg), F32),
            jax.ShapeDtypeStruct((m, 6 * seg), BF16),
            kv_shape, kv_shape, kv_shape,
        ],
        scratch_shapes=[pltpu.VMEM((tm, d), BF16)],
        input_output_aliases=aliases,
        compiler_params=_params("arbitrary", "arbitrary"),
        name="proj",
    )(*args)


def _lam(lam_ref, lam_init):
    v = lam_ref[...]
    s1 = jnp.sum(v[0:1] * v[1:2], axis=1, keepdims=True)
    s2 = jnp.sum(v[2:3] * v[3:4], axis=1, keepdims=True)
    return jnp.exp(s1) - jnp.exp(s2) + lam_init


def _diff_combine(o0, o1, lam, gsub, lam_init):
    o = o0 - lam * o1
    ms = jnp.mean(o * o, axis=-1, keepdims=True)
    return (o * lax.rsqrt(ms + EPS) * gsub) * (1.0 - lam_init)


def _wide(col, n):
    reps = n // LANES
    return col if reps == 1 else jnp.concatenate([col] * reps, axis=1)


def _sb_logs(zn, mask):
    l = jnp.log(1.0 + jnp.exp2(-jnp.abs(zn))) * LOG2E
    lk = jnp.minimum(zn, 0.0) - l
    lsig = lk - zn
    if mask is not None:
        lk = jnp.where(mask, lk, 0.0)
    return lsig, lk


def _skewed(stages, n):
    for t in range(n + len(stages) - 1):
        for s in reversed(range(len(stages))):
            if 0 <= t - s < n:
                stages[s](t - s)


FIRST, LAST, MASKED = 1, 2, 4


def _causal_steps(n_rows, tq, tk, descending):
    qi, kj, fl = [], [], []
    for i in range(n_rows // tq):
        jmax = ((i + 1) * tq - 1) // tk
        ks = list(range(jmax, -1, -1) if descending else range(jmax + 1))
        for n, j in enumerate(ks):
            flags = (FIRST if n == 0 else 0) | (LAST if n == len(ks) - 1 else 0)
            if (j + 1) * tk > i * tq:
                flags |= MASKED
            qi.append(i)
            kj.append(j)
            fl.append(flags)
    return tuple(jnp.asarray(np.array(a, np.int32)) for a in (qi, kj, fl))


def _attn_a_kernel(qi_ref, kj_ref, fl_ref, q_ref, k_ref, v_ref, lam_ref, gsub_ref, o_ref,
                   m_scr, l_scr, acc_scr, *, n_sub, dh, dv, lam_init):
    step = pl.program_id(0)
    flags = fl_ref[step]
    tq = q_ref.shape[0]
    tk = k_ref.shape[0]

    @pl.when((flags & FIRST) != 0)
    def _():
        m_scr[...] = jnp.full(m_scr.shape, NEG_INF, F32)
        l_scr[...] = jnp.zeros(l_scr.shape, F32)
        acc_scr[...] = jnp.zeros(acc_scr.shape, F32)

    def body(masked):
        if masked:
            off = (qi_ref[step] * tq - kj_ref[step] * tk) // CHUNK
            r = lax.broadcasted_iota(jnp.int32, (tq, tk), 0)
            c = lax.broadcasted_iota(jnp.int32, (tq, tk), 1)
            mask = (c // CHUNK - r // CHUNK) <= off
        sc, pb, al = {}, {}, {}

        def scores(hd):
            for h in (2 * hd, 2 * hd + 1):
                q = q_ref[:, h * dh:(h + 1) * dh]
                k = k_ref[:, h * dh:(h + 1) * dh]
                s = lax.dot_general(q, k, NT_DIMS, preferred_element_type=F32)
                if masked:
                    s = jnp.where(mask, s, NEG_INF)
                sc[h] = s

        def softmax_update(hd):
            for h in (2 * hd, 2 * hd + 1):
                s = sc.pop(h)
                m_prev = m_scr[h]
                m_new = jnp.maximum(m_prev, jnp.max(s, axis=1, keepdims=True))
                alpha = jnp.exp2(m_prev - m_new)
                p = jnp.exp2(s - _wide(m_new, tk))
                l_scr[h] = alpha * l_scr[h] + jnp.sum(p, axis=1, keepdims=True)
                m_scr[h] = m_new
                al[h] = alpha
                pb[h] = p.astype(BF16)

        def values(hd):
            v = v_ref[:, hd * dv:(hd + 1) * dv]
            p2 = jnp.concatenate([pb.pop(2 * hd), pb.pop(2 * hd + 1)], axis=0)
            pv = jnp.dot(p2, v, preferred_element_type=F32)
            for u in range(2):
                h = 2 * hd + u
                acc_scr[h] = _wide(al.pop(h), dv) * acc_scr[h] + pv[u * tq:(u + 1) * tq]

        _skewed((scores, softmax_update, values), n_sub // 2)

    @pl.when((flags & MASKED) == 0)
    def _():
        body(False)

    @pl.when((flags & MASKED) != 0)
    def _():
        body(True)

    @pl.when((flags & LAST) != 0)
    def _():
        lam = _lam(lam_ref, lam_init)
        gsub = gsub_ref[...]
        for hd in range(n_sub // 2):
            o0 = acc_scr[2 * hd] / _wide(l_scr[2 * hd], dv)
            o1 = acc_scr[2 * hd + 1] / _wide(l_scr[2 * hd + 1], dv)
            o = _diff_combine(o0, o1, lam, gsub, lam_init)
            o_ref[:, hd * dv:(hd + 1) * dv] = o.astype(BF16)


def _attn_a_prompt(qkv16, oa, lam_vecs, gsub, li, n_rows, tq, tk, seg, n_sub, dh, dv, lam_init):
    qi, kj, fl = _causal_steps(n_rows, tq, tk, descending=False)
    kern = functools.partial(_attn_a_kernel, n_sub=n_sub, dh=dh, dv=dv, lam_init=lam_init)
    grid_spec = pltpu.PrefetchScalarGridSpec(
        num_scalar_prefetch=3,
        grid=(int(qi.shape[0]),),
        in_specs=[
            pl.BlockSpec((tq, seg), lambda s, qi, kj, fl: (qi[s], 0)),
            pl.BlockSpec((tk, seg), lambda s, qi, kj, fl: (kj[s], 1)),
            pl.BlockSpec((tk, seg), lambda s, qi, kj, fl: (kj[s], 2)),
            pl.BlockSpec((None, 4, dh), lambda s, qi, kj, fl: (li, 0, 0)),
            pl.BlockSpec((None, 1, dv), lambda s, qi, kj, fl: (li, 0, 0)),
            pl.BlockSpec(memory_space=pl.ANY),
        ],
        out_specs=pl.BlockSpec((tq, seg), lambda s, qi, kj, fl: (qi[s], 0)),
        scratch_shapes=[
            pltpu.VMEM((n_sub, tq, LANES), F32),
            pltpu.VMEM((n_sub, tq, LANES), F32),
            pltpu.VMEM((n_sub, tq, dv), F32),
        ],
    )

    def kern_alias(qi_ref, kj_ref, fl_ref, q_ref, k_ref, v_ref, lam_ref, gsub_ref, oa_in_ref,
                   o_ref, *scr):
        del oa_in_ref
        kern(qi_ref, kj_ref, fl_ref, q_ref, k_ref, v_ref, lam_ref, gsub_ref, o_ref, *scr)

    return pl.pallas_call(
        kern_alias,
        grid_spec=grid_spec,
        out_shape=jax.ShapeDtypeStruct(oa.shape, BF16),
        input_output_aliases={8: 0},
        compiler_params=_params("arbitrary"),
        name="attn_a_prompt",
    )(qi, kj, fl, qkv16, qkv16, qkv16, lam_vecs, gsub.reshape(gsub.shape[0], 1, dv), oa)


def _sb_heads(q_ref, k_of, v_of, tri, mask, run_scr, er_scr, acc_scr, n_heads, dh):
    zn, lsig, lkb, er, after = {}, {}, {}, {}, {}

    def logits(h):
        q = q_ref[:, h * dh:(h + 1) * dh]
        zn[h] = lax.dot_general(q, k_of(h), NT_DIMS, preferred_element_type=F32)

    def logs(h):
        lsig[h], lk = _sb_logs(zn.pop(h), mask)
        lkb[h] = lk.astype(BF16)
        run = run_scr[h] + jnp.sum(lk, axis=1, keepdims=True)
        run_scr[h] = run
        er[h] = er_scr[h]
        er_scr[h] = jnp.exp2(run)

    def cumsum(h):
        after[h] = jnp.dot(lkb.pop(h), tri, preferred_element_type=F32)

    def weights_pv(h):
        a = jnp.exp2(lsig.pop(h) + after.pop(h))
        if mask is not None:
            a = jnp.where(mask, a, 0.0)
        pv = jnp.dot(a.astype(BF16), v_of(h), preferred_element_type=F32)
        acc_scr[:, h * dh:(h + 1) * dh] += pv * _wide(er.pop(h), dh)

    _skewed((logits, logs, cumsum, weights_pv), n_heads)
    top = er_scr[0]
    for h in range(1, n_heads):
        top = jnp.maximum(top, er_scr[h])
    return jnp.max(top) > 0.0


def _sb_earlier_blocks(first_block, live, block_fn):
    def cond(carry):
        j, alive = carry
        return (j >= 0) & (alive != 0)

    def step(carry):
        j, _ = carry
        return j - 1, block_fn(j).astype(jnp.int32)

    lax.while_loop(cond, step, (first_block, live.astype(jnp.int32)))


def _attn_b_kernel(q_ref, k_ref, v_ref, tri_ref, o_ref, run_scr, er_scr, acc_scr,
                   *, n_heads, dh):
    qi = pl.program_id(0)
    tq = q_ref.shape[0]
    tk = tri_ref.shape[0]
    tri = tri_ref[...]
    run_scr[...] = jnp.zeros(run_scr.shape, F32)
    er_scr[...] = jnp.ones(er_scr.shape, F32)
    acc_scr[...] = jnp.zeros(acc_scr.shape, F32)

    def block(j, mask):
        rows = pl.ds(pl.multiple_of(j * tk, tk), tk)
        return _sb_heads(q_ref,
                         lambda h: k_ref[rows, h * dh:(h + 1) * dh],
                         lambda h: v_ref[rows, h * dh:(h + 1) * dh],
                         tri, mask, run_scr, er_scr, acc_scr, n_heads, dh)

    r = lax.broadcasted_iota(jnp.int32, (tq, tk), 0)
    c = lax.broadcasted_iota(jnp.int32, (tq, tk), 1)
    live = block(qi, c < r)
    _sb_earlier_blocks(qi - 1, live, lambda j: block(j, None))
    o_ref[...] = acc_scr[...].astype(BF16)


def _attn_b_prompt(qkv16, ob, tri, n_rows, tq, seg, n_heads, dh):
    assert tri.shape[0] == tq
    kern = functools.partial(_attn_b_kernel, n_heads=n_heads, dh=dh)
    resident = pl.Buffered(1)

    def kern_alias(q_ref, k_ref, v_ref, tri_ref, ob_in_ref, o_ref, *scr):
        del ob_in_ref
        kern(q_ref, k_ref, v_ref, tri_ref, o_ref, *scr)

    return pl.pallas_call(
        kern_alias,
        grid=(n_rows // tq,),
        in_specs=[
            pl.BlockSpec((tq, seg), lambda i: (i, 3)),
            pl.BlockSpec((n_rows, seg), lambda i: (0, 4), pipeline_mode=resident),
            pl.BlockSpec((n_rows, seg), lambda i: (0, 5), pipeline_mode=resident),
            pl.BlockSpec((tq, tq), lambda i: (0, 0), pipeline_mode=resident),
            pl.BlockSpec(memory_space=pl.ANY),
        ],
        out_specs=pl.BlockSpec((tq, seg), lambda i: (i, 0)),
        out_shape=jax.ShapeDtypeStruct(ob.shape, BF16),
        scratch_shapes=[
            pltpu.VMEM((n_heads, tq, LANES), F32),
            pltpu.VMEM((n_heads, tq, LANES), F32),
            pltpu.VMEM((tq, seg), F32),
        ],
        input_output_aliases={4: 0},
        compiler_params=_params("arbitrary"),
        name="attn_b_prompt",
    )(qkv16, qkv16, qkv16, tri, ob)


def _pad_rows(x, rows):
    n = x.shape[0]
    if n == rows:
        return x
    return jnp.concatenate([x, jnp.zeros((rows - n, x.shape[1]), x.dtype)], axis=0)


def _attn_a_sample_kernel(q_ref, kn_ref, vn_ref, kp_ref, vp_ref, lam_ref, gsub_ref, o_ref,
                          *, n_sub, dh, dv, lam_init):
    sq = q_ref.shape[0]
    past_len = kp_ref.shape[0] // n_sub
    kn = _pad_rows(kn_ref[...], LANES)
    vn = _pad_rows(vn_ref[...], LANES)
    r = lax.broadcasted_iota(jnp.int32, (sq, LANES), 0)
    c = lax.broadcasted_iota(jnp.int32, (sq, LANES), 1)
    mask_new = ((past_len + c) // CHUNK <= (past_len + r) // CHUNK) & (c < sq)
    lam = _lam(lam_ref, lam_init)
    gsub = gsub_ref[...]
    outs = []
    for h in range(n_sub):
        hs = slice(h * dh, (h + 1) * dh)
        vs = slice((h // 2) * dv, (h // 2 + 1) * dv)
        q = q_ref[:, hs]
        kp = kp_ref[pl.ds(h, past_len, stride=n_sub), :].astype(BF16)
        sp = lax.dot_general(q, kp, NT_DIMS, preferred_element_type=F32)
        sn = lax.dot_general(q, kn[:, hs], NT_DIMS, preferred_element_type=F32)
        sn = jnp.where(mask_new, sn, NEG_INF)
        m = jnp.maximum(jnp.max(sp, axis=1, keepdims=True), jnp.max(sn, axis=1, keepdims=True))
        pp = jnp.exp2(sp - m)
        pn = jnp.exp2(sn - m)
        l = jnp.sum(pp, axis=1, keepdims=True) + jnp.sum(pn, axis=1, keepdims=True)
        o = (jnp.dot(pp.astype(BF16), vp_ref[:, vs], preferred_element_type=F32)
             + jnp.dot(pn.astype(BF16), vn[:, vs], preferred_element_type=F32))
        outs.append(o / l)
    for hd in range(n_sub // 2):
        o = _diff_combine(outs[2 * hd], outs[2 * hd + 1], lam, gsub, lam_init)
        o_ref[:, hd * dv:(hd + 1) * dv] = o.astype(BF16)


def _attn_a_sample(qkv16, oa, cache_k, cache_v, li, lam_vecs, gsub, row0, sq, seg,
                   n_sub, dh, dv, lam_init):
    _, nb, past_len, _ = cache_v.shape
    blk0 = row0 // sq
    kern = functools.partial(_attn_a_sample_kernel, n_sub=n_sub, dh=dh, dv=dv,
                             lam_init=lam_init)

    def kern_alias(q_ref, kn_ref, vn_ref, kp_ref, vp_ref, lam_ref, gsub_ref, oa_in_ref, o_ref):
        del oa_in_ref
        kern(q_ref, kn_ref, vn_ref, kp_ref, vp_ref, lam_ref, gsub_ref, o_ref)

    return pl.pallas_call(
        kern_alias,
        grid=(nb,),
        in_specs=[
            pl.BlockSpec((sq, seg), lambda b: (blk0 + b, 0)),
            pl.BlockSpec((sq, seg), lambda b: (blk0 + b, 1)),
            pl.BlockSpec((sq, seg), lambda b: (blk0 + b, 2)),
            pl.BlockSpec((None, None, past_len * n_sub, dh), lambda b: (li, b, 0, 0)),
            pl.BlockSpec((None, None, past_len, seg), lambda b: (li, b, 0, 0)),
            pl.BlockSpec((None, 4, dh), lambda b: (li, 0, 0)),
            pl.BlockSpec((None, 1, dv), lambda b: (li, 0, 0)),
            pl.BlockSpec(memory_space=pl.ANY),
        ],
        out_specs=pl.BlockSpec((sq, seg), lambda b: (blk0 + b, 0)),
        out_shape=jax.ShapeDtypeStruct(oa.shape, BF16),
        input_output_aliases={7: 0},
        compiler_params=_params("arbitrary"),
        name="attn_a_sample",
    )(qkv16, qkv16, qkv16, cache_k, cache_v, lam_vecs, gsub.reshape(gsub.shape[0], 1, dv), oa)


def _attn_b_sample_kernel(q_ref, kn_ref, vn_ref, kp_ref, vp_ref, tri_ref, o_ref,
                          run_scr, er_scr, acc_scr, *, n_heads, dh):
    sq = q_ref.shape[0]
    past_len = kp_ref.shape[0] // n_heads
    tk = tri_ref.shape[0]
    run_scr[...] = jnp.zeros(run_scr.shape, F32)
    er_scr[...] = jnp.ones(er_scr.shape, F32)
    acc_scr[...] = jnp.zeros(acc_scr.shape, F32)
    kn = _pad_rows(kn_ref[...], LANES)
    vn = _pad_rows(vn_ref[...], LANES)
    r = lax.broadcasted_iota(jnp.int32, (sq, LANES), 0)
    c = lax.broadcasted_iota(jnp.int32, (sq, LANES), 1)
    live = _sb_heads(q_ref,
                     lambda h: kn[:, h * dh:(h + 1) * dh],
                     lambda h: vn[:, h * dh:(h + 1) * dh],
                     tri_ref[0:LANES, 0:LANES], c < r, run_scr, er_scr, acc_scr, n_heads, dh)

    def block(j):
        row0 = pl.multiple_of(j * (tk * n_heads), tk * n_heads)

        def head_rows(ref, h):
            return ref[pl.ds(row0 + h, tk, stride=n_heads), :].astype(BF16)

        return _sb_heads(q_ref,
                         lambda h: head_rows(kp_ref, h),
                         lambda h: head_rows(vp_ref, h),
                         tri_ref[...], None, run_scr, er_scr, acc_scr, n_heads, dh)

    _sb_earlier_blocks(past_len // tk - 1, live, block)
    o_ref[...] = acc_scr[...].astype(BF16)


def _attn_b_sample(qkv16, ob, cache_k, cache_v, li, tri, row0, sq, seg, n_heads, dh):
    _, nb, cache_rows, _ = cache_k.shape
    past_len = cache_rows // n_heads
    tk = tri.shape[1]
    assert past_len % tk == 0 and tk >= LANES
    blk0 = row0 // sq
    kern = functools.partial(_attn_b_sample_kernel, n_heads=n_heads, dh=dh)

    def kern_alias(q_ref, kn_ref, vn_ref, kp_ref, vp_ref, tri_ref, ob_in_ref, o_ref, *scr):
        del ob_in_ref
        kern(q_ref, kn_ref, vn_ref, kp_ref, vp_ref, tri_ref, o_ref, *scr)

    return pl.pallas_call(
        kern_alias,
        grid=(nb,),
        in_specs=[
            pl.BlockSpec((sq, seg), lambda b: (blk0 + b, 3)),
            pl.BlockSpec((sq, seg), lambda b: (blk0 + b, 4)),
            pl.BlockSpec((sq, seg), lambda b: (blk0 + b, 5)),
            pl.BlockSpec((None, None, past_len * n_heads, dh), lambda b: (li, b, 0, 0)),
            pl.BlockSpec((None, None, past_len * n_heads, dh), lambda b: (li, b, 0, 0)),
            pl.BlockSpec((tk, tk), lambda b: (0, 0)),
            pl.BlockSpec(memory_space=pl.ANY),
        ],
        out_specs=pl.BlockSpec((sq, seg), lambda b: (blk0 + b, 0)),
        out_shape=jax.ShapeDtypeStruct(ob.shape, BF16),
        scratch_shapes=[
            pltpu.VMEM((n_heads, sq, LANES), F32),
            pltpu.VMEM((n_heads, sq, LANES), F32),
            pltpu.VMEM((sq, seg), F32),
        ],
        input_output_aliases={6: 0},
        compiler_params=_params("arbitrary"),
        name="attn_b_sample",
    )(qkv16, qkv16, qkv16, cache_k, cache_v, tri, ob)


def _merge_kernel(x_ref, oa_ref, ob_ref, ga_ref, gb_ref, wba_ref, wbb_ref, wo_ref, o_ref):
    @pl.when(pl.program_id(1) == 0)
    def _():
        o_ref[...] = x_ref[...]

    ta = jnp.dot(oa_ref[...], wba_ref[...], preferred_element_type=F32)
    tb = jnp.dot(ob_ref[...], wbb_ref[...], preferred_element_type=F32)
    merged = ga_ref[...] * ta + gb_ref[...] * tb
    o_ref[...] += jnp.dot(merged.astype(BF16), wo_ref[...], preferred_element_type=F32)


def _merge(x, oa, ob, proj32, wba, wbb, wo, li, tm, tn, gate_col0):
    m, d = x.shape
    ka = oa.shape[1]
    kb = ob.shape[1]
    nj = d // tn
    ga0 = gate_col0 // tn
    gb0 = ga0 + nj
    return pl.pallas_call(
        _merge_kernel,
        grid=(m // tm, nj),
        in_specs=[
            pl.BlockSpec((tm, d), lambda i, j: (i, 0)),
            pl.BlockSpec((tm, ka), lambda i, j: (i, 0)),
            pl.BlockSpec((tm, kb), lambda i, j: (i, 0)),
            pl.BlockSpec((tm, tn), lambda i, j: (i, ga0 + j)),
            pl.BlockSpec((tm, tn), lambda i, j: (i, gb0 + j)),
            pl.BlockSpec((None, ka, tn), lambda i, j: (li, 0, j)),
            pl.BlockSpec((None, kb, tn), lambda i, j: (li, 0, j)),
            pl.BlockSpec((None, tn, d), lambda i, j: (li, j, 0)),
        ],
        out_specs=pl.BlockSpec((tm, d), lambda i, j: (i, 0)),
        out_shape=jax.ShapeDtypeStruct((m, d), F32),
        compiler_params=_params("parallel", "arbitrary"),
        name="merge",
    )(x, oa, ob, proj32, proj32, wba, wbb, wo)


def _rope_tables(pos, dh):
    inv = 1.0 / (ROPE_THETA ** (jnp.arange(0, dh, 2, dtype=F32) / dh))
    ang = pos.astype(F32)[:, None] * inv[None, :]
    cos = jnp.cos(ang)
    sin = jnp.sin(ang)
    return jnp.concatenate([cos, cos], axis=1), jnp.concatenate([-sin, sin], axis=1)


def kernel(x_prompt, x_sample, cache_a_k, cache_a_v, cache_b_k, cache_b_v,
           ffn1_norm, ffn1_w_gate, ffn1_w_up, ffn1_w_down,
           mix_norm, w_in, a_q_norm, a_k_norm,
           a_lam_q1, a_lam_k1, a_lam_q2, a_lam_k2, a_sub_norm,
           w_branch_a, w_branch_b, w_out,
           ffn2_norm, ffn2_w_gate, ffn2_w_up, ffn2_w_down):
    bp, sp, d = x_prompt.shape
    bs, ss, _ = x_sample.shape
    depth, _, past_len, n_sub, dh_a = cache_a_k.shape
    h_a, dv_a = cache_a_v.shape[3:]
    h_b, dh_b = cache_b_k.shape[3:]
    seg = n_sub * dh_a
    assert bp == 1 and h_a * dv_a == seg and h_b * dh_b == seg and n_sub == 2 * h_a
    assert w_in.shape[2] == 6 * seg + 2 * d and d == 2 * seg

    n_p = bp * sp
    m = n_p + bs * ss
    tm = _pick(m, (768, 512, 384, 256, 128, 64, 32))
    tf = _pick(ffn1_w_gate.shape[2], (512, 256, 128))
    tn = _pick(d, (512, 256, 128))
    tm_p = _pick(n_p, (512, 256, 128, 64, 32))
    assert n_p % (m - n_p) == 0
    tq_a, tk_a = _pick(sp, (256, 128)), _pick(sp, (512, 256, 128))
    tq_b = tk_b = _pick(math.gcd(sp, past_len), (256, 128))
    assert n_p % ss == 0 and min(tq_a, tk_a, tq_b) % CHUNK == 0

    x = jnp.concatenate([x_prompt.reshape(n_p, d), x_sample.reshape(bs * ss, d)], axis=0)
    pos = jnp.concatenate([jnp.arange(sp)] * bp + [past_len + jnp.arange(ss)] * bs)
    cos, sin = _rope_tables(pos, dh_a)
    tri = jnp.asarray(np.tril(np.ones((tk_b, tk_b), np.float32), -1), BF16)
    lam_vecs = jnp.stack([a_lam_q1, a_lam_k1, a_lam_q2, a_lam_k2], axis=1)

    bf = lambda w: w.astype(BF16)
    f1g, f1u, f1d = ffn1_w_gate, ffn1_w_up, ffn1_w_down
    f2g, f2u, f2d = ffn2_w_gate, ffn2_w_up, ffn2_w_down
    w_in16, wba16, wbb16, wo16 = bf(w_in), bf(w_branch_a), bf(w_branch_b), bf(w_out)
    rows_of = lambda c: c.reshape(depth, bs, past_len * c.shape[3], c.shape[4])
    ck_a, ck_b, cv_b = rows_of(cache_a_k), rows_of(cache_b_k), rows_of(cache_b_v)
    cv_a = bf(cache_a_v.reshape(depth, bs, past_len, seg))
    scales = (dh_a ** -0.5 * LOG2E, -(dh_b ** -0.5) * LOG2E)

    rows, kv_p, kv_s = [], None, None
    for li in range(depth):
        lam_init = 0.8 - 0.6 * math.exp(-0.3 * li)

        x = _ffn(x, ffn1_norm, f1g, f1u, f1d, li, tm, tf)
        proj32, qkv16, *kv_p = _proj(x, mix_norm, w_in16, cos, sin, a_q_norm, a_k_norm,
                                     None, kv_p, li, depth, 0, n_p, tm_p, seg, n_sub, dh_a,
                                     *scales)
        proj32, qkv16, *kv_s = _proj(x, mix_norm, w_in16, cos, sin, a_q_norm, a_k_norm,
                                     (proj32, qkv16), kv_s, li, depth, n_p, m - n_p, m - n_p,
                                     seg, n_sub, dh_a, *scales)

        oa = jnp.zeros((m, seg), BF16)
        oa = _attn_a_prompt(qkv16, oa, lam_vecs, a_sub_norm, li, n_p, tq_a, tk_a, seg,
                            n_sub, dh_a, dv_a, lam_init)
        oa = _attn_a_sample(qkv16, oa, ck_a, cv_a, li, lam_vecs, a_sub_norm, n_p, ss,
                            seg, n_sub, dh_a, dv_a, lam_init)
        ob = jnp.zeros((m, seg), BF16)
        ob = _attn_b_prompt(qkv16, ob, tri, n_p, tq_b, seg, h_b, dh_b)
        ob = _attn_b_sample(qkv16, ob, ck_b, cv_b, li, tri, n_p, ss, seg, h_b, dh_b)

        x = _merge(x, oa, ob, proj32, wba16, wbb16, wo16, li, tm, tn, seg)
        x = _ffn(x, ffn2_norm, f2g, f2u, f2d, li, tm, tf)
        rows.append(proj32)

    av_p = jnp.stack([r[:n_p, :seg] for r in rows], axis=0).reshape(depth, bp, sp, h_a, dv_a)
    av_s = jnp.stack([r[n_p:, :seg] for r in rows], axis=0).reshape(depth, bs, ss, h_a, dv_a)
    (ak_p, bk_p, bv_p), (ak_s, bk_s, bv_s) = kv_p, kv_s
    prompt = lambda a, nh, dh: a.reshape(depth, bp, sp, nh, dh)
    sample = lambda a, nh, dh: a.reshape(depth, bs, ss, nh, dh)
    ak_p, ak_s = prompt(ak_p, n_sub, dh_a), sample(ak_s, n_sub, dh_a)
    bk_p, bk_s = prompt(bk_p, h_b, dh_b), sample(bk_s, h_b, dh_b)
    bv_p, bv_s = prompt(bv_p, h_b, dh_b), sample(bv_s, h_b, dh_b)
    return (x[:n_p].reshape(bp, sp, d), x[n_p:].reshape(bs, ss, d),
            ak_p, av_p, bk_p, bv_p, ak_s, av_s, bk_s, bv_s)
```

```python
import functools
import math

import numpy as np
import jax
import jax.numpy as jnp
from jax import lax
from jax.experimental import pallas as pl
from jax.experimental.pallas import tpu as pltpu

F32 = jnp.float32
BF16 = jnp.bfloat16

CHUNK = 64
ROPE_THETA = 10000.0
EPS = 1e-6
NEG_INF = -1e30
LOG2E = math.log2(math.e)
LANES = 128
VMEM_LIMIT = 56 * 1024 * 1024

NT_DIMS = (((1,), (1,)), ((), ()))


def _pick(n, candidates):
    for c in candidates:
        if n % c == 0:
            return c
    return n


def _params(*sem):
    return pltpu.CompilerParams(dimension_semantics=sem, vmem_limit_bytes=VMEM_LIMIT)


def _rmsnorm_rows(x_ref, g_ref, h_ref, copy_ref=None):
    rows = _pick(x_ref.shape[0], (64, 32, 16))
    g = g_ref[...]

    def chunk(c, carry):
        r = pl.ds(pl.multiple_of(c * rows, rows), rows)
        x = x_ref[r, :]
        ms = jnp.mean(x * x, axis=-1, keepdims=True)
        h_ref[r, :] = (x * lax.rsqrt(ms + EPS) * g).astype(BF16)
        if copy_ref is not None:
            copy_ref[r, :] = x
        return carry

    lax.fori_loop(0, x_ref.shape[0] // rows, chunk, 0)


def _ffn_kernel(x_ref, g_ref, wg_ref, wu_ref, wd_ref, o_ref, h_ref):
    @pl.when(pl.program_id(1) == 0)
    def _():
        _rmsnorm_rows(x_ref, g_ref, h_ref, copy_ref=o_ref)

    h = h_ref[...]
    g = jnp.dot(h, wg_ref[...].astype(BF16), preferred_element_type=F32)
    u = jnp.dot(h, wu_ref[...].astype(BF16), preferred_element_type=F32)
    a = (0.5 * g) * jax.nn.sigmoid(g) * u
    o_ref[...] += jnp.dot(a.astype(BF16), wd_ref[...].astype(BF16),
                          preferred_element_type=F32)


def _ffn(x, gain, wg, wu, wd, li, tm, tf):
    m, d = x.shape
    f = wg.shape[2]
    once = pl.Buffered(1)
    return pl.pallas_call(
        _ffn_kernel,
        grid=(m // tm, f // tf),
        in_specs=[
            pl.BlockSpec((tm, d), lambda i, j: (i, 0), pipeline_mode=once),
            pl.BlockSpec((None, 1, d), lambda i, j: (li, 0, 0)),
            pl.BlockSpec((None, d, tf), lambda i, j: (li, 0, j)),
            pl.BlockSpec((None, d, tf), lambda i, j: (li, 0, j)),
            pl.BlockSpec((None, tf, d), lambda i, j: (li, j, 0)),
        ],
        out_specs=pl.BlockSpec((tm, d), lambda i, j: (i, 0), pipeline_mode=once),
        out_shape=jax.ShapeDtypeStruct((m, d), F32),
        scratch_shapes=[pltpu.VMEM((tm, d), BF16)],
        compiler_params=_params("parallel", "arbitrary"),
        name="ffn",
    )(x, gain.reshape(gain.shape[0], 1, d), wg, wu, wd)


def _store_heads(y, kv_ref, h0, n_heads, dh):
    for hh in range(y.shape[1] // dh):
        kv_ref[pl.ds(h0 + hh, y.shape[0], stride=n_heads), :] = y[:, hh * dh:(hh + 1) * dh]


def _proj_kernel(x_ref, g_ref, w_ref, cos_ref, sin_ref, qn_ref, kn_ref, *refs,
                 n_heads, dh, scale_a, scale_b):
    o32_ref, o16_ref, ka_ref, kb_ref, vb_ref, h_ref = refs[-6:]
    j = pl.program_id(1)
    seg = w_ref.shape[1]
    cw = 2 * dh

    @pl.when(j == 0)
    def _():
        _rmsnorm_rows(x_ref, g_ref, h_ref)

    def sweep(epilogue):
        accs = {}

        def matmul(c):
            accs[c] = jnp.dot(h_ref[...], w_ref[:, c * cw:(c + 1) * cw],
                              preferred_element_type=F32)

        def finish(c):
            epilogue(c, slice(c * cw, (c + 1) * cw), accs.pop(c))

        _skewed((matmul, finish), seg // cw)

    def norm_rope(acc, gain_ref, scale):
        cos = cos_ref[...]
        sin = sin_ref[...]
        outs = []
        for hh in range(acc.shape[1] // dh):
            xh = acc[:, hh * dh:(hh + 1) * dh]
            ms = jnp.mean(xh * xh, axis=-1, keepdims=True)
            y = xh * lax.rsqrt(ms + EPS) * gain_ref[...]
            y = y * cos + pltpu.roll(y, dh // 2, 1) * sin
            if scale is not None:
                y = y * scale
            outs.append(y)
        return jnp.concatenate(outs, axis=1)

    def qa(c, cols, acc):
        o16_ref[:, cols] = norm_rope(acc, qn_ref, scale_a).astype(BF16)

    def ka(c, cols, acc):
        y = norm_rope(acc, kn_ref, None)
        o16_ref[:, cols] = y.astype(BF16)
        _store_heads(y, ka_ref, c * (cw // dh), n_heads, dh)

    def va(c, cols, acc):
        o32_ref[:, cols] = acc
        o16_ref[:, cols] = acc.astype(BF16)

    def qb(c, cols, acc):
        o16_ref[:, cols] = (acc * scale_b).astype(BF16)

    def kv_b(kv_ref):
        def epilogue(c, cols, acc):
            o16_ref[:, cols] = acc.astype(BF16)
            _store_heads(acc, kv_ref, c * (cw // dh), n_heads, dh)
        return epilogue

    def gate(c, cols, acc):
        o32_ref[:, cols] = jax.nn.sigmoid(acc)

    for jj, epilogue in enumerate((qa, ka, va, qb, kv_b(kb_ref), kv_b(vb_ref))):
        pl.when(j == jj)(functools.partial(sweep, epilogue))
    pl.when(j >= 6)(functools.partial(sweep, gate))


def _proj(x, gain, w_in, cos, sin, qn, kn, shared, kv, li, depth, row0, n_rows, tm,
          seg, n_heads, dh, scale_a, scale_b):
    m, d = x.shape
    nj = w_in.shape[2] // seg
    assert nj == 10 and n_heads * dh == seg and n_rows % tm == 0 and row0 % tm == 0
    blk0 = row0 // tm

    def o32_map(i, j):
        return (blk0 + i, jnp.where(j >= 6, j - 5, 0))

    def o16_map(i, j):
        return (blk0 + i, jnp.minimum(j, 5))

    kern = functools.partial(_proj_kernel, n_heads=n_heads, dh=dh,
                             scale_a=scale_a, scale_b=scale_b)
    once = pl.Buffered(1)
    kv_spec = pl.BlockSpec((None, tm * n_heads, dh), lambda i, j: (li, i, 0), pipeline_mode=once)
    kv_shape = jax.ShapeDtypeStruct((depth, n_rows * n_heads, dh), F32)
    in_specs = [
        pl.BlockSpec((tm, d), lambda i, j: (blk0 + i, 0), pipeline_mode=once),
        pl.BlockSpec((None, 1, d), lambda i, j: (li, 0, 0)),
        pl.BlockSpec((None, d, seg), lambda i, j: (li, 0, j)),
        pl.BlockSpec((tm, dh), lambda i, j: (blk0 + i, 0)),
        pl.BlockSpec((tm, dh), lambda i, j: (blk0 + i, 0)),
        pl.BlockSpec((None, 1, dh), lambda i, j: (li, 0, 0)),
        pl.BlockSpec((None, 1, dh), lambda i, j: (li, 0, 0)),
    ]
    args = [x, gain.reshape(gain.shape[0], 1, d), w_in, cos, sin,
            qn.reshape(qn.shape[0], 1, dh), kn.reshape(kn.shape[0], 1, dh)]
    aliases = {}
    for out0, group in ((0, shared), (2, kv)):
        for k, a in enumerate(group or ()):
            in_specs.append(pl.BlockSpec(memory_space=pl.ANY))
            aliases[len(args)] = out0 + k
            args.append(a)
    return pl.pallas_call(
        kern,
        grid=(n_rows // tm, nj),
        in_specs=in_specs,
        out_specs=[
            pl.BlockSpec((tm, seg), o32_map),
            pl.BlockSpec((tm, seg), o16_map),
            kv_spec, kv_spec, kv_spec,
        ],
        out_shape=[
            jax.ShapeDtypeStruct((m, 5 * seg), F32),
            jax.ShapeDtypeStruct((m, 6 * seg), BF16),
            kv_shape, kv_shape, kv_shape,
        ],
        scratch_shapes=[pltpu.VMEM((tm, d), BF16)],
        input_output_aliases=aliases,
        compiler_params=_params("arbitrary", "arbitrary"),
        name="proj",
    )(*args)


def _lam(lam_ref, lam_init):
    v = lam_ref[...]
    s1 = jnp.sum(v[0:1] * v[1:2], axis=1, keepdims=True)
    s2 = jnp.sum(v[2:3] * v[3:4], axis=1, keepdims=True)
    return jnp.exp(s1) - jnp.exp(s2) + lam_init


def _diff_combine(o0, o1, lam, gsub, lam_init):
    o = o0 - lam * o1
    ms = jnp.mean(o * o, axis=-1, keepdims=True)
    return (o * lax.rsqrt(ms + EPS) * gsub) * (1.0 - lam_init)


def _wide(col, n):
    reps = n // LANES
    return col if reps == 1 else jnp.concatenate([col] * reps, axis=1)


def _sb_logs(zn, mask):
    l = jnp.log(1.0 + jnp.exp2(-jnp.abs(zn))) * LOG2E
    lk = jnp.minimum(zn, 0.0) - l
    lsig = lk - zn
    if mask is not None:
        lk = jnp.where(mask, lk, 0.0)
    return lsig, lk


def _skewed(stages, n):
    for t in range(n + len(stages) - 1):
        for s in reversed(range(len(stages))):
            if 0 <= t - s < n:
                stages[s](t - s)


def _attn_a_kernel(q_ref, k_ref, v_ref, lam_ref, gsub_ref, o_ref, m_scr, l_scr, acc_scr,
                   *, n_sub, dh, dv, lam_init, tk):
    qi = pl.program_id(0)
    tq = q_ref.shape[0]
    m_scr[...] = jnp.full(m_scr.shape, NEG_INF, F32)
    l_scr[...] = jnp.zeros(l_scr.shape, F32)
    acc_scr[...] = jnp.zeros(acc_scr.shape, F32)

    def body(j, masked):
        rows = pl.ds(pl.multiple_of(j * tk, tk), tk)
        if masked:
            off = (qi * tq - j * tk) // CHUNK
            r = lax.broadcasted_iota(jnp.int32, (tq, tk), 0)
            c = lax.broadcasted_iota(jnp.int32, (tq, tk), 1)
            mask = (c // CHUNK - r // CHUNK) <= off
        sc, pb, al = {}, {}, {}

        def scores(hd):
            for h in (2 * hd, 2 * hd + 1):
                q = q_ref[:, h * dh:(h + 1) * dh]
                k = k_ref[rows, h * dh:(h + 1) * dh]
                s = lax.dot_general(q, k, NT_DIMS, preferred_element_type=F32)
                if masked:
                    s = jnp.where(mask, s, NEG_INF)
                sc[h] = s

        def softmax_update(hd):
            for h in (2 * hd, 2 * hd + 1):
                s = sc.pop(h)
                m_prev = m_scr[h]
                m_new = jnp.maximum(m_prev, jnp.max(s, axis=1, keepdims=True))
                alpha = jnp.exp2(m_prev - m_new)
                p = jnp.exp2(s - _wide(m_new, tk))
                l_scr[h] = alpha * l_scr[h] + jnp.sum(p, axis=1, keepdims=True)
                m_scr[h] = m_new
                al[h] = alpha
                pb[h] = p.astype(BF16)

        def values(hd):
            v = v_ref[rows, hd * dv:(hd + 1) * dv]
            p2 = jnp.concatenate([pb.pop(2 * hd), pb.pop(2 * hd + 1)], axis=0)
            pv = jnp.dot(p2, v, preferred_element_type=F32)
            for u in range(2):
                h = 2 * hd + u
                acc_scr[h] = _wide(al.pop(h), dv) * acc_scr[h] + pv[u * tq:(u + 1) * tq]

        _skewed((scores, softmax_update, values), n_sub // 2)

    n_full = (qi * tq) // tk

    def visible(j, carry):
        body(j, False)
        return carry

    lax.fori_loop(0, n_full, visible, 0)
    body(n_full, True)

    lam = _lam(lam_ref, lam_init)
    gsub = gsub_ref[...]
    for hd in range(n_sub // 2):
        o0 = acc_scr[2 * hd] / _wide(l_scr[2 * hd], dv)
        o1 = acc_scr[2 * hd + 1] / _wide(l_scr[2 * hd + 1], dv)
        o = _diff_combine(o0, o1, lam, gsub, lam_init)
        o_ref[:, hd * dv:(hd + 1) * dv] = o.astype(BF16)


def _attn_a_prompt(qkv16, oa, lam_vecs, gsub, li, n_rows, tq, tk, seg, n_sub, dh, dv, lam_init):
    assert tk % tq == 0 and n_rows % tk == 0
    kern = functools.partial(_attn_a_kernel, n_sub=n_sub, dh=dh, dv=dv, lam_init=lam_init,
                             tk=tk)
    resident = pl.Buffered(1)

    def kern_alias(q_ref, k_ref, v_ref, lam_ref, gsub_ref, oa_in_ref, o_ref, *scr):
        del oa_in_ref
        kern(q_ref, k_ref, v_ref, lam_ref, gsub_ref, o_ref, *scr)

    return pl.pallas_call(
        kern_alias,
        grid=(n_rows // tq,),
        in_specs=[
            pl.BlockSpec((tq, seg), lambda i: (i, 0)),
            pl.BlockSpec((n_rows, seg), lambda i: (0, 1), pipeline_mode=resident),
            pl.BlockSpec((n_rows, seg), lambda i: (0, 2), pipeline_mode=resident),
            pl.BlockSpec((None, 4, dh), lambda i: (li, 0, 0)),
            pl.BlockSpec((None, 1, dv), lambda i: (li, 0, 0)),
            pl.BlockSpec(memory_space=pl.ANY),
        ],
        out_specs=pl.BlockSpec((tq, seg), lambda i: (i, 0)),
        out_shape=jax.ShapeDtypeStruct(oa.shape, BF16),
        scratch_shapes=[
            pltpu.VMEM((n_sub, tq, LANES), F32),
            pltpu.VMEM((n_sub, tq, LANES), F32),
            pltpu.VMEM((n_sub, tq, dv), F32),
        ],
        input_output_aliases={5: 0},
        compiler_params=_params("arbitrary"),
        name="attn_a_prompt",
    )(qkv16, qkv16, qkv16, lam_vecs, gsub.reshape(gsub.shape[0], 1, dv), oa)


def _sb_heads(q_ref, k_of, v_of, tri, mask, run_scr, er_scr, acc_scr, n_heads, dh):
    zn, lsig, lkb, er, after = {}, {}, {}, {}, {}

    def logits(h):
        q = q_ref[:, h * dh:(h + 1) * dh]
        zn[h] = lax.dot_general(q, k_of(h), NT_DIMS, preferred_element_type=F32)

    def logs(h):
        lsig[h], lk = _sb_logs(zn.pop(h), mask)
        lkb[h] = lk.astype(BF16)
        run = run_scr[h] + jnp.sum(lk, axis=1, keepdims=True)
        run_scr[h] = run
        er[h] = er_scr[h]
        er_scr[h] = jnp.exp2(run)

    def cumsum(h):
        after[h] = jnp.dot(lkb.pop(h), tri, preferred_element_type=F32)

    def weights_pv(h):
        a = jnp.exp2(lsig.pop(h) + after.pop(h))
        if mask is not None:
            a = jnp.where(mask, a, 0.0)
        pv = jnp.dot(a.astype(BF16), v_of(h), preferred_element_type=F32)
        acc_scr[:, h * dh:(h + 1) * dh] += pv * _wide(er.pop(h), dh)

    _skewed((logits, logs, cumsum, weights_pv), n_heads)
    top = er_scr[0]
    for h in range(1, n_heads):
        top = jnp.maximum(top, er_scr[h])
    return jnp.max(top) > 0.0


def _sb_earlier_blocks(first_block, live, block_fn):
    def cond(carry):
        j, alive = carry
        return (j >= 0) & (alive != 0)

    def step(carry):
        j, _ = carry
        return j - 1, block_fn(j).astype(jnp.int32)

    lax.while_loop(cond, step, (first_block, live.astype(jnp.int32)))


def _attn_b_kernel(q_ref, k_ref, v_ref, tri_ref, o_ref, run_scr, er_scr, acc_scr,
                   *, n_heads, dh):
    qi = pl.program_id(0)
    tq = q_ref.shape[0]
    tk = tri_ref.shape[0]
    tri = tri_ref[...]
    run_scr[...] = jnp.zeros(run_scr.shape, F32)
    er_scr[...] = jnp.ones(er_scr.shape, F32)
    acc_scr[...] = jnp.zeros(acc_scr.shape, F32)

    def block(j, mask):
        rows = pl.ds(pl.multiple_of(j * tk, tk), tk)
        return _sb_heads(q_ref,
                         lambda h: k_ref[rows, h * dh:(h + 1) * dh],
                         lambda h: v_ref[rows, h * dh:(h + 1) * dh],
                         tri, mask, run_scr, er_scr, acc_scr, n_heads, dh)

    r = lax.broadcasted_iota(jnp.int32, (tq, tk), 0)
    c = lax.broadcasted_iota(jnp.int32, (tq, tk), 1)
    live = block(qi, c < r)
    _sb_earlier_blocks(qi - 1, live, lambda j: block(j, None))
    o_ref[...] = acc_scr[...].astype(BF16)


def _attn_b_prompt(qkv16, ob, tri, n_rows, tq, seg, n_heads, dh):
    assert tri.shape[0] == tq
    kern = functools.partial(_attn_b_kernel, n_heads=n_heads, dh=dh)
    resident = pl.Buffered(1)

    def kern_alias(q_ref, k_ref, v_ref, tri_ref, ob_in_ref, o_ref, *scr):
        del ob_in_ref
        kern(q_ref, k_ref, v_ref, tri_ref, o_ref, *scr)

    return pl.pallas_call(
        kern_alias,
        grid=(n_rows // tq,),
        in_specs=[
            pl.BlockSpec((tq, seg), lambda i: (i, 3)),
            pl.BlockSpec((n_rows, seg), lambda i: (0, 4), pipeline_mode=resident),
            pl.BlockSpec((n_rows, seg), lambda i: (0, 5), pipeline_mode=resident),
            pl.BlockSpec((tq, tq), lambda i: (0, 0), pipeline_mode=resident),
            pl.BlockSpec(memory_space=pl.ANY),
        ],
        out_specs=pl.BlockSpec((tq, seg), lambda i: (i, 0)),
        out_shape=jax.ShapeDtypeStruct(ob.shape, BF16),
        scratch_shapes=[
            pltpu.VMEM((n_heads, tq, LANES), F32),
            pltpu.VMEM((n_heads, tq, LANES), F32),
            pltpu.VMEM((tq, seg), F32),
        ],
        input_output_aliases={4: 0},
        compiler_params=_params("arbitrary"),
        name="attn_b_prompt",
    )(qkv16, qkv16, qkv16, tri, ob)


def _pad_rows(x, rows):
    n = x.shape[0]
    if n == rows:
        return x
    return jnp.concatenate([x, jnp.zeros((rows - n, x.shape[1]), x.dtype)], axis=0)


def _attn_a_sample_kernel(q_ref, kn_ref, vn_ref, kp_ref, vp_ref, lam_ref, gsub_ref, o_ref,
                          *, n_sub, dh, dv, lam_init):
    sq = q_ref.shape[0]
    past_len = kp_ref.shape[0] // n_sub
    kn = _pad_rows(kn_ref[...], LANES)
    vn = _pad_rows(vn_ref[...], LANES)
    r = lax.broadcasted_iota(jnp.int32, (sq, LANES), 0)
    c = lax.broadcasted_iota(jnp.int32, (sq, LANES), 1)
    mask_new = ((past_len + c) // CHUNK <= (past_len + r) // CHUNK) & (c < sq)
    lam = _lam(lam_ref, lam_init)
    gsub = gsub_ref[...]
    outs = []
    for h in range(n_sub):
        hs = slice(h * dh, (h + 1) * dh)
        vs = slice((h // 2) * dv, (h // 2 + 1) * dv)
        q = q_ref[:, hs]
        kp = kp_ref[pl.ds(h, past_len, stride=n_sub), :].astype(BF16)
        sp = lax.dot_general(q, kp, NT_DIMS, preferred_element_type=F32)
        sn = lax.dot_general(q, kn[:, hs], NT_DIMS, preferred_element_type=F32)
        sn = jnp.where(mask_new, sn, NEG_INF)
        m = jnp.maximum(jnp.max(sp, axis=1, keepdims=True), jnp.max(sn, axis=1, keepdims=True))
        pp = jnp.exp2(sp - m)
        pn = jnp.exp2(sn - m)
        l = jnp.sum(pp, axis=1, keepdims=True) + jnp.sum(pn, axis=1, keepdims=True)
        o = (jnp.dot(pp.astype(BF16), vp_ref[:, vs], preferred_element_type=F32)
             + jnp.dot(pn.astype(BF16), vn[:, vs], preferred_element_type=F32))
        outs.append(o / l)
    for hd in range(n_sub // 2):
        o = _diff_combine(outs[2 * hd], outs[2 * hd + 1], lam, gsub, lam_init)
        o_ref[:, hd * dv:(hd + 1) * dv] = o.astype(BF16)


def _attn_a_sample(qkv16, oa, cache_k, cache_v, li, lam_vecs, gsub, row0, sq, seg,
                   n_sub, dh, dv, lam_init):
    _, nb, past_len, _ = cache_v.shape
    blk0 = row0 // sq
    kern = functools.partial(_attn_a_sample_kernel, n_sub=n_sub, dh=dh, dv=dv,
                             lam_init=lam_init)

    def kern_alias(q_ref, kn_ref, vn_ref, kp_ref, vp_ref, lam_ref, gsub_ref, oa_in_ref, o_ref):
        del oa_in_ref
        kern(q_ref, kn_ref, vn_ref, kp_ref, vp_ref, lam_ref, gsub_ref, o_ref)

    return pl.pallas_call(
        kern_alias,
        grid=(nb,),
        in_specs=[
            pl.BlockSpec((sq, seg), lambda b: (blk0 + b, 0)),
            pl.BlockSpec((sq, seg), lambda b: (blk0 + b, 1)),
            pl.BlockSpec((sq, seg), lambda b: (blk0 + b, 2)),
            pl.BlockSpec((None, None, past_len * n_sub, dh), lambda b: (li, b, 0, 0)),
            pl.BlockSpec((None, None, past_len, seg), lambda b: (li, b, 0, 0)),
            pl.BlockSpec((None, 4, dh), lambda b: (li, 0, 0)),
            pl.BlockSpec((None, 1, dv), lambda b: (li, 0, 0)),
            pl.BlockSpec(memory_space=pl.ANY),
        ],
        out_specs=pl.BlockSpec((sq, seg), lambda b: (blk0 + b, 0)),
        out_shape=jax.ShapeDtypeStruct(oa.shape, BF16),
        input_output_aliases={7: 0},
        compiler_params=_params("arbitrary"),
        name="attn_a_sample",
    )(qkv16, qkv16, qkv16, cache_k, cache_v, lam_vecs, gsub.reshape(gsub.shape[0], 1, dv), oa)


def _attn_b_sample_kernel(q_ref, kn_ref, vn_ref, kp_ref, vp_ref, tri_ref, o_ref,
                          run_scr, er_scr, acc_scr, *, n_heads, dh):
    sq = q_ref.shape[0]
    past_len = kp_ref.shape[0] // n_heads
    tk = tri_ref.shape[0]
    run_scr[...] = jnp.zeros(run_scr.shape, F32)
    er_scr[...] = jnp.ones(er_scr.shape, F32)
    acc_scr[...] = jnp.zeros(acc_scr.shape, F32)
    kn = _pad_rows(kn_ref[...], LANES)
    vn = _pad_rows(vn_ref[...], LANES)
    r = lax.broadcasted_iota(jnp.int32, (sq, LANES), 0)
    c = lax.broadcasted_iota(jnp.int32, (sq, LANES), 1)
    live = _sb_heads(q_ref,
                     lambda h: kn[:, h * dh:(h + 1) * dh],
                     lambda h: vn[:, h * dh:(h + 1) * dh],
                     tri_ref[0:LANES, 0:LANES], c < r, run_scr, er_scr, acc_scr, n_heads, dh)

    def block(j):
        row0 = pl.multiple_of(j * (tk * n_heads), tk * n_heads)

        def head_rows(ref, h):
            return ref[pl.ds(row0 + h, tk, stride=n_heads), :].astype(BF16)

        return _sb_heads(q_ref,
                         lambda h: head_rows(kp_ref, h),
                         lambda h: head_rows(vp_ref, h),
                         tri_ref[...], None, run_scr, er_scr, acc_scr, n_heads, dh)

    _sb_earlier_blocks(past_len // tk - 1, live, block)
    o_ref[...] = acc_scr[...].astype(BF16)


def _attn_b_sample(qkv16, ob, cache_k, cache_v, li, tri, row0, sq, seg, n_heads, dh):
    _, nb, cache_rows, _ = cache_k.shape
    past_len = cache_rows // n_heads
    tk = tri.shape[1]
    assert past_len % tk == 0 and tk >= LANES
    blk0 = row0 // sq
    kern = functools.partial(_attn_b_sample_kernel, n_heads=n_heads, dh=dh)

    def kern_alias(q_ref, kn_ref, vn_ref, kp_ref, vp_ref, tri_ref, ob_in_ref, o_ref, *scr):
        del ob_in_ref
        kern(q_ref, kn_ref, vn_ref, kp_ref, vp_ref, tri_ref, o_ref, *scr)

    return pl.pallas_call(
        kern_alias,
        grid=(nb,),
        in_specs=[
            pl.BlockSpec((sq, seg), lambda b: (blk0 + b, 3)),
            pl.BlockSpec((sq, seg), lambda b: (blk0 + b, 4)),
            pl.BlockSpec((sq, seg), lambda b: (blk0 + b, 5)),
            pl.BlockSpec((None, None, past_len * n_heads, dh), lambda b: (li, b, 0, 0)),
            pl.BlockSpec((None, None, past_len * n_heads, dh), lambda b: (li, b, 0, 0)),
            pl.BlockSpec((tk, tk), lambda b: (0, 0)),
            pl.BlockSpec(memory_space=pl.ANY),
        ],
        out_specs=pl.BlockSpec((sq, seg), lambda b: (blk0 + b, 0)),
        out_shape=jax.ShapeDtypeStruct(ob.shape, BF16),
        scratch_shapes=[
            pltpu.VMEM((n_heads, sq, LANES), F32),
            pltpu.VMEM((n_heads, sq, LANES), F32),
            pltpu.VMEM((sq, seg), F32),
        ],
        input_output_aliases={6: 0},
        compiler_params=_params("arbitrary"),
        name="attn_b_sample",
    )(qkv16, qkv16, qkv16, cache_k, cache_v, tri, ob)


def _merge_kernel(x_ref, oa_ref, ob_ref, ga_ref, gb_ref, wba_ref, wbb_ref, wo_ref, o_ref):
    @pl.when(pl.program_id(1) == 0)
    def _():
        o_ref[...] = x_ref[...]

    ta = jnp.dot(oa_ref[...], wba_ref[...], preferred_element_type=F32)
    tb = jnp.dot(ob_ref[...], wbb_ref[...], preferred_element_type=F32)
    merged = ga_ref[...] * ta + gb_ref[...] * tb
    o_ref[...] += jnp.dot(merged.astype(BF16), wo_ref[...], preferred_element_type=F32)


def _merge(x, oa, ob, proj32, wba, wbb, wo, li, tm, tn, gate_col0):
    m, d = x.shape
    ka = oa.shape[1]
    kb = ob.shape[1]
    nj = d // tn
    ga0 = gate_col0 // tn
    gb0 = ga0 + nj
    return pl.pallas_call(
        _merge_kernel,
        grid=(m // tm, nj),
        in_specs=[
            pl.BlockSpec((tm, d), lambda i, j: (i, 0)),
            pl.BlockSpec((tm, ka), lambda i, j: (i, 0)),
            pl.BlockSpec((tm, kb), lambda i, j: (i, 0)),
            pl.BlockSpec((tm, tn), lambda i, j: (i, ga0 + j)),
            pl.BlockSpec((tm, tn), lambda i, j: (i, gb0 + j)),
            pl.BlockSpec((None, ka, tn), lambda i, j: (li, 0, j)),
            pl.BlockSpec((None, kb, tn), lambda i, j: (li, 0, j)),
            pl.BlockSpec((None, tn, d), lambda i, j: (li, j, 0)),
        ],
        out_specs=pl.BlockSpec((tm, d), lambda i, j: (i, 0)),
        out_shape=jax.ShapeDtypeStruct((m, d), F32),
        compiler_params=_params("parallel", "arbitrary"),
        name="merge",
    )(x, oa, ob, proj32, proj32, wba, wbb, wo)


def _rope_tables(pos, dh):
    inv = 1.0 / (ROPE_THETA ** (jnp.arange(0, dh, 2, dtype=F32) / dh))
    ang = pos.astype(F32)[:, None] * inv[None, :]
    cos = jnp.cos(ang)
    sin = jnp.sin(ang)
    return jnp.concatenate([cos, cos], axis=1), jnp.concatenate([-sin, sin], axis=1)


def kernel(x_prompt, x_sample, cache_a_k, cache_a_v, cache_b_k, cache_b_v,
           ffn1_norm, ffn1_w_gate, ffn1_w_up, ffn1_w_down,
           mix_norm, w_in, a_q_norm, a_k_norm,
           a_lam_q1, a_lam_k1, a_lam_q2, a_lam_k2, a_sub_norm,
           w_branch_a, w_branch_b, w_out,
           ffn2_norm, ffn2_w_gate, ffn2_w_up, ffn2_w_down):
    bp, sp, d = x_prompt.shape
    bs, ss, _ = x_sample.shape
    depth, _, past_len, n_sub, dh_a = cache_a_k.shape
    h_a, dv_a = cache_a_v.shape[3:]
    h_b, dh_b = cache_b_k.shape[3:]
    seg = n_sub * dh_a
    assert bp == 1 and h_a * dv_a == seg and h_b * dh_b == seg and n_sub == 2 * h_a
    assert w_in.shape[2] == 6 * seg + 2 * d and d == 2 * seg

    n_p = bp * sp
    m = n_p + bs * ss
    tm = _pick(m, (768, 512, 384, 256, 128, 64, 32))
    tm_f = _pick(m, (1056, 768, 512, 384, 256, 128, 64, 32))
    tf = _pick(ffn1_w_gate.shape[2], (512, 256, 128))
    tn = _pick(d, (512, 256, 128))
    tm_p = _pick(n_p, (512, 256, 128, 64, 32))
    assert n_p % (m - n_p) == 0
    tq_a, tk_a = _pick(sp, (256, 128)), _pick(sp, (512, 256, 128))
    tq_b = tk_b = _pick(math.gcd(sp, past_len), (256, 128))
    assert n_p % ss == 0 and min(tq_a, tk_a, tq_b) % CHUNK == 0

    x = jnp.concatenate([x_prompt.reshape(n_p, d), x_sample.reshape(bs * ss, d)], axis=0)
    pos = jnp.concatenate([jnp.arange(sp)] * bp + [past_len + jnp.arange(ss)] * bs)
    cos, sin = _rope_tables(pos, dh_a)
    tri = jnp.asarray(np.tril(np.ones((tk_b, tk_b), np.float32), -1), BF16)
    lam_vecs = jnp.stack([a_lam_q1, a_lam_k1, a_lam_q2, a_lam_k2], axis=1)

    bf = lambda w: w.astype(BF16)
    f1g, f1u, f1d = ffn1_w_gate, ffn1_w_up, ffn1_w_down
    f2g, f2u, f2d = ffn2_w_gate, ffn2_w_up, ffn2_w_down
    w_in16, wba16, wbb16, wo16 = bf(w_in), bf(w_branch_a), bf(w_branch_b), bf(w_out)
    rows_of = lambda c: c.reshape(depth, bs, past_len * c.shape[3], c.shape[4])
    ck_a, ck_b, cv_b = rows_of(cache_a_k), rows_of(cache_b_k), rows_of(cache_b_v)
    cv_a = bf(cache_a_v.reshape(depth, bs, past_len, seg))
    scales = (dh_a ** -0.5 * LOG2E, -(dh_b ** -0.5) * LOG2E)

    rows, kv_p, kv_s = [], None, None
    for li in range(depth):
        lam_init = 0.8 - 0.6 * math.exp(-0.3 * li)

        x = _ffn(x, ffn1_norm, f1g, f1u, f1d, li, tm_f, tf)
        proj32, qkv16, *kv_p = _proj(x, mix_norm, w_in16, cos, sin, a_q_norm, a_k_norm,
                                     None, kv_p, li, depth, 0, n_p, tm_p, seg, n_sub, dh_a,
                                     *scales)
        proj32, qkv16, *kv_s = _proj(x, mix_norm, w_in16, cos, sin, a_q_norm, a_k_norm,
                                     (proj32, qkv16), kv_s, li, depth, n_p, m - n_p, m - n_p,
                                     seg, n_sub, dh_a, *scales)

        oa = jnp.zeros((m, seg), BF16)
        oa = _attn_a_prompt(qkv16, oa, lam_vecs, a_sub_norm, li, n_p, tq_a, tk_a, seg,
                            n_sub, dh_a, dv_a, lam_init)
        oa = _attn_a_sample(qkv16, oa, ck_a, cv_a, li, lam_vecs, a_sub_norm, n_p, ss,
                            seg, n_sub, dh_a, dv_a, lam_init)
        ob = jnp.zeros((m, seg), BF16)
        ob = _attn_b_prompt(qkv16, ob, tri, n_p, tq_b, seg, h_b, dh_b)
        ob = _attn_b_sample(qkv16, ob, ck_b, cv_b, li, tri, n_p, ss, seg, h_b, dh_b)

        x = _merge(x, oa, ob, proj32, wba16, wbb16, wo16, li, tm, tn, seg)
        x = _ffn(x, ffn2_norm, f2g, f2u, f2d, li, tm_f, tf)
        rows.append(proj32)

    av_p = jnp.stack([r[:n_p, :seg] for r in rows], axis=0).reshape(depth, bp, sp, h_a, dv_a)
    av_s = jnp.stack([r[n_p:, :seg] for r in rows], axis=0).reshape(depth, bs, ss, h_a, dv_a)
    (ak_p, bk_p, bv_p), (ak_s, bk_s, bv_s) = kv_p, kv_s
    prompt = lambda a, nh, dh: a.reshape(depth, bp, sp, nh, dh)
    sample = lambda a, nh, dh: a.reshape(depth, bs, ss, nh, dh)
    ak_p, ak_s = prompt(ak_p, n_sub, dh_a), sample(ak_s, n_sub, dh_a)
    bk_p, bk_s = prompt(bk_p, h_b, dh_b), sample(bk_s, h_b, dh_b)
    bv_p, bv_s = prompt(bv_p, h_b, dh_b), sample(bv_s, h_b, dh_b)
    return (x[:n_p].reshape(bp, sp, d), x[n_p:].reshape(bs, ss, d),
            ak_p, av_p, bk_p, bv_p, ak_s, av_s, bk_s, bv_s)
```

```python
import functools
import math

import numpy as np
import jax
import jax.numpy as jnp
from jax import lax
from jax.experimental import pallas as pl
from jax.experimental.pallas import tpu as pltpu

F32 = jnp.float32
BF16 = jnp.bfloat16

CHUNK = 64
ROPE_THETA = 10000.0
EPS = 1e-6
NEG_INF = -1e30
LOG2E = math.log2(math.e)
LANES = 128
VMEM_LIMIT = 56 * 1024 * 1024

NT_DIMS = (((1,), (1,)), ((), ()))


def _pick(n, candidates):
    for c in candidates:
        if n % c == 0:
            return c
    return n


def _params(*sem):
    return pltpu.CompilerParams(dimension_semantics=sem, vmem_limit_bytes=VMEM_LIMIT)


def _rmsnorm_rows(x_ref, g_ref, h_ref, copy_ref=None):
    rows = _pick(x_ref.shape[0], (64, 32, 16))
    g = g_ref[...]

    def chunk(c, carry):
        r = pl.ds(pl.multiple_of(c * rows, rows), rows)
        x = x_ref[r, :]
        ms = jnp.mean(x * x, axis=-1, keepdims=True)
        h_ref[r, :] = (x * lax.rsqrt(ms + EPS) * g).astype(BF16)
        if copy_ref is not None:
            copy_ref[r, :] = x
        return carry

    lax.fori_loop(0, x_ref.shape[0] // rows, chunk, 0)


def _ffn_kernel(x_ref, g_ref, wg_ref, wu_ref, wd_ref, o_ref, h_ref):
    @pl.when(pl.program_id(1) == 0)
    def _():
        _rmsnorm_rows(x_ref, g_ref, h_ref, copy_ref=o_ref)

    h = h_ref[...]
    g = jnp.dot(h, wg_ref[...], preferred_element_type=F32)
    u = jnp.dot(h, wu_ref[...], preferred_element_type=F32)
    a = (0.5 * g) * jax.nn.sigmoid(g) * u
    o_ref[...] += jnp.dot(a.astype(BF16), wd_ref[...], preferred_element_type=F32)


def _ffn(x, gain, wg, wu, wd, li, tm, tf):
    m, d = x.shape
    f = wg.shape[2]
    once = pl.Buffered(1)
    return pl.pallas_call(
        _ffn_kernel,
        grid=(m // tm, f // tf),
        in_specs=[
            pl.BlockSpec((tm, d), lambda i, j: (i, 0), pipeline_mode=once),
            pl.BlockSpec((None, 1, d), lambda i, j: (li, 0, 0)),
            pl.BlockSpec((None, d, tf), lambda i, j: (li, 0, j)),
            pl.BlockSpec((None, d, tf), lambda i, j: (li, 0, j)),
            pl.BlockSpec((None, tf, d), lambda i, j: (li, j, 0)),
        ],
        out_specs=pl.BlockSpec((tm, d), lambda i, j: (i, 0), pipeline_mode=once),
        out_shape=jax.ShapeDtypeStruct((m, d), F32),
        scratch_shapes=[pltpu.VMEM((tm, d), BF16)],
        compiler_params=_params("parallel", "arbitrary"),
        name="ffn",
    )(x, gain.reshape(gain.shape[0], 1, d), wg, wu, wd)


def _store_heads(y, kv_ref, h0, n_heads, dh):
    for hh in range(y.shape[1] // dh):
        kv_ref[pl.ds(h0 + hh, y.shape[0], stride=n_heads), :] = y[:, hh * dh:(hh + 1) * dh]


def _proj_kernel(x_ref, g_ref, w_ref, cos_ref, sin_ref, qn_ref, kn_ref, *refs,
                 n_heads, dh, scale_a, scale_b):
    o32_ref, o16_ref, ka_ref, kb_ref, vb_ref, h_ref = refs[-6:]
    j = pl.program_id(1)
    seg = w_ref.shape[1]
    cw = 2 * dh

    @pl.when(j == 0)
    def _():
        _rmsnorm_rows(x_ref, g_ref, h_ref)

    def sweep(epilogue):
        accs = {}

        def matmul(c):
            accs[c] = jnp.dot(h_ref[...], w_ref[:, c * cw:(c + 1) * cw],
                              preferred_element_type=F32)

        def finish(c):
            epilogue(c, slice(c * cw, (c + 1) * cw), accs.pop(c))

        _skewed((matmul, finish), seg // cw)

    def norm_rope(acc, gain_ref, scale):
        cos = cos_ref[...]
        sin = sin_ref[...]
        outs = []
        for hh in range(acc.shape[1] // dh):
            xh = acc[:, hh * dh:(hh + 1) * dh]
            ms = jnp.mean(xh * xh, axis=-1, keepdims=True)
            y = xh * lax.rsqrt(ms + EPS) * gain_ref[...]
            y = y * cos + pltpu.roll(y, dh // 2, 1) * sin
            if scale is not None:
                y = y * scale
            outs.append(y)
        return jnp.concatenate(outs, axis=1)

    def qa(c, cols, acc):
        o16_ref[:, cols] = norm_rope(acc, qn_ref, scale_a).astype(BF16)

    def ka(c, cols, acc):
        y = norm_rope(acc, kn_ref, None)
        o16_ref[:, cols] = y.astype(BF16)
        _store_heads(y, ka_ref, c * (cw // dh), n_heads, dh)

    def va(c, cols, acc):
        o32_ref[:, cols] = acc
        o16_ref[:, cols] = acc.astype(BF16)

    def qb(c, cols, acc):
        o16_ref[:, cols] = (acc * scale_b).astype(BF16)

    def kv_b(kv_ref):
        def epilogue(c, cols, acc):
            o16_ref[:, cols] = acc.astype(BF16)
            _store_heads(acc, kv_ref, c * (cw // dh), n_heads, dh)
        return epilogue

    def gate(c, cols, acc):
        o32_ref[:, cols] = jax.nn.sigmoid(acc)

    for jj, epilogue in enumerate((qa, ka, va, qb, kv_b(kb_ref), kv_b(vb_ref))):
        pl.when(j == jj)(functools.partial(sweep, epilogue))
    pl.when(j >= 6)(functools.partial(sweep, gate))


def _proj(x, gain, w_in, cos, sin, qn, kn, shared, kv, li, depth, row0, n_rows, tm,
          seg, n_heads, dh, scale_a, scale_b):
    m, d = x.shape
    nj = w_in.shape[2] // seg
    assert nj == 10 and n_heads * dh == seg and n_rows % tm == 0 and row0 % tm == 0
    blk0 = row0 // tm

    def o32_map(i, j):
        return (blk0 + i, jnp.where(j >= 6, j - 5, 0))

    def o16_map(i, j):
        return (blk0 + i, jnp.minimum(j, 5))

    kern = functools.partial(_proj_kernel, n_heads=n_heads, dh=dh,
                             scale_a=scale_a, scale_b=scale_b)
    kv_spec = pl.BlockSpec((None, tm * n_heads, dh), lambda i, j: (li, i, 0))
    kv_shape = jax.ShapeDtypeStruct((depth, n_rows * n_heads, dh), F32)
    in_specs = [
        pl.BlockSpec((tm, d), lambda i, j: (blk0 + i, 0)),
        pl.BlockSpec((None, 1, d), lambda i, j: (li, 0, 0)),
        pl.BlockSpec((None, d, seg), lambda i, j: (li, 0, j)),
        pl.BlockSpec((tm, dh), lambda i, j: (blk0 + i, 0)),
        pl.BlockSpec((tm, dh), lambda i, j: (blk0 + i, 0)),
        pl.BlockSpec((None, 1, dh), lambda i, j: (li, 0, 0)),
        pl.BlockSpec((None, 1, dh), lambda i, j: (li, 0, 0)),
    ]
    args = [x, gain.reshape(gain.shape[0], 1, d), w_in, cos, sin,
            qn.reshape(qn.shape[0], 1, dh), kn.reshape(kn.shape[0], 1, dh)]
    aliases = {}
    for out0, group in ((0, shared), (2, kv)):
        for k, a in enumerate(group or ()):
            in_specs.append(pl.BlockSpec(memory_space=pl.ANY))
            aliases[len(args)] = out0 + k
            args.append(a)
    return pl.pallas_call(
        kern,
        grid=(n_rows // tm, nj),
        in_specs=in_specs,
        out_specs=[
            pl.BlockSpec((tm, seg), o32_map),
            pl.BlockSpec((tm, seg), o16_map),
            kv_spec, kv_spec, kv_spec,
        ],
        out_shape=[
            jax.ShapeDtypeStruct((m, 5 * seg), F32),
            jax.ShapeDtypeStruct((m, 6 * seg), BF16),
            kv_shape, kv_shape, kv_shape,
        ],
        scratch_shapes=[pltpu.VMEM((tm, d), BF16)],
        input_output_aliases=aliases,
        compiler_params=_params("arbitrary", "arbitrary"),
        name="proj",
    )(*args)


def _lam(lam_ref, lam_init):
    v = lam_ref[...]
    s1 = jnp.sum(v[0:1] * v[1:2], axis=1, keepdims=True)
    s2 = jnp.sum(v[2:3] * v[3:4], axis=1, keepdims=True)
    return jnp.exp(s1) - jnp.exp(s2) + lam_init


def _diff_combine(o0, o1, lam, gsub, lam_init):
    o = o0 - lam * o1
    ms = jnp.mean(o * o, axis=-1, keepdims=True)
    return (o * lax.rsqrt(ms + EPS) * gsub) * (1.0 - lam_init)


def _wide(col, n):
    reps = n // LANES
    return col if reps == 1 else jnp.concatenate([col] * reps, axis=1)


def _sb_logs(zn, mask):
    l = jnp.log(1.0 + jnp.exp2(-jnp.abs(zn))) * LOG2E
    lk = jnp.minimum(zn, 0.0) - l
    lsig = lk - zn
    if mask is not None:
        lk = jnp.where(mask, lk, 0.0)
    return lsig, lk


def _skewed(stages, n):
    for t in range(n + len(stages) - 1):
        for s in reversed(range(len(stages))):
            if 0 <= t - s < n:
                stages[s](t - s)


def _attn_a_kernel(q_ref, k_ref, v_ref, lam_ref, gsub_ref, o_ref, m_scr, l_scr, acc_scr,
                   *, n_sub, dh, dv, lam_init, tk):
    qi = pl.program_id(0)
    tq = q_ref.shape[0]
    m_scr[...] = jnp.full(m_scr.shape, NEG_INF, F32)
    l_scr[...] = jnp.zeros(l_scr.shape, F32)
    acc_scr[...] = jnp.zeros(acc_scr.shape, F32)

    def body(j, masked):
        rows = pl.ds(pl.multiple_of(j * tk, tk), tk)
        if masked:
            off = (qi * tq - j * tk) // CHUNK
            r = lax.broadcasted_iota(jnp.int32, (tq, tk), 0)
            c = lax.broadcasted_iota(jnp.int32, (tq, tk), 1)
            mask = (c // CHUNK - r // CHUNK) <= off
        sc, pb, al = {}, {}, {}

        def scores(hd):
            for h in (2 * hd, 2 * hd + 1):
                q = q_ref[:, h * dh:(h + 1) * dh]
                k = k_ref[rows, h * dh:(h + 1) * dh]
                s = lax.dot_general(q, k, NT_DIMS, preferred_element_type=F32)
                if masked:
                    s = jnp.where(mask, s, NEG_INF)
                sc[h] = s

        def softmax_update(hd):
            for h in (2 * hd, 2 * hd + 1):
                s = sc.pop(h)
                m_prev = m_scr[h]
                m_new = jnp.maximum(m_prev, jnp.max(s, axis=1, keepdims=True))
                alpha = jnp.exp2(m_prev - m_new)
                p = jnp.exp2(s - _wide(m_new, tk))
                l_scr[h] = alpha * l_scr[h] + jnp.sum(p, axis=1, keepdims=True)
                m_scr[h] = m_new
                al[h] = alpha
                pb[h] = p.astype(BF16)

        def values(hd):
            v = v_ref[rows, hd * dv:(hd + 1) * dv]
            p2 = jnp.concatenate([pb.pop(2 * hd), pb.pop(2 * hd + 1)], axis=0)
            pv = jnp.dot(p2, v, preferred_element_type=F32)
            for u in range(2):
                h = 2 * hd + u
                acc_scr[h] = _wide(al.pop(h), dv) * acc_scr[h] + pv[u * tq:(u + 1) * tq]

        _skewed((scores, softmax_update, values), n_sub // 2)

    n_full = (qi * tq) // tk

    def visible(j, carry):
        body(j, False)
        return carry

    lax.fori_loop(0, n_full, visible, 0)
    body(n_full, True)

    lam = _lam(lam_ref, lam_init)
    gsub = gsub_ref[...]
    for hd in range(n_sub // 2):
        o0 = acc_scr[2 * hd] / _wide(l_scr[2 * hd], dv)
        o1 = acc_scr[2 * hd + 1] / _wide(l_scr[2 * hd + 1], dv)
        o = _diff_combine(o0, o1, lam, gsub, lam_init)
        o_ref[:, hd * dv:(hd + 1) * dv] = o.astype(BF16)


def _attn_a_prompt(qkv16, oa, lam_vecs, gsub, li, n_rows, tq, tk, seg, n_sub, dh, dv, lam_init):
    assert tk % tq == 0 and n_rows % tk == 0
    kern = functools.partial(_attn_a_kernel, n_sub=n_sub, dh=dh, dv=dv, lam_init=lam_init,
                             tk=tk)
    resident = pl.Buffered(1)

    def kern_alias(q_ref, k_ref, v_ref, lam_ref, gsub_ref, oa_in_ref, o_ref, *scr):
        del oa_in_ref
        kern(q_ref, k_ref, v_ref, lam_ref, gsub_ref, o_ref, *scr)

    return pl.pallas_call(
        kern_alias,
        grid=(n_rows // tq,),
        in_specs=[
            pl.BlockSpec((tq, seg), lambda i: (i, 0)),
            pl.BlockSpec((n_rows, seg), lambda i: (0, 1), pipeline_mode=resident),
            pl.BlockSpec((n_rows, seg), lambda i: (0, 2), pipeline_mode=resident),
            pl.BlockSpec((None, 4, dh), lambda i: (li, 0, 0)),
            pl.BlockSpec((None, 1, dv), lambda i: (li, 0, 0)),
            pl.BlockSpec(memory_space=pl.ANY),
        ],
        out_specs=pl.BlockSpec((tq, seg), lambda i: (i, 0)),
        out_shape=jax.ShapeDtypeStruct(oa.shape, BF16),
        scratch_shapes=[
            pltpu.VMEM((n_sub, tq, LANES), F32),
            pltpu.VMEM((n_sub, tq, LANES), F32),
            pltpu.VMEM((n_sub, tq, dv), F32),
        ],
        input_output_aliases={5: 0},
        compiler_params=_params("arbitrary"),
        name="attn_a_prompt",
    )(qkv16, qkv16, qkv16, lam_vecs, gsub.reshape(gsub.shape[0], 1, dv), oa)


def _sb_heads(q_ref, k_of, v_of, tri, mask, run_scr, er_scr, acc_scr, n_heads, dh):
    zn, lsig, lkb, er, after = {}, {}, {}, {}, {}

    def logits(h):
        q = q_ref[:, h * dh:(h + 1) * dh]
        zn[h] = lax.dot_general(q, k_of(h), NT_DIMS, preferred_element_type=F32)

    def logs(h):
        lsig[h], lk = _sb_logs(zn.pop(h), mask)
        lkb[h] = lk.astype(BF16)
        run = run_scr[h] + jnp.sum(lk, axis=1, keepdims=True)
        run_scr[h] = run
        er[h] = er_scr[h]
        er_scr[h] = jnp.exp2(run)

    def cumsum(h):
        after[h] = jnp.dot(lkb.pop(h), tri, preferred_element_type=F32)

    def weights_pv(h):
        a = jnp.exp2(lsig.pop(h) + after.pop(h))
        if mask is not None:
            a = jnp.where(mask, a, 0.0)
        pv = jnp.dot(a.astype(BF16), v_of(h), preferred_element_type=F32)
        acc_scr[:, h * dh:(h + 1) * dh] += pv * _wide(er.pop(h), dh)

    _skewed((logits, logs, cumsum, weights_pv), n_heads)
    top = er_scr[0]
    for h in range(1, n_heads):
        top = jnp.maximum(top, er_scr[h])
    return jnp.max(top) > 0.0


def _sb_earlier_blocks(first_block, live, block_fn):
    def cond(carry):
        j, alive = carry
        return (j >= 0) & (alive != 0)

    def step(carry):
        j, _ = carry
        return j - 1, block_fn(j).astype(jnp.int32)

    lax.while_loop(cond, step, (first_block, live.astype(jnp.int32)))


def _attn_b_kernel(q_ref, k_ref, v_ref, tri_ref, o_ref, run_scr, er_scr, acc_scr,
                   *, n_heads, dh):
    qi = pl.program_id(0)
    tq = q_ref.shape[0]
    tk = tri_ref.shape[0]
    tri = tri_ref[...]
    run_scr[...] = jnp.zeros(run_scr.shape, F32)
    er_scr[...] = jnp.ones(er_scr.shape, F32)
    acc_scr[...] = jnp.zeros(acc_scr.shape, F32)

    def block(j, mask):
        rows = pl.ds(pl.multiple_of(j * tk, tk), tk)
        return _sb_heads(q_ref,
                         lambda h: k_ref[rows, h * dh:(h + 1) * dh],
                         lambda h: v_ref[rows, h * dh:(h + 1) * dh],
                         tri, mask, run_scr, er_scr, acc_scr, n_heads, dh)

    r = lax.broadcasted_iota(jnp.int32, (tq, tk), 0)
    c = lax.broadcasted_iota(jnp.int32, (tq, tk), 1)
    live = block(qi, c < r)
    _sb_earlier_blocks(qi - 1, live, lambda j: block(j, None))
    o_ref[...] = acc_scr[...].astype(BF16)


def _attn_b_prompt(qkv16, ob, tri, n_rows, tq, seg, n_heads, dh):
    assert tri.shape[0] == tq
    kern = functools.partial(_attn_b_kernel, n_heads=n_heads, dh=dh)
    resident = pl.Buffered(1)

    def kern_alias(q_ref, k_ref, v_ref, tri_ref, ob_in_ref, o_ref, *scr):
        del ob_in_ref
        kern(q_ref, k_ref, v_ref, tri_ref, o_ref, *scr)

    return pl.pallas_call(
        kern_alias,
        grid=(n_rows // tq,),
        in_specs=[
            pl.BlockSpec((tq, seg), lambda i: (i, 3)),
            pl.BlockSpec((n_rows, seg), lambda i: (0, 4), pipeline_mode=resident),
            pl.BlockSpec((n_rows, seg), lambda i: (0, 5), pipeline_mode=resident),
            pl.BlockSpec((tq, tq), lambda i: (0, 0), pipeline_mode=resident),
            pl.BlockSpec(memory_space=pl.ANY),
        ],
        out_specs=pl.BlockSpec((tq, seg), lambda i: (i, 0)),
        out_shape=jax.ShapeDtypeStruct(ob.shape, BF16),
        scratch_shapes=[
            pltpu.VMEM((n_heads, tq, LANES), F32),
            pltpu.VMEM((n_heads, tq, LANES), F32),
            pltpu.VMEM((tq, seg), F32),
        ],
        input_output_aliases={4: 0},
        compiler_params=_params("arbitrary"),
        name="attn_b_prompt",
    )(qkv16, qkv16, qkv16, tri, ob)


def _pad_rows(x, rows):
    n = x.shape[0]
    if n == rows:
        return x
    return jnp.concatenate([x, jnp.zeros((rows - n, x.shape[1]), x.dtype)], axis=0)


def _attn_a_sample_kernel(q_ref, kn_ref, vn_ref, kp_ref, vp_ref, lam_ref, gsub_ref, o_ref,
                          *, n_sub, dh, dv, lam_init):
    sq = q_ref.shape[0]
    past_len = kp_ref.shape[0] // n_sub
    kn = _pad_rows(kn_ref[...], LANES)
    vn = _pad_rows(vn_ref[...], LANES)
    r = lax.broadcasted_iota(jnp.int32, (sq, LANES), 0)
    c = lax.broadcasted_iota(jnp.int32, (sq, LANES), 1)
    mask_new = ((past_len + c) // CHUNK <= (past_len + r) // CHUNK) & (c < sq)
    lam = _lam(lam_ref, lam_init)
    gsub = gsub_ref[...]
    outs = []
    for h in range(n_sub):
        hs = slice(h * dh, (h + 1) * dh)
        vs = slice((h // 2) * dv, (h // 2 + 1) * dv)
        q = q_ref[:, hs]
        kp = kp_ref[pl.ds(h, past_len, stride=n_sub), :].astype(BF16)
        sp = lax.dot_general(q, kp, NT_DIMS, preferred_element_type=F32)
        sn = lax.dot_general(q, kn[:, hs], NT_DIMS, preferred_element_type=F32)
        sn = jnp.where(mask_new, sn, NEG_INF)
        m = jnp.maximum(jnp.max(sp, axis=1, keepdims=True), jnp.max(sn, axis=1, keepdims=True))
        pp = jnp.exp2(sp - m)
        pn = jnp.exp2(sn - m)
        l = jnp.sum(pp, axis=1, keepdims=True) + jnp.sum(pn, axis=1, keepdims=True)
        o = (jnp.dot(pp.astype(BF16), vp_ref[:, vs], preferred_element_type=F32)
             + jnp.dot(pn.astype(BF16), vn[:, vs], preferred_element_type=F32))
        outs.append(o / l)
    for hd in range(n_sub // 2):
        o = _diff_combine(outs[2 * hd], outs[2 * hd + 1], lam, gsub, lam_init)
        o_ref[:, hd * dv:(hd + 1) * dv] = o.astype(BF16)


def _attn_a_sample(qkv16, oa, cache_k, cache_v, li, lam_vecs, gsub, row0, sq, seg,
                   n_sub, dh, dv, lam_init):
    _, nb, past_len, _ = cache_v.shape
    blk0 = row0 // sq
    kern = functools.partial(_attn_a_sample_kernel, n_sub=n_sub, dh=dh, dv=dv,
                             lam_init=lam_init)

    def kern_alias(q_ref, kn_ref, vn_ref, kp_ref, vp_ref, lam_ref, gsub_ref, oa_in_ref, o_ref):
        del oa_in_ref
        kern(q_ref, kn_ref, vn_ref, kp_ref, vp_ref, lam_ref, gsub_ref, o_ref)

    return pl.pallas_call(
        kern_alias,
        grid=(nb,),
        in_specs=[
            pl.BlockSpec((sq, seg), lambda b: (blk0 + b, 0)),
            pl.BlockSpec((sq, seg), lambda b: (blk0 + b, 1)),
            pl.BlockSpec((sq, seg), lambda b: (blk0 + b, 2)),
            pl.BlockSpec((None, None, past_len * n_sub, dh), lambda b: (li, b, 0, 0)),
            pl.BlockSpec((None, None, past_len, seg), lambda b: (li, b, 0, 0)),
            pl.BlockSpec((None, 4, dh), lambda b: (li, 0, 0)),
            pl.BlockSpec((None, 1, dv), lambda b: (li, 0, 0)),
            pl.BlockSpec(memory_space=pl.ANY),
        ],
        out_specs=pl.BlockSpec((sq, seg), lambda b: (blk0 + b, 0)),
        out_shape=jax.ShapeDtypeStruct(oa.shape, BF16),
        input_output_aliases={7: 0},
        compiler_params=_params("arbitrary"),
        name="attn_a_sample",
    )(qkv16, qkv16, qkv16, cache_k, cache_v, lam_vecs, gsub.reshape(gsub.shape[0], 1, dv), oa)


def _attn_b_sample_kernel(q_ref, kn_ref, vn_ref, kp_ref, vp_ref, tri_ref, o_ref,
                          run_scr, er_scr, acc_scr, *, n_heads, dh):
    sq = q_ref.shape[0]
    past_len = kp_ref.shape[0] // n_heads
    tk = tri_ref.shape[0]
    run_scr[...] = jnp.zeros(run_scr.shape, F32)
    er_scr[...] = jnp.ones(er_scr.shape, F32)
    acc_scr[...] = jnp.zeros(acc_scr.shape, F32)
    kn = _pad_rows(kn_ref[...], LANES)
    vn = _pad_rows(vn_ref[...], LANES)
    r = lax.broadcasted_iota(jnp.int32, (sq, LANES), 0)
    c = lax.broadcasted_iota(jnp.int32, (sq, LANES), 1)
    live = _sb_heads(q_ref,
                     lambda h: kn[:, h * dh:(h + 1) * dh],
                     lambda h: vn[:, h * dh:(h + 1) * dh],
                     tri_ref[0:LANES, 0:LANES], c < r, run_scr, er_scr, acc_scr, n_heads, dh)

    def block(j):
        row0 = pl.multiple_of(j * (tk * n_heads), tk * n_heads)

        def head_rows(ref, h):
            return ref[pl.ds(row0 + h, tk, stride=n_heads), :].astype(BF16)

        return _sb_heads(q_ref,
                         lambda h: head_rows(kp_ref, h),
                         lambda h: head_rows(vp_ref, h),
                         tri_ref[...], None, run_scr, er_scr, acc_scr, n_heads, dh)

    _sb_earlier_blocks(past_len // tk - 1, live, block)
    o_ref[...] = acc_scr[...].astype(BF16)


def _attn_b_sample(qkv16, ob, cache_k, cache_v, li, tri, row0, sq, seg, n_heads, dh):
    _, nb, cache_rows, _ = cache_k.shape
    past_len = cache_rows // n_heads
    tk = tri.shape[1]
    assert past_len % tk == 0 and tk >= LANES
    blk0 = row0 // sq
    kern = functools.partial(_attn_b_sample_kernel, n_heads=n_heads, dh=dh)

    def kern_alias(q_ref, kn_ref, vn_ref, kp_ref, vp_ref, tri_ref, ob_in_ref, o_ref, *scr):
        del ob_in_ref
        kern(q_ref, kn_ref, vn_ref, kp_ref, vp_ref, tri_ref, o_ref, *scr)

    return pl.pallas_call(
        kern_alias,
        grid=(nb,),
        in_specs=[
            pl.BlockSpec((sq, seg), lambda b: (blk0 + b, 3)),
            pl.BlockSpec((sq, seg), lambda b: (blk0 + b, 4)),
            pl.BlockSpec((sq, seg), lambda b: (blk0 + b, 5)),
            pl.BlockSpec((None, None, past_len * n_heads, dh), lambda b: (li, b, 0, 0)),
            pl.BlockSpec((None, None, past_len * n_heads, dh), lambda b: (li, b, 0, 0)),
            pl.BlockSpec((tk, tk), lambda b: (0, 0)),
            pl.BlockSpec(memory_space=pl.ANY),
        ],
        out_specs=pl.BlockSpec((sq, seg), lambda b: (blk0 + b, 0)),
        out_shape=jax.ShapeDtypeStruct(ob.shape, BF16),
        scratch_shapes=[
            pltpu.VMEM((n_heads, sq, LANES), F32),
            pltpu.VMEM((n_heads, sq, LANES), F32),
            pltpu.VMEM((sq, seg), F32),
        ],
        input_output_aliases={6: 0},
        compiler_params=_params("arbitrary"),
        name="attn_b_sample",
    )(qkv16, qkv16, qkv16, cache_k, cache_v, tri, ob)


def _merge_kernel(x_ref, oa_ref, ob_ref, ga_ref, gb_ref, wba_ref, wbb_ref, wo_ref, o_ref):
    @pl.when(pl.program_id(1) == 0)
    def _():
        o_ref[...] = x_ref[...]

    ta = jnp.dot(oa_ref[...], wba_ref[...], preferred_element_type=F32)
    tb = jnp.dot(ob_ref[...], wbb_ref[...], preferred_element_type=F32)
    merged = ga_ref[...] * ta + gb_ref[...] * tb
    o_ref[...] += jnp.dot(merged.astype(BF16), wo_ref[...], preferred_element_type=F32)


def _merge(x, oa, ob, proj32, wba, wbb, wo, li, tm, tn, gate_col0):
    m, d = x.shape
    ka = oa.shape[1]
    kb = ob.shape[1]
    nj = d // tn
    ga0 = gate_col0 // tn
    gb0 = ga0 + nj
    once = pl.Buffered(1)
    return pl.pallas_call(
        _merge_kernel,
        grid=(m // tm, nj),
        in_specs=[
            pl.BlockSpec((tm, d), lambda i, j: (i, 0), pipeline_mode=once),
            pl.BlockSpec((tm, ka), lambda i, j: (i, 0)),
            pl.BlockSpec((tm, kb), lambda i, j: (i, 0)),
            pl.BlockSpec((tm, tn), lambda i, j: (i, ga0 + j)),
            pl.BlockSpec((tm, tn), lambda i, j: (i, gb0 + j)),
            pl.BlockSpec((None, ka, tn), lambda i, j: (li, 0, j)),
            pl.BlockSpec((None, kb, tn), lambda i, j: (li, 0, j)),
            pl.BlockSpec((None, tn, d), lambda i, j: (li, j, 0)),
        ],
        out_specs=pl.BlockSpec((tm, d), lambda i, j: (i, 0), pipeline_mode=once),
        out_shape=jax.ShapeDtypeStruct((m, d), F32),
        compiler_params=_params("parallel", "arbitrary"),
        name="merge",
    )(x, oa, ob, proj32, proj32, wba, wbb, wo)


def _rope_tables(pos, dh):
    inv = 1.0 / (ROPE_THETA ** (jnp.arange(0, dh, 2, dtype=F32) / dh))
    ang = pos.astype(F32)[:, None] * inv[None, :]
    cos = jnp.cos(ang)
    sin = jnp.sin(ang)
    return jnp.concatenate([cos, cos], axis=1), jnp.concatenate([-sin, sin], axis=1)


def kernel(x_prompt, x_sample, cache_a_k, cache_a_v, cache_b_k, cache_b_v,
           ffn1_norm, ffn1_w_gate, ffn1_w_up, ffn1_w_down,
           mix_norm, w_in, a_q_norm, a_k_norm,
           a_lam_q1, a_lam_k1, a_lam_q2, a_lam_k2, a_sub_norm,
           w_branch_a, w_branch_b, w_out,
           ffn2_norm, ffn2_w_gate, ffn2_w_up, ffn2_w_down):
    bp, sp, d = x_prompt.shape
    bs, ss, _ = x_sample.shape
    depth, _, past_len, n_sub, dh_a = cache_a_k.shape
    h_a, dv_a = cache_a_v.shape[3:]
    h_b, dh_b = cache_b_k.shape[3:]
    seg = n_sub * dh_a
    assert bp == 1 and h_a * dv_a == seg and h_b * dh_b == seg and n_sub == 2 * h_a
    assert w_in.shape[2] == 6 * seg + 2 * d and d == 2 * seg

    n_p = bp * sp
    m = n_p + bs * ss
    tm = _pick(m, (768, 512, 384, 256, 128, 64, 32))
    tm_f = _pick(m, (1056, 768, 512, 384, 256, 128, 64, 32))
    tf = _pick(ffn1_w_gate.shape[2], (512, 256, 128))
    tn = _pick(d, (1024, 512, 256, 128))
    tm_p = _pick(n_p, (512, 256, 128, 64, 32))
    assert n_p % (m - n_p) == 0
    tq_a, tk_a = _pick(sp, (256, 128)), _pick(sp, (512, 256, 128))
    tq_b = tk_b = _pick(math.gcd(sp, past_len), (256, 128))
    assert n_p % ss == 0 and min(tq_a, tk_a, tq_b) % CHUNK == 0

    x = jnp.concatenate([x_prompt.reshape(n_p, d), x_sample.reshape(bs * ss, d)], axis=0)
    pos = jnp.concatenate([jnp.arange(sp)] * bp + [past_len + jnp.arange(ss)] * bs)
    cos, sin = _rope_tables(pos, dh_a)
    tri = jnp.asarray(np.tril(np.ones((tk_b, tk_b), np.float32), -1), BF16)
    lam_vecs = jnp.stack([a_lam_q1, a_lam_k1, a_lam_q2, a_lam_k2], axis=1)

    bf = lambda w: w.astype(BF16)
    f1g, f1u, f1d = bf(ffn1_w_gate), bf(ffn1_w_up), bf(ffn1_w_down)
    f2g, f2u, f2d = bf(ffn2_w_gate), bf(ffn2_w_up), bf(ffn2_w_down)
    w_in16, wba16, wbb16, wo16 = bf(w_in), bf(w_branch_a), bf(w_branch_b), bf(w_out)
    rows_of = lambda c: c.reshape(depth, bs, past_len * c.shape[3], c.shape[4])
    ck_a, ck_b, cv_b = rows_of(cache_a_k), rows_of(cache_b_k), rows_of(cache_b_v)
    cv_a = bf(cache_a_v.reshape(depth, bs, past_len, seg))
    scales = (dh_a ** -0.5 * LOG2E, -(dh_b ** -0.5) * LOG2E)

    rows, kv_p, kv_s = [], None, None
    for li in range(depth):
        lam_init = 0.8 - 0.6 * math.exp(-0.3 * li)

        x = _ffn(x, ffn1_norm, f1g, f1u, f1d, li, tm_f, tf)
        proj32, qkv16, *kv_p = _proj(x, mix_norm, w_in16, cos, sin, a_q_norm, a_k_norm,
                                     None, kv_p, li, depth, 0, n_p, tm_p, seg, n_sub, dh_a,
                                     *scales)
        proj32, qkv16, *kv_s = _proj(x, mix_norm, w_in16, cos, sin, a_q_norm, a_k_norm,
                                     (proj32, qkv16), kv_s, li, depth, n_p, m - n_p, m - n_p,
                                     seg, n_sub, dh_a, *scales)

        oa = jnp.zeros((m, seg), BF16)
        oa = _attn_a_prompt(qkv16, oa, lam_vecs, a_sub_norm, li, n_p, tq_a, tk_a, seg,
                            n_sub, dh_a, dv_a, lam_init)
        oa = _attn_a_sample(qkv16, oa, ck_a, cv_a, li, lam_vecs, a_sub_norm, n_p, ss,
                            seg, n_sub, dh_a, dv_a, lam_init)
        ob = jnp.zeros((m, seg), BF16)
        ob = _attn_b_prompt(qkv16, ob, tri, n_p, tq_b, seg, h_b, dh_b)
        ob = _attn_b_sample(qkv16, ob, ck_b, cv_b, li, tri, n_p, ss, seg, h_b, dh_b)

        x = _merge(x, oa, ob, proj32, wba16, wbb16, wo16, li, tm, tn, seg)
        x = _ffn(x, ffn2_norm, f2g, f2u, f2d, li, tm_f, tf)
        rows.append(proj32)

    av_p = jnp.stack([r[:n_p, :seg] for r in rows], axis=0).reshape(depth, bp, sp, h_a, dv_a)
    av_s = jnp.stack([r[n_p:, :seg] for r in rows], axis=0).reshape(depth, bs, ss, h_a, dv_a)
    (ak_p, bk_p, bv_p), (ak_s, bk_s, bv_s) = kv_p, kv_s
    prompt = lambda a, nh, dh: a.reshape(depth, bp, sp, nh, dh)
    sample = lambda a, nh, dh: a.reshape(depth, bs, ss, nh, dh)
    ak_p, ak_s = prompt(ak_p, n_sub, dh_a), sample(ak_s, n_sub, dh_a)
    bk_p, bk_s = prompt(bk_p, h_b, dh_b), sample(bk_s, h_b, dh_b)
    bv_p, bv_s = prompt(bv_p, h_b, dh_b), sample(bv_s, h_b, dh_b)
    return (x[:n_p].reshape(bp, sp, d), x[n_p:].reshape(bs, ss, d),
            ak_p, av_p, bk_p, bv_p, ak_s, av_s, bk_s, bv_s)
```

```python
import functools
import math

import numpy as np
import jax
import jax.numpy as jnp
from jax import lax
from jax.experimental import pallas as pl
from jax.experimental.pallas import tpu as pltpu

F32 = jnp.float32
BF16 = jnp.bfloat16

CHUNK = 64
ROPE_THETA = 10000.0
EPS = 1e-6
NEG_INF = -1e30
LOG2E = math.log2(math.e)
LANES = 128
VMEM_LIMIT = 56 * 1024 * 1024

NT_DIMS = (((1,), (1,)), ((), ()))


def _pick(n, candidates):
    for c in candidates:
        if n % c == 0:
            return c
    return n


def _params(*sem):
    return pltpu.CompilerParams(dimension_semantics=sem, vmem_limit_bytes=VMEM_LIMIT)


def _rmsnorm_rows(x_ref, g_ref, h_ref, copy_ref=None):
    rows = _pick(x_ref.shape[0], (64, 32, 16))
    g = g_ref[...]

    def chunk(c, carry):
        r = pl.ds(pl.multiple_of(c * rows, rows), rows)
        x = x_ref[r, :]
        ms = jnp.mean(x * x, axis=-1, keepdims=True)
        h_ref[r, :] = (x * lax.rsqrt(ms + EPS) * g).astype(BF16)
        if copy_ref is not None:
            copy_ref[r, :] = x
        return carry

    lax.fori_loop(0, x_ref.shape[0] // rows, chunk, 0)


def _ffn_kernel(x_ref, g_ref, wg_ref, wu_ref, wd_ref, o_ref, h_ref):
    @pl.when(pl.program_id(1) == 0)
    def _():
        _rmsnorm_rows(x_ref, g_ref, h_ref, copy_ref=o_ref)

    h = h_ref[...]
    g = jnp.dot(h, wg_ref[...].astype(BF16), preferred_element_type=F32)
    u = jnp.dot(h, wu_ref[...].astype(BF16), preferred_element_type=F32)
    a = (0.5 * g) * jax.nn.sigmoid(g) * u
    o_ref[...] += jnp.dot(a.astype(BF16), wd_ref[...].astype(BF16),
                          preferred_element_type=F32)


def _ffn(x, gain, wg, wu, wd, li, tm, tf):
    m, d = x.shape
    f = wg.shape[2]
    return pl.pallas_call(
        _ffn_kernel,
        grid=(m // tm, f // tf),
        in_specs=[
            pl.BlockSpec((tm, d), lambda i, j: (i, 0)),
            pl.BlockSpec((None, 1, d), lambda i, j: (li, 0, 0)),
            pl.BlockSpec((None, d, tf), lambda i, j: (li, 0, j)),
            pl.BlockSpec((None, d, tf), lambda i, j: (li, 0, j)),
            pl.BlockSpec((None, tf, d), lambda i, j: (li, j, 0)),
        ],
        out_specs=pl.BlockSpec((tm, d), lambda i, j: (i, 0)),
        out_shape=jax.ShapeDtypeStruct((m, d), F32),
        scratch_shapes=[pltpu.VMEM((tm, d), BF16)],
        compiler_params=_params("parallel", "arbitrary"),
        name="ffn",
    )(x, gain.reshape(gain.shape[0], 1, d), wg, wu, wd)


def _store_heads(y, kv_ref, h0, n_heads, dh):
    for hh in range(y.shape[1] // dh):
        kv_ref[pl.ds(h0 + hh, y.shape[0], stride=n_heads), :] = y[:, hh * dh:(hh + 1) * dh]


def _proj_kernel(x_ref, g_ref, w_ref, cos_ref, sin_ref, qn_ref, kn_ref, *refs,
                 n_heads, dh, scale_a, scale_b):
    o32_ref, o16_ref, ka_ref, kb_ref, vb_ref, h_ref = refs[-6:]
    j = pl.program_id(1)
    seg = w_ref.shape[1]
    cw = 2 * dh

    @pl.when(j == 0)
    def _():
        _rmsnorm_rows(x_ref, g_ref, h_ref)

    def sweep(epilogue):
        accs = {}

        def matmul(c):
            accs[c] = jnp.dot(h_ref[...], w_ref[:, c * cw:(c + 1) * cw].astype(BF16),
                              preferred_element_type=F32)

        def finish(c):
            epilogue(c, slice(c * cw, (c + 1) * cw), accs.pop(c))

        _skewed((matmul, finish), seg // cw)

    def norm_rope(acc, gain_ref, scale):
        cos = cos_ref[...]
        sin = sin_ref[...]
        outs = []
        for hh in range(acc.shape[1] // dh):
            xh = acc[:, hh * dh:(hh + 1) * dh]
            ms = jnp.mean(xh * xh, axis=-1, keepdims=True)
            y = xh * lax.rsqrt(ms + EPS) * gain_ref[...]
            y = y * cos + pltpu.roll(y, dh // 2, 1) * sin
            if scale is not None:
                y = y * scale
            outs.append(y)
        return jnp.concatenate(outs, axis=1)

    def qa(c, cols, acc):
        o16_ref[:, cols] = norm_rope(acc, qn_ref, scale_a).astype(BF16)

    def ka(c, cols, acc):
        y = norm_rope(acc, kn_ref, None)
        o16_ref[:, cols] = y.astype(BF16)
        _store_heads(y, ka_ref, c * (cw // dh), n_heads, dh)

    def va(c, cols, acc):
        o32_ref[:, cols] = acc
        o16_ref[:, cols] = acc.astype(BF16)

    def qb(c, cols, acc):
        o16_ref[:, cols] = (acc * scale_b).astype(BF16)

    def kv_b(kv_ref):
        def epilogue(c, cols, acc):
            o16_ref[:, cols] = acc.astype(BF16)
            _store_heads(acc, kv_ref, c * (cw // dh), n_heads, dh)
        return epilogue

    def gate(c, cols, acc):
        o32_ref[:, cols] = jax.nn.sigmoid(acc)

    for jj, epilogue in enumerate((qa, ka, va, qb, kv_b(kb_ref), kv_b(vb_ref))):
        pl.when(j == jj)(functools.partial(sweep, epilogue))
    pl.when(j >= 6)(functools.partial(sweep, gate))


def _proj(x, gain, w_in, cos, sin, qn, kn, shared, kv, li, depth, row0, n_rows, tm,
          seg, n_heads, dh, scale_a, scale_b):
    m, d = x.shape
    nj = w_in.shape[2] // seg
    assert nj == 10 and n_heads * dh == seg and n_rows % tm == 0 and row0 % tm == 0
    blk0 = row0 // tm

    def o32_map(i, j):
        return (blk0 + i, jnp.where(j >= 6, j - 5, 0))

    def o16_map(i, j):
        return (blk0 + i, jnp.minimum(j, 5))

    kern = functools.partial(_proj_kernel, n_heads=n_heads, dh=dh,
                             scale_a=scale_a, scale_b=scale_b)
    kv_spec = pl.BlockSpec((None, tm * n_heads, dh), lambda i, j: (li, i, 0))
    kv_shape = jax.ShapeDtypeStruct((depth, n_rows * n_heads, dh), F32)
    in_specs = [
        pl.BlockSpec((tm, d), lambda i, j: (blk0 + i, 0)),
        pl.BlockSpec((None, 1, d), lambda i, j: (li, 0, 0)),
        pl.BlockSpec((None, d, seg), lambda i, j: (li, 0, j)),
        pl.BlockSpec((tm, dh), lambda i, j: (blk0 + i, 0)),
        pl.BlockSpec((tm, dh), lambda i, j: (blk0 + i, 0)),
        pl.BlockSpec((None, 1, dh), lambda i, j: (li, 0, 0)),
        pl.BlockSpec((None, 1, dh), lambda i, j: (li, 0, 0)),
    ]
    args = [x, gain.reshape(gain.shape[0], 1, d), w_in, cos, sin,
            qn.reshape(qn.shape[0], 1, dh), kn.reshape(kn.shape[0], 1, dh)]
    aliases = {}
    for out0, group in ((0, shared), (2, kv)):
        for k, a in enumerate(group or ()):
            in_specs.append(pl.BlockSpec(memory_space=pl.ANY))
            aliases[len(args)] = out0 + k
            args.append(a)
    return pl.pallas_call(
        kern,
        grid=(n_rows // tm, nj),
        in_specs=in_specs,
        out_specs=[
            pl.BlockSpec((tm, seg), o32_map),
            pl.BlockSpec((tm, seg), o16_map),
            kv_spec, kv_spec, kv_spec,
        ],
        out_shape=[
            jax.ShapeDtypeStruct((m, 5 * seg), F32),
            jax.ShapeDtypeStruct((m, 6 * seg), BF16),
            kv_shape, kv_shape, kv_shape,
        ],
        scratch_shapes=[pltpu.VMEM((tm, d), BF16)],
        input_output_aliases=aliases,
        compiler_params=_params("arbitrary", "arbitrary"),
        name="proj",
    )(*args)


def _lam(lam_ref, lam_init):
    v = lam_ref[...]
    s1 = jnp.sum(v[0:1] * v[1:2], axis=1, keepdims=True)
    s2 = jnp.sum(v[2:3] * v[3:4], axis=1, keepdims=True)
    return jnp.exp(s1) - jnp.exp(s2) + lam_init


def _diff_combine(o0, o1, lam, gsub, lam_init):
    o = o0 - lam * o1
    ms = jnp.mean(o * o, axis=-1, keepdims=True)
    return (o * lax.rsqrt(ms + EPS) * gsub) * (1.0 - lam_init)


def _wide(col, n):
    reps = n // LANES
    return col if reps == 1 else jnp.concatenate([col] * reps, axis=1)


def _sb_logs(zn, mask):
    l = jnp.log(1.0 + jnp.exp2(-jnp.abs(zn))) * LOG2E
    lk = jnp.minimum(zn, 0.0) - l
    lsig = lk - zn
    if mask is not None:
        lk = jnp.where(mask, lk, 0.0)
    return lsig, lk


def _skewed(stages, n):
    for t in range(n + len(stages) - 1):
        for s in reversed(range(len(stages))):
            if 0 <= t - s < n:
                stages[s](t - s)


def _attn_a_kernel(q_ref, k_ref, v_ref, lam_ref, gsub_ref, o_ref, m_scr, l_scr, acc_scr,
                   *, n_sub, dh, dv, lam_init, tk):
    qi = pl.program_id(0)
    tq = q_ref.shape[0]
    m_scr[...] = jnp.full(m_scr.shape, NEG_INF, F32)
    l_scr[...] = jnp.zeros(l_scr.shape, F32)
    acc_scr[...] = jnp.zeros(acc_scr.shape, F32)

    def body(j, masked):
        rows = pl.ds(pl.multiple_of(j * tk, tk), tk)
        if masked:
            off = (qi * tq - j * tk) // CHUNK
            r = lax.broadcasted_iota(jnp.int32, (tq, tk), 0)
            c = lax.broadcasted_iota(jnp.int32, (tq, tk), 1)
            mask = (c // CHUNK - r // CHUNK) <= off
        sc, pb, al = {}, {}, {}

        def scores(hd):
            for h in (2 * hd, 2 * hd + 1):
                q = q_ref[:, h * dh:(h + 1) * dh]
                k = k_ref[rows, h * dh:(h + 1) * dh]
                s = lax.dot_general(q, k, NT_DIMS, preferred_element_type=F32)
                if masked:
                    s = jnp.where(mask, s, NEG_INF)
                sc[h] = s

        def softmax_update(hd):
            for h in (2 * hd, 2 * hd + 1):
                s = sc.pop(h)
                m_prev = m_scr[h]
                m_new = jnp.maximum(m_prev, jnp.max(s, axis=1, keepdims=True))
                alpha = jnp.exp2(m_prev - m_new)
                p = jnp.exp2(s - _wide(m_new, tk))
                l_scr[h] = alpha * l_scr[h] + jnp.sum(p, axis=1, keepdims=True)
                m_scr[h] = m_new
                al[h] = alpha
                pb[h] = p.astype(BF16)

        def values(hd):
            v = v_ref[rows, hd * dv:(hd + 1) * dv]
            p2 = jnp.concatenate([pb.pop(2 * hd), pb.pop(2 * hd + 1)], axis=0)
            pv = jnp.dot(p2, v, preferred_element_type=F32)
            for u in range(2):
                h = 2 * hd + u
                acc_scr[h] = _wide(al.pop(h), dv) * acc_scr[h] + pv[u * tq:(u + 1) * tq]

        _skewed((scores, softmax_update, values), n_sub // 2)

    n_full = (qi * tq) // tk

    def visible(j, carry):
        body(j, False)
        return carry

    lax.fori_loop(0, n_full, visible, 0)
    body(n_full, True)

    lam = _lam(lam_ref, lam_init)
    gsub = gsub_ref[...]
    for hd in range(n_sub // 2):
        o0 = acc_scr[2 * hd] / _wide(l_scr[2 * hd], dv)
        o1 = acc_scr[2 * hd + 1] / _wide(l_scr[2 * hd + 1], dv)
        o = _diff_combine(o0, o1, lam, gsub, lam_init)
        o_ref[:, hd * dv:(hd + 1) * dv] = o.astype(BF16)


def _attn_a_prompt(qkv16, oa, lam_vecs, gsub, li, n_rows, tq, tk, seg, n_sub, dh, dv, lam_init):
    assert tk % tq == 0 and n_rows % tk == 0
    kern = functools.partial(_attn_a_kernel, n_sub=n_sub, dh=dh, dv=dv, lam_init=lam_init,
                             tk=tk)
    resident = pl.Buffered(1)

    def kern_alias(q_ref, k_ref, v_ref, lam_ref, gsub_ref, oa_in_ref, o_ref, *scr):
        del oa_in_ref
        kern(q_ref, k_ref, v_ref, lam_ref, gsub_ref, o_ref, *scr)

    return pl.pallas_call(
        kern_alias,
        grid=(n_rows // tq,),
        in_specs=[
            pl.BlockSpec((tq, seg), lambda i: (i, 0)),
            pl.BlockSpec((n_rows, seg), lambda i: (0, 1), pipeline_mode=resident),
            pl.BlockSpec((n_rows, seg), lambda i: (0, 2), pipeline_mode=resident),
            pl.BlockSpec((None, 4, dh), lambda i: (li, 0, 0)),
            pl.BlockSpec((None, 1, dv), lambda i: (li, 0, 0)),
            pl.BlockSpec(memory_space=pl.ANY),
        ],
        out_specs=pl.BlockSpec((tq, seg), lambda i: (i, 0)),
        out_shape=jax.ShapeDtypeStruct(oa.shape, BF16),
        scratch_shapes=[
            pltpu.VMEM((n_sub, tq, LANES), F32),
            pltpu.VMEM((n_sub, tq, LANES), F32),
            pltpu.VMEM((n_sub, tq, dv), F32),
        ],
        input_output_aliases={5: 0},
        compiler_params=_params("arbitrary"),
        name="attn_a_prompt",
    )(qkv16, qkv16, qkv16, lam_vecs, gsub.reshape(gsub.shape[0], 1, dv), oa)


def _sb_heads(q_ref, k_of, v_of, tri, mask, run_scr, er_scr, acc_scr, n_heads, dh):
    zn, lsig, lkb, er, after = {}, {}, {}, {}, {}

    def logits(h):
        q = q_ref[:, h * dh:(h + 1) * dh]
        zn[h] = lax.dot_general(q, k_of(h), NT_DIMS, preferred_element_type=F32)

    def logs(h):
        lsig[h], lk = _sb_logs(zn.pop(h), mask)
        lkb[h] = lk.astype(BF16)
        run = run_scr[h] + jnp.sum(lk, axis=1, keepdims=True)
        run_scr[h] = run
        er[h] = er_scr[h]
        er_scr[h] = jnp.exp2(run)

    def cumsum(h):
        after[h] = jnp.dot(lkb.pop(h), tri, preferred_element_type=F32)

    def weights_pv(h):
        a = jnp.exp2(lsig.pop(h) + after.pop(h))
        if mask is not None:
            a = jnp.where(mask, a, 0.0)
        pv = jnp.dot(a.astype(BF16), v_of(h), preferred_element_type=F32)
        acc_scr[:, h * dh:(h + 1) * dh] += pv * _wide(er.pop(h), dh)

    _skewed((logits, logs, cumsum, weights_pv), n_heads)
    top = er_scr[0]
    for h in range(1, n_heads):
        top = jnp.maximum(top, er_scr[h])
    return jnp.max(top) > 0.0


def _sb_earlier_blocks(first_block, live, block_fn):
    def cond(carry):
        j, alive = carry
        return (j >= 0) & (alive != 0)

    def step(carry):
        j, _ = carry
        return j - 1, block_fn(j).astype(jnp.int32)

    lax.while_loop(cond, step, (first_block, live.astype(jnp.int32)))


def _attn_b_kernel(q_ref, k_ref, v_ref, tri_ref, o_ref, run_scr, er_scr, acc_scr,
                   *, n_heads, dh):
    qi = pl.program_id(0)
    tq = q_ref.shape[0]
    tk = tri_ref.shape[0]
    tri = tri_ref[...]
    run_scr[...] = jnp.zeros(run_scr.shape, F32)
    er_scr[...] = jnp.ones(er_scr.shape, F32)
    acc_scr[...] = jnp.zeros(acc_scr.shape, F32)

    def block(j, mask):
        rows = pl.ds(pl.multiple_of(j * tk, tk), tk)
        return _sb_heads(q_ref,
                         lambda h: k_ref[rows, h * dh:(h + 1) * dh],
                         lambda h: v_ref[rows, h * dh:(h + 1) * dh],
                         tri, mask, run_scr, er_scr, acc_scr, n_heads, dh)

    r = lax.broadcasted_iota(jnp.int32, (tq, tk), 0)
    c = lax.broadcasted_iota(jnp.int32, (tq, tk), 1)
    live = block(qi, c < r)
    _sb_earlier_blocks(qi - 1, live, lambda j: block(j, None))
    o_ref[...] = acc_scr[...].astype(BF16)


def _attn_b_prompt(qkv16, ob, tri, n_rows, tq, seg, n_heads, dh):
    assert tri.shape[0] == tq
    kern = functools.partial(_attn_b_kernel, n_heads=n_heads, dh=dh)
    resident = pl.Buffered(1)

    def kern_alias(q_ref, k_ref, v_ref, tri_ref, ob_in_ref, o_ref, *scr):
        del ob_in_ref
        kern(q_ref, k_ref, v_ref, tri_ref, o_ref, *scr)

    return pl.pallas_call(
        kern_alias,
        grid=(n_rows // tq,),
        in_specs=[
            pl.BlockSpec((tq, seg), lambda i: (i, 3)),
            pl.BlockSpec((n_rows, seg), lambda i: (0, 4), pipeline_mode=resident),
            pl.BlockSpec((n_rows, seg), lambda i: (0, 5), pipeline_mode=resident),
            pl.BlockSpec((tq, tq), lambda i: (0, 0), pipeline_mode=resident),
            pl.BlockSpec(memory_space=pl.ANY),
        ],
        out_specs=pl.BlockSpec((tq, seg), lambda i: (i, 0)),
        out_shape=jax.ShapeDtypeStruct(ob.shape, BF16),
        scratch_shapes=[
            pltpu.VMEM((n_heads, tq, LANES), F32),
            pltpu.VMEM((n_heads, tq, LANES), F32),
            pltpu.VMEM((tq, seg), F32),
        ],
        input_output_aliases={4: 0},
        compiler_params=_params("arbitrary"),
        name="attn_b_prompt",
    )(qkv16, qkv16, qkv16, tri, ob)


def _pad_rows(x, rows):
    n = x.shape[0]
    if n == rows:
        return x
    return jnp.concatenate([x, jnp.zeros((rows - n, x.shape[1]), x.dtype)], axis=0)


def _attn_a_sample_kernel(q_ref, kn_ref, vn_ref, kp_ref, vp_ref, lam_ref, gsub_ref, o_ref,
                          *, n_sub, dh, dv, lam_init):
    sq = q_ref.shape[0]
    past_len = kp_ref.shape[0] // n_sub
    kn = _pad_rows(kn_ref[...], LANES)
    vn = _pad_rows(vn_ref[...], LANES)
    r = lax.broadcasted_iota(jnp.int32, (sq, LANES), 0)
    c = lax.broadcasted_iota(jnp.int32, (sq, LANES), 1)
    mask_new = ((past_len + c) // CHUNK <= (past_len + r) // CHUNK) & (c < sq)
    lam = _lam(lam_ref, lam_init)
    gsub = gsub_ref[...]
    outs = []
    for h in range(n_sub):
        hs = slice(h * dh, (h + 1) * dh)
        vs = slice((h // 2) * dv, (h // 2 + 1) * dv)
        q = q_ref[:, hs]
        kp = kp_ref[pl.ds(h, past_len, stride=n_sub), :].astype(BF16)
        sp = lax.dot_general(q, kp, NT_DIMS, preferred_element_type=F32)
        sn = lax.dot_general(q, kn[:, hs], NT_DIMS, preferred_element_type=F32)
        sn = jnp.where(mask_new, sn, NEG_INF)
        m = jnp.maximum(jnp.max(sp, axis=1, keepdims=True), jnp.max(sn, axis=1, keepdims=True))
        pp = jnp.exp2(sp - m)
        pn = jnp.exp2(sn - m)
        l = jnp.sum(pp, axis=1, keepdims=True) + jnp.sum(pn, axis=1, keepdims=True)
        o = (jnp.dot(pp.astype(BF16), vp_ref[:, vs], preferred_element_type=F32)
             + jnp.dot(pn.astype(BF16), vn[:, vs], preferred_element_type=F32))
        outs.append(o / l)
    for hd in range(n_sub // 2):
        o = _diff_combine(outs[2 * hd], outs[2 * hd + 1], lam, gsub, lam_init)
        o_ref[:, hd * dv:(hd + 1) * dv] = o.astype(BF16)


def _attn_a_sample(qkv16, oa, cache_k, cache_v, li, lam_vecs, gsub, row0, sq, seg,
                   n_sub, dh, dv, lam_init):
    _, nb, past_len, _ = cache_v.shape
    blk0 = row0 // sq
    kern = functools.partial(_attn_a_sample_kernel, n_sub=n_sub, dh=dh, dv=dv,
                             lam_init=lam_init)

    def kern_alias(q_ref, kn_ref, vn_ref, kp_ref, vp_ref, lam_ref, gsub_ref, oa_in_ref, o_ref):
        del oa_in_ref
        kern(q_ref, kn_ref, vn_ref, kp_ref, vp_ref, lam_ref, gsub_ref, o_ref)

    return pl.pallas_call(
        kern_alias,
        grid=(nb,),
        in_specs=[
            pl.BlockSpec((sq, seg), lambda b: (blk0 + b, 0)),
            pl.BlockSpec((sq, seg), lambda b: (blk0 + b, 1)),
            pl.BlockSpec((sq, seg), lambda b: (blk0 + b, 2)),
            pl.BlockSpec((None, None, past_len * n_sub, dh), lambda b: (li, b, 0, 0)),
            pl.BlockSpec((None, None, past_len, seg), lambda b: (li, b, 0, 0)),
            pl.BlockSpec((None, 4, dh), lambda b: (li, 0, 0)),
            pl.BlockSpec((None, 1, dv), lambda b: (li, 0, 0)),
            pl.BlockSpec(memory_space=pl.ANY),
        ],
        out_specs=pl.BlockSpec((sq, seg), lambda b: (blk0 + b, 0)),
        out_shape=jax.ShapeDtypeStruct(oa.shape, BF16),
        input_output_aliases={7: 0},
        compiler_params=_params("arbitrary"),
        name="attn_a_sample",
    )(qkv16, qkv16, qkv16, cache_k, cache_v, lam_vecs, gsub.reshape(gsub.shape[0], 1, dv), oa)


def _attn_b_sample_kernel(q_ref, kn_ref, vn_ref, kp_ref, vp_ref, tri_ref, o_ref,
                          run_scr, er_scr, acc_scr, *, n_heads, dh):
    sq = q_ref.shape[0]
    past_len = kp_ref.shape[0] // n_heads
    tk = tri_ref.shape[0]
    run_scr[...] = jnp.zeros(run_scr.shape, F32)
    er_scr[...] = jnp.ones(er_scr.shape, F32)
    acc_scr[...] = jnp.zeros(acc_scr.shape, F32)
    kn = _pad_rows(kn_ref[...], LANES)
    vn = _pad_rows(vn_ref[...], LANES)
    r = lax.broadcasted_iota(jnp.int32, (sq, LANES), 0)
    c = lax.broadcasted_iota(jnp.int32, (sq, LANES), 1)
    live = _sb_heads(q_ref,
                     lambda h: kn[:, h * dh:(h + 1) * dh],
                     lambda h: vn[:, h * dh:(h + 1) * dh],
                     tri_ref[0:LANES, 0:LANES], c < r, run_scr, er_scr, acc_scr, n_heads, dh)

    def block(j):
        row0 = pl.multiple_of(j * (tk * n_heads), tk * n_heads)

        def head_rows(ref, h):
            return ref[pl.ds(row0 + h, tk, stride=n_heads), :].astype(BF16)

        return _sb_heads(q_ref,
                         lambda h: head_rows(kp_ref, h),
                         lambda h: head_rows(vp_ref, h),
                         tri_ref[...], None, run_scr, er_scr, acc_scr, n_heads, dh)

    _sb_earlier_blocks(past_len // tk - 1, live, block)
    o_ref[...] = acc_scr[...].astype(BF16)


def _attn_b_sample(qkv16, ob, cache_k, cache_v, li, tri, row0, sq, seg, n_heads, dh):
    _, nb, cache_rows, _ = cache_k.shape
    past_len = cache_rows // n_heads
    tk = tri.shape[1]
    assert past_len % tk == 0 and tk >= LANES
    blk0 = row0 // sq
    kern = functools.partial(_attn_b_sample_kernel, n_heads=n_heads, dh=dh)

    def kern_alias(q_ref, kn_ref, vn_ref, kp_ref, vp_ref, tri_ref, ob_in_ref, o_ref, *scr):
        del ob_in_ref
        kern(q_ref, kn_ref, vn_ref, kp_ref, vp_ref, tri_ref, o_ref, *scr)

    return pl.pallas_call(
        kern_alias,
        grid=(nb,),
        in_specs=[
            pl.BlockSpec((sq, seg), lambda b: (blk0 + b, 3)),
            pl.BlockSpec((sq, seg), lambda b: (blk0 + b, 4)),
            pl.BlockSpec((sq, seg), lambda b: (blk0 + b, 5)),
            pl.BlockSpec((None, None, past_len * n_heads, dh), lambda b: (li, b, 0, 0)),
            pl.BlockSpec((None, None, past_len * n_heads, dh), lambda b: (li, b, 0, 0)),
            pl.BlockSpec((tk, tk), lambda b: (0, 0)),
            pl.BlockSpec(memory_space=pl.ANY),
        ],
        out_specs=pl.BlockSpec((sq, seg), lambda b: (blk0 + b, 0)),
        out_shape=jax.ShapeDtypeStruct(ob.shape, BF16),
        scratch_shapes=[
            pltpu.VMEM((n_heads, sq, LANES), F32),
            pltpu.VMEM((n_heads, sq, LANES), F32),
            pltpu.VMEM((sq, seg), F32),
        ],
        input_output_aliases={6: 0},
        compiler_params=_params("arbitrary"),
        name="attn_b_sample",
    )(qkv16, qkv16, qkv16, cache_k, cache_v, tri, ob)


def _merge_kernel(x_ref, oa_ref, ob_ref, ga_ref, gb_ref, wba_ref, wbb_ref, wo_ref, o_ref):
    @pl.when(pl.program_id(1) == 0)
    def _():
        o_ref[...] = x_ref[...]

    ta = jnp.dot(oa_ref[...], wba_ref[...], preferred_element_type=F32)
    tb = jnp.dot(ob_ref[...], wbb_ref[...], preferred_element_type=F32)
    merged = ga_ref[...] * ta + gb_ref[...] * tb
    o_ref[...] += jnp.dot(merged.astype(BF16), wo_ref[...], preferred_element_type=F32)


def _merge(x, oa, ob, proj32, wba, wbb, wo, li, tm, tn, gate_col0):
    m, d = x.shape
    ka = oa.shape[1]
    kb = ob.shape[1]
    nj = d // tn
    ga0 = gate_col0 // tn
    gb0 = ga0 + nj
    return pl.pallas_call(
        _merge_kernel,
        grid=(m // tm, nj),
        in_specs=[
            pl.BlockSpec((tm, d), lambda i, j: (i, 0)),
            pl.BlockSpec((tm, ka), lambda i, j: (i, 0)),
            pl.BlockSpec((tm, kb), lambda i, j: (i, 0)),
            pl.BlockSpec((tm, tn), lambda i, j: (i, ga0 + j)),
            pl.BlockSpec((tm, tn), lambda i, j: (i, gb0 + j)),
            pl.BlockSpec((None, ka, tn), lambda i, j: (li, 0, j)),
            pl.BlockSpec((None, kb, tn), lambda i, j: (li, 0, j)),
            pl.BlockSpec((None, tn, d), lambda i, j: (li, j, 0)),
        ],
        out_specs=pl.BlockSpec((tm, d), lambda i, j: (i, 0)),
        out_shape=jax.ShapeDtypeStruct((m, d), F32),
        compiler_params=_params("parallel", "arbitrary"),
        name="merge",
    )(x, oa, ob, proj32, proj32, wba, wbb, wo)


def _rope_tables(pos, dh):
    inv = 1.0 / (ROPE_THETA ** (jnp.arange(0, dh, 2, dtype=F32) / dh))
    ang = pos.astype(F32)[:, None] * inv[None, :]
    cos = jnp.cos(ang)
    sin = jnp.sin(ang)
    return jnp.concatenate([cos, cos], axis=1), jnp.concatenate([-sin, sin], axis=1)


def kernel(x_prompt, x_sample, cache_a_k, cache_a_v, cache_b_k, cache_b_v,
           ffn1_norm, ffn1_w_gate, ffn1_w_up, ffn1_w_down,
           mix_norm, w_in, a_q_norm, a_k_norm,
           a_lam_q1, a_lam_k1, a_lam_q2, a_lam_k2, a_sub_norm,
           w_branch_a, w_branch_b, w_out,
           ffn2_norm, ffn2_w_gate, ffn2_w_up, ffn2_w_down):
    bp, sp, d = x_prompt.shape
    bs, ss, _ = x_sample.shape
    depth, _, past_len, n_sub, dh_a = cache_a_k.shape
    h_a, dv_a = cache_a_v.shape[3:]
    h_b, dh_b = cache_b_k.shape[3:]
    seg = n_sub * dh_a
    assert bp == 1 and h_a * dv_a == seg and h_b * dh_b == seg and n_sub == 2 * h_a
    assert w_in.shape[2] == 6 * seg + 2 * d and d == 2 * seg

    n_p = bp * sp
    m = n_p + bs * ss
    tm = _pick(m, (768, 512, 384, 256, 128, 64, 32))
    tf = _pick(ffn1_w_gate.shape[2], (512, 256, 128))
    tn = _pick(d, (512, 256, 128))
    tm_p = _pick(n_p, (512, 256, 128, 64, 32))
    assert n_p % (m - n_p) == 0
    tq_a, tk_a = _pick(sp, (256, 128)), _pick(sp, (512, 256, 128))
    tq_b = tk_b = _pick(math.gcd(sp, past_len), (256, 128))
    assert n_p % ss == 0 and min(tq_a, tk_a, tq_b) % CHUNK == 0

    x = jnp.concatenate([x_prompt.reshape(n_p, d), x_sample.reshape(bs * ss, d)], axis=0)
    pos = jnp.concatenate([jnp.arange(sp)] * bp + [past_len + jnp.arange(ss)] * bs)
    cos, sin = _rope_tables(pos, dh_a)
    tri = jnp.asarray(np.tril(np.ones((tk_b, tk_b), np.float32), -1), BF16)
    lam_vecs = jnp.stack([a_lam_q1, a_lam_k1, a_lam_q2, a_lam_k2], axis=1)

    bf = lambda w: w.astype(BF16)
    f1g, f1u, f1d = ffn1_w_gate, ffn1_w_up, ffn1_w_down
    f2g, f2u, f2d = ffn2_w_gate, ffn2_w_up, ffn2_w_down
    wba16, wbb16, wo16 = bf(w_branch_a), bf(w_branch_b), bf(w_out)
    rows_of = lambda c: c.reshape(depth, bs, past_len * c.shape[3], c.shape[4])
    ck_a, ck_b, cv_b = rows_of(cache_a_k), rows_of(cache_b_k), rows_of(cache_b_v)
    cv_a = bf(cache_a_v.reshape(depth, bs, past_len, seg))
    scales = (dh_a ** -0.5 * LOG2E, -(dh_b ** -0.5) * LOG2E)

    rows, kv_p, kv_s = [], None, None
    for li in range(depth):
        lam_init = 0.8 - 0.6 * math.exp(-0.3 * li)

        x = _ffn(x, ffn1_norm, f1g, f1u, f1d, li, tm, tf)
        proj32, qkv16, *kv_p = _proj(x, mix_norm, w_in, cos, sin, a_q_norm, a_k_norm,
                                     None, kv_p, li, depth, 0, n_p, tm_p, seg, n_sub, dh_a,
                                     *scales)
        proj32, qkv16, *kv_s = _proj(x, mix_norm, w_in, cos, sin, a_q_norm, a_k_norm,
                                     (proj32, qkv16), kv_s, li, depth, n_p, m - n_p, m - n_p,
                                     seg, n_sub, dh_a, *scales)

        oa = jnp.zeros((m, seg), BF16)
        oa = _attn_a_prompt(qkv16, oa, lam_vecs, a_sub_norm, li, n_p, tq_a, tk_a, seg,
                            n_sub, dh_a, dv_a, lam_init)
        oa = _attn_a_sample(qkv16, oa, ck_a, cv_a, li, lam_vecs, a_sub_norm, n_p, ss,
                            seg, n_sub, dh_a, dv_a, lam_init)
        ob = jnp.zeros((m, seg), BF16)
        ob = _attn_b_prompt(qkv16, ob, tri, n_p, tq_b, seg, h_b, dh_b)
        ob = _attn_b_sample(qkv16, ob, ck_b, cv_b, li, tri, n_p, ss, seg, h_b, dh_b)

        x = _merge(x, oa, ob, proj32, wba16, wbb16, wo16, li, tm, tn, seg)
        x = _ffn(x, ffn2_norm, f2g, f2u, f2d, li, tm, tf)
        rows.append(proj32)

    av_p = jnp.stack([r[:n_p, :seg] for r in rows], axis=0).reshape(depth, bp, sp, h_a, dv_a)
    av_s = jnp.stack([r[n_p:, :seg] for r in rows], axis=0).reshape(depth, bs, ss, h_a, dv_a)
    (ak_p, bk_p, bv_p), (ak_s, bk_s, bv_s) = kv_p, kv_s
    prompt = lambda a, nh, dh: a.reshape(depth, bp, sp, nh, dh)
    sample = lambda a, nh, dh: a.reshape(depth, bs, ss, nh, dh)
    ak_p, ak_s = prompt(ak_p, n_sub, dh_a), sample(ak_s, n_sub, dh_a)
    bk_p, bk_s = prompt(bk_p, h_b, dh_b), sample(bk_s, h_b, dh_b)
    bv_p, bv_s = prompt(bv_p, h_b, dh_b), sample(bv_s, h_b, dh_b)
    return (x[:n_p].reshape(bp, sp, d), x[n_p:].reshape(bs, ss, d),
            ak_p, av_p, bk_p, bv_p, ak_s, av_s, bk_s, bv_s)
```

```python
import functools
import math

import numpy as np
import jax
import jax.numpy as jnp
from jax import lax
from jax.experimental import pallas as pl
from jax.experimental.pallas import tpu as pltpu

F32 = jnp.float32
BF16 = jnp.bfloat16

CHUNK = 64
ROPE_THETA = 10000.0
EPS = 1e-6
NEG_INF = -1e30
LOG2E = math.log2(math.e)
LANES = 128
VMEM_LIMIT = 56 * 1024 * 1024

NT_DIMS = (((1,), (1,)), ((), ()))


def _pick(n, candidates):
    for c in candidates:
        if n % c == 0:
            return c
    return n


def _params(*sem):
    return pltpu.CompilerParams(dimension_semantics=sem, vmem_limit_bytes=VMEM_LIMIT)


def _rmsnorm_rows(x_ref, g_ref, h_ref, copy_ref=None):
    rows = _pick(x_ref.shape[0], (64, 32, 16))
    g = g_ref[...]

    def chunk(c, carry):
        r = pl.ds(pl.multiple_of(c * rows, rows), rows)
        x = x_ref[r, :]
        ms = jnp.mean(x * x, axis=-1, keepdims=True)
        h_ref[r, :] = (x * lax.rsqrt(ms + EPS) * g).astype(BF16)
        if copy_ref is not None:
            copy_ref[r, :] = x
        return carry

    lax.fori_loop(0, x_ref.shape[0] // rows, chunk, 0)


def _ffn_kernel(x_ref, g_ref, wg_ref, wu_ref, wd_ref, o_ref, h_ref):
    @pl.when(pl.program_id(1) == 0)
    def _():
        _rmsnorm_rows(x_ref, g_ref, h_ref, copy_ref=o_ref)

    h = h_ref[...]
    g = jnp.dot(h, wg_ref[...].astype(BF16), preferred_element_type=F32)
    u = jnp.dot(h, wu_ref[...].astype(BF16), preferred_element_type=F32)
    a = (0.5 * g) * jax.nn.sigmoid(g) * u
    o_ref[...] += jnp.dot(a.astype(BF16), wd_ref[...].astype(BF16),
                          preferred_element_type=F32)


def _ffn(x, gain, wg, wu, wd, li, tm, tf):
    m, d = x.shape
    f = wg.shape[2]
    return pl.pallas_call(
        _ffn_kernel,
        grid=(m // tm, f // tf),
        in_specs=[
            pl.BlockSpec((tm, d), lambda i, j: (i, 0)),
            pl.BlockSpec((None, 1, d), lambda i, j: (li, 0, 0)),
            pl.BlockSpec((None, d, tf), lambda i, j: (li, 0, j)),
            pl.BlockSpec((None, d, tf), lambda i, j: (li, 0, j)),
            pl.BlockSpec((None, tf, d), lambda i, j: (li, j, 0)),
        ],
        out_specs=pl.BlockSpec((tm, d), lambda i, j: (i, 0)),
        out_shape=jax.ShapeDtypeStruct((m, d), F32),
        scratch_shapes=[pltpu.VMEM((tm, d), BF16)],
        compiler_params=_params("parallel", "arbitrary"),
        name="ffn",
    )(x, gain.reshape(gain.shape[0], 1, d), wg, wu, wd)


def _store_heads(y, kv_ref, h0, n_heads, dh):
    for hh in range(y.shape[1] // dh):
        kv_ref[pl.ds(h0 + hh, y.shape[0], stride=n_heads), :] = y[:, hh * dh:(hh + 1) * dh]


def _proj_kernel(x_ref, g_ref, w_ref, cos_ref, sin_ref, qn_ref, kn_ref, *refs,
                 n_heads, dh, scale_a, scale_b):
    o32_ref, o16_ref, ka_ref, kb_ref, vb_ref, h_ref = refs[-6:]
    j = pl.program_id(1)
    seg = w_ref.shape[1]
    cw = 2 * dh

    @pl.when(j == 0)
    def _():
        _rmsnorm_rows(x_ref, g_ref, h_ref)

    def sweep(epilogue):
        accs = {}

        def matmul(c):
            accs[c] = jnp.dot(h_ref[...], w_ref[:, c * cw:(c + 1) * cw],
                              preferred_element_type=F32)

        def finish(c):
            epilogue(c, slice(c * cw, (c + 1) * cw), accs.pop(c))

        _skewed((matmul, finish), seg // cw)

    def norm_rope(acc, gain_ref, scale):
        cos = cos_ref[...]
        sin = sin_ref[...]
        outs = []
        for hh in range(acc.shape[1] // dh):
            xh = acc[:, hh * dh:(hh + 1) * dh]
            ms = jnp.mean(xh * xh, axis=-1, keepdims=True)
            y = xh * lax.rsqrt(ms + EPS) * gain_ref[...]
            y = y * cos + pltpu.roll(y, dh // 2, 1) * sin
            if scale is not None:
                y = y * scale
            outs.append(y)
        return jnp.concatenate(outs, axis=1)

    def qa(c, cols, acc):
        o16_ref[:, cols] = norm_rope(acc, qn_ref, scale_a).astype(BF16)

    def ka(c, cols, acc):
        y = norm_rope(acc, kn_ref, None)
        o16_ref[:, cols] = y.astype(BF16)
        _store_heads(y, ka_ref, c * (cw // dh), n_heads, dh)

    def va(c, cols, acc):
        o32_ref[:, cols] = acc
        o16_ref[:, cols] = acc.astype(BF16)

    def qb(c, cols, acc):
        o16_ref[:, cols] = (acc * scale_b).astype(BF16)

    def kv_b(kv_ref):
        def epilogue(c, cols, acc):
            o16_ref[:, cols] = acc.astype(BF16)
            _store_heads(acc, kv_ref, c * (cw // dh), n_heads, dh)
        return epilogue

    def gate(c, cols, acc):
        o32_ref[:, cols] = jax.nn.sigmoid(acc)

    for jj, epilogue in enumerate((qa, ka, va, qb, kv_b(kb_ref), kv_b(vb_ref))):
        pl.when(j == jj)(functools.partial(sweep, epilogue))
    pl.when(j >= 6)(functools.partial(sweep, gate))


def _proj(x, gain, w_in, cos, sin, qn, kn, shared, kv, li, depth, row0, n_rows, tm,
          seg, n_heads, dh, scale_a, scale_b):
    m, d = x.shape
    nj = w_in.shape[2] // seg
    assert nj == 10 and n_heads * dh == seg and n_rows % tm == 0 and row0 % tm == 0
    blk0 = row0 // tm

    def o32_map(i, j):
        return (blk0 + i, jnp.where(j >= 6, j - 5, 0))

    def o16_map(i, j):
        return (blk0 + i, jnp.minimum(j, 5))

    kern = functools.partial(_proj_kernel, n_heads=n_heads, dh=dh,
                             scale_a=scale_a, scale_b=scale_b)
    kv_spec = pl.BlockSpec((None, tm * n_heads, dh), lambda i, j: (li, i, 0))
    kv_shape = jax.ShapeDtypeStruct((depth, n_rows * n_heads, dh), F32)
    in_specs = [
        pl.BlockSpec((tm, d), lambda i, j: (blk0 + i, 0)),
        pl.BlockSpec((None, 1, d), lambda i, j: (li, 0, 0)),
        pl.BlockSpec((None, d, seg), lambda i, j: (li, 0, j)),
        pl.BlockSpec((tm, dh), lambda i, j: (blk0 + i, 0)),
        pl.BlockSpec((tm, dh), lambda i, j: (blk0 + i, 0)),
        pl.BlockSpec((None, 1, dh), lambda i, j: (li, 0, 0)),
        pl.BlockSpec((None, 1, dh), lambda i, j: (li, 0, 0)),
    ]
    args = [x, gain.reshape(gain.shape[0], 1, d), w_in, cos, sin,
            qn.reshape(qn.shape[0], 1, dh), kn.reshape(kn.shape[0], 1, dh)]
    aliases = {}
    for out0, group in ((0, shared), (2, kv)):
        for k, a in enumerate(group or ()):
            in_specs.append(pl.BlockSpec(memory_space=pl.ANY))
            aliases[len(args)] = out0 + k
            args.append(a)
    return pl.pallas_call(
        kern,
        grid=(n_rows // tm, nj),
        in_specs=in_specs,
        out_specs=[
            pl.BlockSpec((tm, seg), o32_map),
            pl.BlockSpec((tm, seg), o16_map),
            kv_spec, kv_spec, kv_spec,
        ],
        out_shape=[
            jax.ShapeDtypeStruct((m, 5 * seg), F32),
            jax.ShapeDtypeStruct((m, 6 * seg), BF16),
            kv_shape, kv_shape, kv_shape,
        ],
        scratch_shapes=[pltpu.VMEM((tm, d), BF16)],
        input_output_aliases=aliases,
        compiler_params=_params("arbitrary", "arbitrary"),
        name="proj",
    )(*args)


def _lam(lam_ref, lam_init):
    v = lam_ref[...]
    s1 = jnp.sum(v[0:1] * v[1:2], axis=1, keepdims=True)
    s2 = jnp.sum(v[2:3] * v[3:4], axis=1, keepdims=True)
    return jnp.exp(s1) - jnp.exp(s2) + lam_init


def _diff_combine(o0, o1, lam, gsub, lam_init):
    o = o0 - lam * o1
    ms = jnp.mean(o * o, axis=-1, keepdims=True)
    return (o * lax.rsqrt(ms + EPS) * gsub) * (1.0 - lam_init)


def _wide(col, n):
    reps = n // LANES
    return col if reps == 1 else jnp.concatenate([col] * reps, axis=1)


def _sb_logs(zn, mask):
    l = jnp.log(1.0 + jnp.exp2(-jnp.abs(zn))) * LOG2E
    lk = jnp.minimum(zn, 0.0) - l
    lsig = lk - zn
    if mask is not None:
        lk = jnp.where(mask, lk, 0.0)
    return lsig, lk


def _skewed(stages, n):
    for t in range(n + len(stages) - 1):
        for s in reversed(range(len(stages))):
            if 0 <= t - s < n:
                stages[s](t - s)


def _attn_a_kernel(q_ref, k_ref, v_ref, lam_ref, gsub_ref, o_ref, m_scr, l_scr, acc_scr,
                   *, n_sub, dh, dv, lam_init, tk):
    qi = pl.program_id(0)
    tq = q_ref.shape[0]
    m_scr[...] = jnp.full(m_scr.shape, NEG_INF, F32)
    l_scr[...] = jnp.zeros(l_scr.shape, F32)
    acc_scr[...] = jnp.zeros(acc_scr.shape, F32)

    def body(j, masked):
        rows = pl.ds(pl.multiple_of(j * tk, tk), tk)
        if masked:
            off = (qi * tq - j * tk) // CHUNK
            r = lax.broadcasted_iota(jnp.int32, (tq, tk), 0)
            c = lax.broadcasted_iota(jnp.int32, (tq, tk), 1)
            mask = (c // CHUNK - r // CHUNK) <= off
        sc, pb, al = {}, {}, {}

        def scores(hd):
            for h in (2 * hd, 2 * hd + 1):
                q = q_ref[:, h * dh:(h + 1) * dh]
                k = k_ref[rows, h * dh:(h + 1) * dh]
                s = lax.dot_general(q, k, NT_DIMS, preferred_element_type=F32)
                if masked:
                    s = jnp.where(mask, s, NEG_INF)
                sc[h] = s

        def softmax_update(hd):
            for h in (2 * hd, 2 * hd + 1):
                s = sc.pop(h)
                m_prev = m_scr[h]
                m_new = jnp.maximum(m_prev, jnp.max(s, axis=1, keepdims=True))
                alpha = jnp.exp2(m_prev - m_new)
                p = jnp.exp2(s - _wide(m_new, tk))
                l_scr[h] = alpha * l_scr[h] + jnp.sum(p, axis=1, keepdims=True)
                m_scr[h] = m_new
                al[h] = alpha
                pb[h] = p.astype(BF16)

        def values(hd):
            v = v_ref[rows, hd * dv:(hd + 1) * dv]
            p2 = jnp.concatenate([pb.pop(2 * hd), pb.pop(2 * hd + 1)], axis=0)
            pv = jnp.dot(p2, v, preferred_element_type=F32)
            for u in range(2):
                h = 2 * hd + u
                acc_scr[h] = _wide(al.pop(h), dv) * acc_scr[h] + pv[u * tq:(u + 1) * tq]

        _skewed((scores, softmax_update, values), n_sub // 2)

    n_full = (qi * tq) // tk

    def visible(j, carry):
        body(j, False)
        return carry

    lax.fori_loop(0, n_full, visible, 0)
    body(n_full, True)

    lam = _lam(lam_ref, lam_init)
    gsub = gsub_ref[...]
    for hd in range(n_sub // 2):
        o0 = acc_scr[2 * hd] / _wide(l_scr[2 * hd], dv)
        o1 = acc_scr[2 * hd + 1] / _wide(l_scr[2 * hd + 1], dv)
        o = _diff_combine(o0, o1, lam, gsub, lam_init)
        o_ref[:, hd * dv:(hd + 1) * dv] = o.astype(BF16)


def _attn_a_prompt(qkv16, oa, lam_vecs, gsub, li, n_rows, tq, tk, seg, n_sub, dh, dv, lam_init):
    assert tk % tq == 0 and n_rows % tk == 0
    kern = functools.partial(_attn_a_kernel, n_sub=n_sub, dh=dh, dv=dv, lam_init=lam_init,
                             tk=tk)
    resident = pl.Buffered(1)

    def kern_alias(q_ref, k_ref, v_ref, lam_ref, gsub_ref, oa_in_ref, o_ref, *scr):
        del oa_in_ref
        kern(q_ref, k_ref, v_ref, lam_ref, gsub_ref, o_ref, *scr)

    return pl.pallas_call(
        kern_alias,
        grid=(n_rows // tq,),
        in_specs=[
            pl.BlockSpec((tq, seg), lambda i: (i, 0)),
            pl.BlockSpec((n_rows, seg), lambda i: (0, 1), pipeline_mode=resident),
            pl.BlockSpec((n_rows, seg), lambda i: (0, 2), pipeline_mode=resident),
            pl.BlockSpec((None, 4, dh), lambda i: (li, 0, 0)),
            pl.BlockSpec((None, 1, dv), lambda i: (li, 0, 0)),
            pl.BlockSpec(memory_space=pl.ANY),
        ],
        out_specs=pl.BlockSpec((tq, seg), lambda i: (i, 0)),
        out_shape=jax.ShapeDtypeStruct(oa.shape, BF16),
        scratch_shapes=[
            pltpu.VMEM((n_sub, tq, LANES), F32),
            pltpu.VMEM((n_sub, tq, LANES), F32),
            pltpu.VMEM((n_sub, tq, dv), F32),
        ],
        input_output_aliases={5: 0},
        compiler_params=_params("arbitrary"),
        name="attn_a_prompt",
    )(qkv16, qkv16, qkv16, lam_vecs, gsub.reshape(gsub.shape[0], 1, dv), oa)


def _sb_heads(q_ref, k_of, v_of, tri, mask, run_scr, er_scr, acc_scr, n_heads, dh):
    zn, lsig, lkb, er, after = {}, {}, {}, {}, {}

    def logits(h):
        q = q_ref[:, h * dh:(h + 1) * dh]
        zn[h] = lax.dot_general(q, k_of(h), NT_DIMS, preferred_element_type=F32)

    def logs(h):
        lsig[h], lk = _sb_logs(zn.pop(h), mask)
        lkb[h] = lk.astype(BF16)
        run = run_scr[h] + jnp.sum(lk, axis=1, keepdims=True)
        run_scr[h] = run
        er[h] = er_scr[h]
        er_scr[h] = jnp.exp2(run)

    def cumsum(h):
        after[h] = jnp.dot(lkb.pop(h), tri, preferred_element_type=F32)

    def weights_pv(h):
        a = jnp.exp2(lsig.pop(h) + after.pop(h))
        if mask is not None:
            a = jnp.where(mask, a, 0.0)
        pv = jnp.dot(a.astype(BF16), v_of(h), preferred_element_type=F32)
        acc_scr[:, h * dh:(h + 1) * dh] += pv * _wide(er.pop(h), dh)

    _skewed((logits, logs, cumsum, weights_pv), n_heads)
    top = er_scr[0]
    for h in range(1, n_heads):
        top = jnp.maximum(top, er_scr[h])
    return jnp.max(top) > 0.0


def _sb_earlier_blocks(first_block, live, block_fn):
    def cond(carry):
        j, alive = carry
        return (j >= 0) & (alive != 0)

    def step(carry):
        j, _ = carry
        return j - 1, block_fn(j).astype(jnp.int32)

    lax.while_loop(cond, step, (first_block, live.astype(jnp.int32)))


def _attn_b_kernel(q_ref, k_ref, v_ref, tri_ref, o_ref, run_scr, er_scr, acc_scr,
                   *, n_heads, dh):
    qi = pl.program_id(0)
    tq = q_ref.shape[0]
    tk = tri_ref.shape[0]
    tri = tri_ref[...]
    run_scr[...] = jnp.zeros(run_scr.shape, F32)
    er_scr[...] = jnp.ones(er_scr.shape, F32)
    acc_scr[...] = jnp.zeros(acc_scr.shape, F32)

    def block(j, mask):
        rows = pl.ds(pl.multiple_of(j * tk, tk), tk)
        return _sb_heads(q_ref,
                         lambda h: k_ref[rows, h * dh:(h + 1) * dh],
                         lambda h: v_ref[rows, h * dh:(h + 1) * dh],
                         tri, mask, run_scr, er_scr, acc_scr, n_heads, dh)

    r = lax.broadcasted_iota(jnp.int32, (tq, tk), 0)
    c = lax.broadcasted_iota(jnp.int32, (tq, tk), 1)
    live = block(qi, c < r)
    _sb_earlier_blocks(qi - 1, live, lambda j: block(j, None))
    o_ref[...] = acc_scr[...].astype(BF16)


def _attn_b_prompt(qkv16, ob, tri, n_rows, tq, seg, n_heads, dh):
    assert tri.shape[0] == tq
    kern = functools.partial(_attn_b_kernel, n_heads=n_heads, dh=dh)
    resident = pl.Buffered(1)

    def kern_alias(q_ref, k_ref, v_ref, tri_ref, ob_in_ref, o_ref, *scr):
        del ob_in_ref
        kern(q_ref, k_ref, v_ref, tri_ref, o_ref, *scr)

    return pl.pallas_call(
        kern_alias,
        grid=(n_rows // tq,),
        in_specs=[
            pl.BlockSpec((tq, seg), lambda i: (i, 3)),
            pl.BlockSpec((n_rows, seg), lambda i: (0, 4), pipeline_mode=resident),
            pl.BlockSpec((n_rows, seg), lambda i: (0, 5), pipeline_mode=resident),
            pl.BlockSpec((tq, tq), lambda i: (0, 0), pipeline_mode=resident),
            pl.BlockSpec(memory_space=pl.ANY),
        ],
        out_specs=pl.BlockSpec((tq, seg), lambda i: (i, 0)),
        out_shape=jax.ShapeDtypeStruct(ob.shape, BF16),
        scratch_shapes=[
            pltpu.VMEM((n_heads, tq, LANES), F32),
            pltpu.VMEM((n_heads, tq, LANES), F32),
            pltpu.VMEM((tq, seg), F32),
        ],
        input_output_aliases={4: 0},
        compiler_params=_params("arbitrary"),
        name="attn_b_prompt",
    )(qkv16, qkv16, qkv16, tri, ob)


def _pad_rows(x, rows):
    n = x.shape[0]
    if n == rows:
        return x
    return jnp.concatenate([x, jnp.zeros((rows - n, x.shape[1]), x.dtype)], axis=0)


def _attn_a_sample_kernel(q_ref, kn_ref, vn_ref, kp_ref, vp_ref, lam_ref, gsub_ref, o_ref,
                          *, n_sub, dh, dv, lam_init):
    sq = q_ref.shape[0]
    past_len = kp_ref.shape[0] // n_sub
    kn = _pad_rows(kn_ref[...], LANES)
    vn = _pad_rows(vn_ref[...], LANES)
    r = lax.broadcasted_iota(jnp.int32, (sq, LANES), 0)
    c = lax.broadcasted_iota(jnp.int32, (sq, LANES), 1)
    mask_new = ((past_len + c) // CHUNK <= (past_len + r) // CHUNK) & (c < sq)
    lam = _lam(lam_ref, lam_init)
    gsub = gsub_ref[...]
    outs = []
    for h in range(n_sub):
        hs = slice(h * dh, (h + 1) * dh)
        vs = slice((h // 2) * dv, (h // 2 + 1) * dv)
        q = q_ref[:, hs]
        kp = kp_ref[pl.ds(h, past_len, stride=n_sub), :].astype(BF16)
        sp = lax.dot_general(q, kp, NT_DIMS, preferred_element_type=F32)
        sn = lax.dot_general(q, kn[:, hs], NT_DIMS, preferred_element_type=F32)
        sn = jnp.where(mask_new, sn, NEG_INF)
        m = jnp.maximum(jnp.max(sp, axis=1, keepdims=True), jnp.max(sn, axis=1, keepdims=True))
        pp = jnp.exp2(sp - m)
        pn = jnp.exp2(sn - m)
        l = jnp.sum(pp, axis=1, keepdims=True) + jnp.sum(pn, axis=1, keepdims=True)
        o = (jnp.dot(pp.astype(BF16), vp_ref[:, vs], preferred_element_type=F32)
             + jnp.dot(pn.astype(BF16), vn[:, vs], preferred_element_type=F32))
        outs.append(o / l)
    for hd in range(n_sub // 2):
        o = _diff_combine(outs[2 * hd], outs[2 * hd + 1], lam, gsub, lam_init)
        o_ref[:, hd * dv:(hd + 1) * dv] = o.astype(BF16)


def _attn_a_sample(qkv16, oa, cache_k, cache_v, li, lam_vecs, gsub, row0, sq, seg,
                   n_sub, dh, dv, lam_init):
    _, nb, past_len, _ = cache_v.shape
    blk0 = row0 // sq
    kern = functools.partial(_attn_a_sample_kernel, n_sub=n_sub, dh=dh, dv=dv,
                             lam_init=lam_init)

    def kern_alias(q_ref, kn_ref, vn_ref, kp_ref, vp_ref, lam_ref, gsub_ref, oa_in_ref, o_ref):
        del oa_in_ref
        kern(q_ref, kn_ref, vn_ref, kp_ref, vp_ref, lam_ref, gsub_ref, o_ref)

    return pl.pallas_call(
        kern_alias,
        grid=(nb,),
        in_specs=[
            pl.BlockSpec((sq, seg), lambda b: (blk0 + b, 0)),
            pl.BlockSpec((sq, seg), lambda b: (blk0 + b, 1)),
            pl.BlockSpec((sq, seg), lambda b: (blk0 + b, 2)),
            pl.BlockSpec((None, None, past_len * n_sub, dh), lambda b: (li, b, 0, 0)),
            pl.BlockSpec((None, None, past_len, seg), lambda b: (li, b, 0, 0)),
            pl.BlockSpec((None, 4, dh), lambda b: (li, 0, 0)),
            pl.BlockSpec((None, 1, dv), lambda b: (li, 0, 0)),
            pl.BlockSpec(memory_space=pl.ANY),
        ],
        out_specs=pl.BlockSpec((sq, seg), lambda b: (blk0 + b, 0)),
        out_shape=jax.ShapeDtypeStruct(oa.shape, BF16),
        input_output_aliases={7: 0},
        compiler_params=_params("arbitrary"),
        name="attn_a_sample",
    )(qkv16, qkv16, qkv16, cache_k, cache_v, lam_vecs, gsub.reshape(gsub.shape[0], 1, dv), oa)


def _attn_b_sample_kernel(q_ref, kn_ref, vn_ref, kp_ref, vp_ref, tri_ref, o_ref,
                          run_scr, er_scr, acc_scr, *, n_heads, dh):
    sq = q_ref.shape[0]
    past_len = kp_ref.shape[0] // n_heads
    tk = tri_ref.shape[0]
    run_scr[...] = jnp.zeros(run_scr.shape, F32)
    er_scr[...] = jnp.ones(er_scr.shape, F32)
    acc_scr[...] = jnp.zeros(acc_scr.shape, F32)
    kn = _pad_rows(kn_ref[...], LANES)
    vn = _pad_rows(vn_ref[...], LANES)
    r = lax.broadcasted_iota(jnp.int32, (sq, LANES), 0)
    c = lax.broadcasted_iota(jnp.int32, (sq, LANES), 1)
    live = _sb_heads(q_ref,
                     lambda h: kn[:, h * dh:(h + 1) * dh],
                     lambda h: vn[:, h * dh:(h + 1) * dh],
                     tri_ref[0:LANES, 0:LANES], c < r, run_scr, er_scr, acc_scr, n_heads, dh)

    def block(j):
        row0 = pl.multiple_of(j * (tk * n_heads), tk * n_heads)

        def head_rows(ref, h):
            return ref[pl.ds(row0 + h, tk, stride=n_heads), :].astype(BF16)

        return _sb_heads(q_ref,
                         lambda h: head_rows(kp_ref, h),
                         lambda h: head_rows(vp_ref, h),
                         tri_ref[...], None, run_scr, er_scr, acc_scr, n_heads, dh)

    _sb_earlier_blocks(past_len // tk - 1, live, block)
    o_ref[...] = acc_scr[...].astype(BF16)


def _attn_b_sample(qkv16, ob, cache_k, cache_v, li, tri, row0, sq, seg, n_heads, dh):
    _, nb, cache_rows, _ = cache_k.shape
    past_len = cache_rows // n_heads
    tk = tri.shape[1]
    assert past_len % tk == 0 and tk >= LANES
    blk0 = row0 // sq
    kern = functools.partial(_attn_b_sample_kernel, n_heads=n_heads, dh=dh)

    def kern_alias(q_ref, kn_ref, vn_ref, kp_ref, vp_ref, tri_ref, ob_in_ref, o_ref, *scr):
        del ob_in_ref
        kern(q_ref, kn_ref, vn_ref, kp_ref, vp_ref, tri_ref, o_ref, *scr)

    return pl.pallas_call(
        kern_alias,
        grid=(nb,),
        in_specs=[
            pl.BlockSpec((sq, seg), lambda b: (blk0 + b, 3)),
            pl.BlockSpec((sq, seg), lambda b: (blk0 + b, 4)),
            pl.BlockSpec((sq, seg), lambda b: (blk0 + b, 5)),
            pl.BlockSpec((None, None, past_len * n_heads, dh), lambda b: (li, b, 0, 0)),
            pl.BlockSpec((None, None, past_len * n_heads, dh), lambda b: (li, b, 0, 0)),
            pl.BlockSpec((tk, tk), lambda b: (0, 0)),
            pl.BlockSpec(memory_space=pl.ANY),
        ],
        out_specs=pl.BlockSpec((sq, seg), lambda b: (blk0 + b, 0)),
        out_shape=jax.ShapeDtypeStruct(ob.shape, BF16),
        scratch_shapes=[
            pltpu.VMEM((n_heads, sq, LANES), F32),
            pltpu.VMEM((n_heads, sq, LANES), F32),
            pltpu.VMEM((sq, seg), F32),
        ],
        input_output_aliases={6: 0},
        compiler_params=_params("arbitrary"),
        name="attn_b_sample",
    )(qkv16, qkv16, qkv16, cache_k, cache_v, tri, ob)


def _merge_kernel(x_ref, oa_ref, ob_ref, ga_ref, gb_ref, wba_ref, wbb_ref, wo_ref, o_ref):
    @pl.when(pl.program_id(1) == 0)
    def _():
        o_ref[...] = x_ref[...]

    ta = jnp.dot(oa_ref[...], wba_ref[...].astype(BF16), preferred_element_type=F32)
    tb = jnp.dot(ob_ref[...], wbb_ref[...].astype(BF16), preferred_element_type=F32)
    merged = ga_ref[...] * ta + gb_ref[...] * tb
    o_ref[...] += jnp.dot(merged.astype(BF16), wo_ref[...].astype(BF16),
                          preferred_element_type=F32)


def _merge(x, oa, ob, proj32, wba, wbb, wo, li, tm, tn, gate_col0):
    m, d = x.shape
    ka = oa.shape[1]
    kb = ob.shape[1]
    nj = d // tn
    ga0 = gate_col0 // tn
    gb0 = ga0 + nj
    return pl.pallas_call(
        _merge_kernel,
        grid=(m // tm, nj),
        in_specs=[
            pl.BlockSpec((tm, d), lambda i, j: (i, 0)),
            pl.BlockSpec((tm, ka), lambda i, j: (i, 0)),
            pl.BlockSpec((tm, kb), lambda i, j: (i, 0)),
            pl.BlockSpec((tm, tn), lambda i, j: (i, ga0 + j)),
            pl.BlockSpec((tm, tn), lambda i, j: (i, gb0 + j)),
            pl.BlockSpec((None, ka, tn), lambda i, j: (li, 0, j)),
            pl.BlockSpec((None, kb, tn), lambda i, j: (li, 0, j)),
            pl.BlockSpec((None, tn, d), lambda i, j: (li, j, 0)),
        ],
        out_specs=pl.BlockSpec((tm, d), lambda i, j: (i, 0)),
        out_shape=jax.ShapeDtypeStruct((m, d), F32),
        compiler_params=_params("parallel", "arbitrary"),
        name="merge",
    )(x, oa, ob, proj32, proj32, wba, wbb, wo)


def _rope_tables(pos, dh):
    inv = 1.0 / (ROPE_THETA ** (jnp.arange(0, dh, 2, dtype=F32) / dh))
    ang = pos.astype(F32)[:, None] * inv[None, :]
    cos = jnp.cos(ang)
    sin = jnp.sin(ang)
    return jnp.concatenate([cos, cos], axis=1), jnp.concatenate([-sin, sin], axis=1)


def kernel(x_prompt, x_sample, cache_a_k, cache_a_v, cache_b_k, cache_b_v,
           ffn1_norm, ffn1_w_gate, ffn1_w_up, ffn1_w_down,
           mix_norm, w_in, a_q_norm, a_k_norm,
           a_lam_q1, a_lam_k1, a_lam_q2, a_lam_k2, a_sub_norm,
           w_branch_a, w_branch_b, w_out,
           ffn2_norm, ffn2_w_gate, ffn2_w_up, ffn2_w_down):
    bp, sp, d = x_prompt.shape
    bs, ss, _ = x_sample.shape
    depth, _, past_len, n_sub, dh_a = cache_a_k.shape
    h_a, dv_a = cache_a_v.shape[3:]
    h_b, dh_b = cache_b_k.shape[3:]
    seg = n_sub * dh_a
    assert bp == 1 and h_a * dv_a == seg and h_b * dh_b == seg and n_sub == 2 * h_a
    assert w_in.shape[2] == 6 * seg + 2 * d and d == 2 * seg

    n_p = bp * sp
    m = n_p + bs * ss
    tm = _pick(m, (768, 512, 384, 256, 128, 64, 32))
    tf = _pick(ffn1_w_gate.shape[2], (512, 256, 128))
    tn = _pick(d, (512, 256, 128))
    tm_p = _pick(n_p, (512, 256, 128, 64, 32))
    assert n_p % (m - n_p) == 0
    tq_a, tk_a = _pick(sp, (256, 128)), _pick(sp, (512, 256, 128))
    tq_b = tk_b = _pick(math.gcd(sp, past_len), (256, 128))
    assert n_p % ss == 0 and min(tq_a, tk_a, tq_b) % CHUNK == 0

    x = jnp.concatenate([x_prompt.reshape(n_p, d), x_sample.reshape(bs * ss, d)], axis=0)
    pos = jnp.concatenate([jnp.arange(sp)] * bp + [past_len + jnp.arange(ss)] * bs)
    cos, sin = _rope_tables(pos, dh_a)
    tri = jnp.asarray(np.tril(np.ones((tk_b, tk_b), np.float32), -1), BF16)
    lam_vecs = jnp.stack([a_lam_q1, a_lam_k1, a_lam_q2, a_lam_k2], axis=1)

    bf = lambda w: w.astype(BF16)
    f1g, f1u, f1d = ffn1_w_gate, ffn1_w_up, ffn1_w_down
    f2g, f2u, f2d = ffn2_w_gate, ffn2_w_up, ffn2_w_down
    w_in16 = bf(w_in)
    rows_of = lambda c: c.reshape(depth, bs, past_len * c.shape[3], c.shape[4])
    ck_a, ck_b, cv_b = rows_of(cache_a_k), rows_of(cache_b_k), rows_of(cache_b_v)
    cv_a = bf(cache_a_v.reshape(depth, bs, past_len, seg))
    scales = (dh_a ** -0.5 * LOG2E, -(dh_b ** -0.5) * LOG2E)

    rows, kv_p, kv_s = [], None, None
    for li in range(depth):
        lam_init = 0.8 - 0.6 * math.exp(-0.3 * li)

        x = _ffn(x, ffn1_norm, f1g, f1u, f1d, li, tm, tf)
        proj32, qkv16, *kv_p = _proj(x, mix_norm, w_in16, cos, sin, a_q_norm, a_k_norm,
                                     None, kv_p, li, depth, 0, n_p, tm_p, seg, n_sub, dh_a,
                                     *scales)
        proj32, qkv16, *kv_s = _proj(x, mix_norm, w_in16, cos, sin, a_q_norm, a_k_norm,
                                     (proj32, qkv16), kv_s, li, depth, n_p, m - n_p, m - n_p,
                                     seg, n_sub, dh_a, *scales)

        oa = jnp.zeros((m, seg), BF16)
        oa = _attn_a_prompt(qkv16, oa, lam_vecs, a_sub_norm, li, n_p, tq_a, tk_a, seg,
                            n_sub, dh_a, dv_a, lam_init)
        oa = _attn_a_sample(qkv16, oa, ck_a, cv_a, li, lam_vecs, a_sub_norm, n_p, ss,
                            seg, n_sub, dh_a, dv_a, lam_init)
        ob = jnp.zeros((m, seg), BF16)
        ob = _attn_b_prompt(qkv16, ob, tri, n_p, tq_b, seg, h_b, dh_b)
        ob = _attn_b_sample(qkv16, ob, ck_b, cv_b, li, tri, n_p, ss, seg, h_b, dh_b)

        x = _merge(x, oa, ob, proj32, w_branch_a, w_branch_b, w_out, li, tm, tn, seg)
        x = _ffn(x, ffn2_norm, f2g, f2u, f2d, li, tm, tf)
        rows.append(proj32)

    av_p = jnp.stack([r[:n_p, :seg] for r in rows], axis=0).reshape(depth, bp, sp, h_a, dv_a)
    av_s = jnp.stack([r[n_p:, :seg] for r in rows], axis=0).reshape(depth, bs, ss, h_a, dv_a)
    (ak_p, bk_p, bv_p), (ak_s, bk_s, bv_s) = kv_p, kv_s
    prompt = lambda a, nh, dh: a.reshape(depth, bp, sp, nh, dh)
    sample = lambda a, nh, dh: a.reshape(depth, bs, ss, nh, dh)
    ak_p, ak_s = prompt(ak_p, n_sub, dh_a), sample(ak_s, n_sub, dh_a)
    bk_p, bk_s = prompt(bk_p, h_b, dh_b), sample(bk_s, h_b, dh_b)
    bv_p, bv_s = prompt(bv_p, h_b, dh_b), sample(bv_s, h_b, dh_b)
    return (x[:n_p].reshape(bp, sp, d), x[n_p:].reshape(bs, ss, d),
            ak_p, av_p, bk_p, bv_p, ak_s, av_s, bk_s, bv_s)
```

```python
import functools
import math

import numpy as np
import jax
import jax.numpy as jnp
from jax import lax
from jax.experimental import pallas as pl
from jax.experimental.pallas import tpu as pltpu

F32 = jnp.float32
BF16 = jnp.bfloat16

CHUNK = 64
ROPE_THETA = 10000.0
EPS = 1e-6
NEG_INF = -1e30
LOG2E = math.log2(math.e)
LANES = 128
VMEM_LIMIT = 56 * 1024 * 1024

NT_DIMS = (((1,), (1,)), ((), ()))


def _pick(n, candidates):
    for c in candidates:
        if n % c == 0:
            return c
    return n


def _params(*sem):
    return pltpu.CompilerParams(dimension_semantics=sem, vmem_limit_bytes=VMEM_LIMIT)


def _rmsnorm_rows(x_ref, g_ref, h_ref, copy_ref=None):
    rows = _pick(x_ref.shape[0], (64, 32, 16))
    g = g_ref[...]

    def chunk(c, carry):
        r = pl.ds(pl.multiple_of(c * rows, rows), rows)
        x = x_ref[r, :]
        ms = jnp.mean(x * x, axis=-1, keepdims=True)
        h_ref[r, :] = (x * lax.rsqrt(ms + EPS) * g).astype(BF16)
        if copy_ref is not None:
            copy_ref[r, :] = x
        return carry

    lax.fori_loop(0, x_ref.shape[0] // rows, chunk, 0)


def _ffn_kernel(x_ref, g_ref, wg_ref, wu_ref, wd_ref, o_ref, h_ref):
    @pl.when(pl.program_id(1) == 0)
    def _():
        _rmsnorm_rows(x_ref, g_ref, h_ref, copy_ref=o_ref)

    h = h_ref[...]
    g = jnp.dot(h, wg_ref[...].astype(BF16), preferred_element_type=F32)
    u = jnp.dot(h, wu_ref[...].astype(BF16), preferred_element_type=F32)
    a = (0.5 * g) * jax.nn.sigmoid(g) * u
    o_ref[...] += jnp.dot(a.astype(BF16), wd_ref[...].astype(BF16),
                          preferred_element_type=F32)


def _ffn(x, gain, wg, wu, wd, li, tm, tf):
    m, d = x.shape
    f = wg.shape[2]
    return pl.pallas_call(
        _ffn_kernel,
        grid=(m // tm, f // tf),
        in_specs=[
            pl.BlockSpec((tm, d), lambda i, j: (i, 0)),
            pl.BlockSpec((None, 1, d), lambda i, j: (li, 0, 0)),
            pl.BlockSpec((None, d, tf), lambda i, j: (li, 0, j)),
            pl.BlockSpec((None, d, tf), lambda i, j: (li, 0, j)),
            pl.BlockSpec((None, tf, d), lambda i, j: (li, j, 0)),
        ],
        out_specs=pl.BlockSpec((tm, d), lambda i, j: (i, 0)),
        out_shape=jax.ShapeDtypeStruct((m, d), F32),
        scratch_shapes=[pltpu.VMEM((tm, d), BF16)],
        compiler_params=_params("parallel", "arbitrary"),
        name="ffn",
    )(x, gain.reshape(gain.shape[0], 1, d), wg, wu, wd)


def _store_heads(y, kv_ref, h0, n_heads, dh):
    for hh in range(y.shape[1] // dh):
        kv_ref[pl.ds(h0 + hh, y.shape[0], stride=n_heads), :] = y[:, hh * dh:(hh + 1) * dh]


def _proj_kernel(x_ref, g_ref, w_ref, cos_ref, sin_ref, qn_ref, kn_ref, *refs,
                 n_heads, dh, scale_a, scale_b):
    o32_ref, o16_ref, ka_ref, kb_ref, vb_ref, h_ref = refs[-6:]
    j = pl.program_id(1)
    seg = w_ref.shape[1]
    cw = 2 * dh

    @pl.when(j == 0)
    def _():
        _rmsnorm_rows(x_ref, g_ref, h_ref)

    def sweep(epilogue):
        accs = {}

        def matmul(c):
            accs[c] = jnp.dot(h_ref[...], w_ref[:, c * cw:(c + 1) * cw],
                              preferred_element_type=F32)

        def finish(c):
            epilogue(c, slice(c * cw, (c + 1) * cw), accs.pop(c))

        _skewed((matmul, finish), seg // cw)

    def norm_rope(acc, gain_ref, scale):
        cos = cos_ref[...]
        sin = sin_ref[...]
        outs = []
        for hh in range(acc.shape[1] // dh):
            xh = acc[:, hh * dh:(hh + 1) * dh]
            ms = jnp.mean(xh * xh, axis=-1, keepdims=True)
            y = xh * lax.rsqrt(ms + EPS) * gain_ref[...]
            y = y * cos + pltpu.roll(y, dh // 2, 1) * sin
            if scale is not None:
                y = y * scale
            outs.append(y)
        return jnp.concatenate(outs, axis=1)

    def qa(c, cols, acc):
        o16_ref[:, cols] = norm_rope(acc, qn_ref, scale_a).astype(BF16)

    def ka(c, cols, acc):
        y = norm_rope(acc, kn_ref, None)
        o16_ref[:, cols] = y.astype(BF16)
        _store_heads(y, ka_ref, c * (cw // dh), n_heads, dh)

    def va(c, cols, acc):
        o32_ref[:, cols] = acc
        o16_ref[:, cols] = acc.astype(BF16)

    def qb(c, cols, acc):
        o16_ref[:, cols] = (acc * scale_b).astype(BF16)

    def kv_b(kv_ref):
        def epilogue(c, cols, acc):
            o16_ref[:, cols] = acc.astype(BF16)
            _store_heads(acc, kv_ref, c * (cw // dh), n_heads, dh)
        return epilogue

    def gate(c, cols, acc):
        o32_ref[:, cols] = jax.nn.sigmoid(acc)

    for jj, epilogue in enumerate((qa, ka, va, qb, kv_b(kb_ref), kv_b(vb_ref))):
        pl.when(j == jj)(functools.partial(sweep, epilogue))
    pl.when(j >= 6)(functools.partial(sweep, gate))


def _proj(x, gain, w_in, cos, sin, qn, kn, shared, kv, li, depth, row0, n_rows, tm,
          seg, n_heads, dh, scale_a, scale_b):
    m, d = x.shape
    nj = w_in.shape[2] // seg
    assert nj == 10 and n_heads * dh == seg and n_rows % tm == 0 and row0 % tm == 0
    blk0 = row0 // tm

    def o32_map(i, j):
        return (blk0 + i, jnp.where(j >= 6, j - 5, 0))

    def o16_map(i, j):
        return (blk0 + i, jnp.minimum(j, 5))

    kern = functools.partial(_proj_kernel, n_heads=n_heads, dh=dh,
                             scale_a=scale_a, scale_b=scale_b)
    kv_spec = pl.BlockSpec((None, tm * n_heads, dh), lambda i, j: (li, i, 0))
    kv_shape = jax.ShapeDtypeStruct((depth, n_rows * n_heads, dh), F32)
    in_specs = [
        pl.BlockSpec((tm, d), lambda i, j: (blk0 + i, 0)),
        pl.BlockSpec((None, 1, d), lambda i, j: (li, 0, 0)),
        pl.BlockSpec((None, d, seg), lambda i, j: (li, 0, j)),
        pl.BlockSpec((tm, dh), lambda i, j: (blk0 + i, 0)),
        pl.BlockSpec((tm, dh), lambda i, j: (blk0 + i, 0)),
        pl.BlockSpec((None, 1, dh), lambda i, j: (li, 0, 0)),
        pl.BlockSpec((None, 1, dh), lambda i, j: (li, 0, 0)),
    ]
    args = [x, gain.reshape(gain.shape[0], 1, d), w_in, cos, sin,
            qn.reshape(qn.shape[0], 1, dh), kn.reshape(kn.shape[0], 1, dh)]
    aliases = {}
    for out0, group in ((0, shared), (2, kv)):
        for k, a in enumerate(group or ()):
            in_specs.append(pl.BlockSpec(memory_space=pl.ANY))
            aliases[len(args)] = out0 + k
            args.append(a)
    return pl.pallas_call(
        kern,
        grid=(n_rows // tm, nj),
        in_specs=in_specs,
        out_specs=[
            pl.BlockSpec((tm, seg), o32_map),
            pl.BlockSpec((tm, seg), o16_map),
            kv_spec, kv_spec, kv_spec,
        ],
        out_shape=[
            jax.ShapeDtypeStruct((m, 5 * seg), F32),
            jax.ShapeDtypeStruct((m, 6 * seg), BF16),
            kv_shape, kv_shape, kv_shape,
        ],
        scratch_shapes=[pltpu.VMEM((tm, d), BF16)],
        input_output_aliases=aliases,
        compiler_params=_params("arbitrary", "arbitrary"),
        name="proj",
    )(*args)


def _lam(lam_ref, lam_init):
    v = lam_ref[...]
    s1 = jnp.sum(v[0:1] * v[1:2], axis=1, keepdims=True)
    s2 = jnp.sum(v[2:3] * v[3:4], axis=1, keepdims=True)
    return jnp.exp(s1) - jnp.exp(s2) + lam_init


def _diff_combine(o0, o1, lam, gsub, lam_init):
    o = o0 - lam * o1
    ms = jnp.mean(o * o, axis=-1, keepdims=True)
    return (o * lax.rsqrt(ms + EPS) * gsub) * (1.0 - lam_init)


def _wide(col, n):
    reps = n // LANES
    return col if reps == 1 else jnp.concatenate([col] * reps, axis=1)


def _sb_logs(zn, mask):
    l = jnp.log(1.0 + jnp.exp2(-jnp.abs(zn))) * LOG2E
    lk = jnp.minimum(zn, 0.0) - l
    lsig = lk - zn
    if mask is not None:
        lk = jnp.where(mask, lk, 0.0)
    return lsig, lk


def _skewed(stages, n):
    for t in range(n + len(stages) - 1):
        for s in reversed(range(len(stages))):
            if 0 <= t - s < n:
                stages[s](t - s)


def _attn_a_kernel(q_ref, k_ref, v_ref, lam_ref, gsub_ref, o_ref, m_scr, l_scr, acc_scr,
                   *, n_sub, dh, dv, lam_init, tk):
    qi = pl.program_id(0)
    tq = q_ref.shape[0]
    m_scr[...] = jnp.full(m_scr.shape, NEG_INF, F32)
    l_scr[...] = jnp.zeros(l_scr.shape, F32)
    acc_scr[...] = jnp.zeros(acc_scr.shape, F32)

    def body(j, masked):
        rows = pl.ds(pl.multiple_of(j * tk, tk), tk)
        if masked:
            off = (qi * tq - j * tk) // CHUNK
            r = lax.broadcasted_iota(jnp.int32, (tq, tk), 0)
            c = lax.broadcasted_iota(jnp.int32, (tq, tk), 1)
            mask = (c // CHUNK - r // CHUNK) <= off
        sc, pb, al = {}, {}, {}

        def scores(hd):
            for h in (2 * hd, 2 * hd + 1):
                q = q_ref[:, h * dh:(h + 1) * dh]
                k = k_ref[rows, h * dh:(h + 1) * dh]
                s = lax.dot_general(q, k, NT_DIMS, preferred_element_type=F32)
                if masked:
                    s = jnp.where(mask, s, NEG_INF)
                sc[h] = s

        def softmax_update(hd):
            for h in (2 * hd, 2 * hd + 1):
                s = sc.pop(h)
                m_prev = m_scr[h]
                m_new = jnp.maximum(m_prev, jnp.max(s, axis=1, keepdims=True))
                alpha = jnp.exp2(m_prev - m_new)
                p = jnp.exp2(s - _wide(m_new, tk))
                l_scr[h] = alpha * l_scr[h] + jnp.sum(p, axis=1, keepdims=True)
                m_scr[h] = m_new
                al[h] = alpha
                pb[h] = p.astype(BF16)

        def values(hd):
            v = v_ref[rows, hd * dv:(hd + 1) * dv]
            p2 = jnp.concatenate([pb.pop(2 * hd), pb.pop(2 * hd + 1)], axis=0)
            pv = jnp.dot(p2, v, preferred_element_type=F32)
            for u in range(2):
                h = 2 * hd + u
                acc_scr[h] = _wide(al.pop(h), dv) * acc_scr[h] + pv[u * tq:(u + 1) * tq]

        _skewed((scores, softmax_update, values), n_sub // 2)

    n_full = (qi * tq) // tk

    def visible(j, carry):
        body(j, False)
        return carry

    lax.fori_loop(0, n_full, visible, 0)
    body(n_full, True)

    lam = _lam(lam_ref, lam_init)
    gsub = gsub_ref[...]
    for hd in range(n_sub // 2):
        o0 = acc_scr[2 * hd] / _wide(l_scr[2 * hd], dv)
        o1 = acc_scr[2 * hd + 1] / _wide(l_scr[2 * hd + 1], dv)
        o = _diff_combine(o0, o1, lam, gsub, lam_init)
        o_ref[:, hd * dv:(hd + 1) * dv] = o.astype(BF16)


def _attn_a_prompt(qkv16, oa, lam_vecs, gsub, li, n_rows, tq, tk, seg, n_sub, dh, dv, lam_init):
    assert tk % tq == 0 and n_rows % tk == 0
    kern = functools.partial(_attn_a_kernel, n_sub=n_sub, dh=dh, dv=dv, lam_init=lam_init,
                             tk=tk)
    resident = pl.Buffered(1)

    def kern_alias(q_ref, k_ref, v_ref, lam_ref, gsub_ref, oa_in_ref, o_ref, *scr):
        del oa_in_ref
        kern(q_ref, k_ref, v_ref, lam_ref, gsub_ref, o_ref, *scr)

    return pl.pallas_call(
        kern_alias,
        grid=(n_rows // tq,),
        in_specs=[
            pl.BlockSpec((tq, seg), lambda i: (i, 0)),
            pl.BlockSpec((n_rows, seg), lambda i: (0, 1), pipeline_mode=resident),
            pl.BlockSpec((n_rows, seg), lambda i: (0, 2), pipeline_mode=resident),
            pl.BlockSpec((None, 4, dh), lambda i: (li, 0, 0)),
            pl.BlockSpec((None, 1, dv), lambda i: (li, 0, 0)),
            pl.BlockSpec(memory_space=pl.ANY),
        ],
        out_specs=pl.BlockSpec((tq, seg), lambda i: (i, 0)),
        out_shape=jax.ShapeDtypeStruct(oa.shape, BF16),
        scratch_shapes=[
            pltpu.VMEM((n_sub, tq, LANES), F32),
            pltpu.VMEM((n_sub, tq, LANES), F32),
            pltpu.VMEM((n_sub, tq, dv), F32),
        ],
        input_output_aliases={5: 0},
        compiler_params=_params("arbitrary"),
        name="attn_a_prompt",
    )(qkv16, qkv16, qkv16, lam_vecs, gsub.reshape(gsub.shape[0], 1, dv), oa)


def _sb_heads(q_ref, k_of, v_of, tri, mask, run_scr, er_scr, acc_scr, n_heads, dh):
    zn, lsig, lkb, er, after = {}, {}, {}, {}, {}

    def logits(h):
        q = q_ref[:, h * dh:(h + 1) * dh]
        zn[h] = lax.dot_general(q, k_of(h), NT_DIMS, preferred_element_type=F32)

    def logs(h):
        lsig[h], lk = _sb_logs(zn.pop(h), mask)
        lkb[h] = lk.astype(BF16)
        run = run_scr[h] + jnp.sum(lk, axis=1, keepdims=True)
        run_scr[h] = run
        er[h] = er_scr[h]
        er_scr[h] = jnp.exp2(run)

    def cumsum(h):
        after[h] = jnp.dot(lkb.pop(h), tri, preferred_element_type=F32)

    def weights_pv(h):
        a = jnp.exp2(lsig.pop(h) + after.pop(h))
        if mask is not None:
            a = jnp.where(mask, a, 0.0)
        pv = jnp.dot(a.astype(BF16), v_of(h), preferred_element_type=F32)
        acc_scr[:, h * dh:(h + 1) * dh] += pv * _wide(er.pop(h), dh)

    _skewed((logits, logs, cumsum, weights_pv), n_heads)
    top = er_scr[0]
    for h in range(1, n_heads):
        top = jnp.maximum(top, er_scr[h])
    return jnp.max(top) > 0.0


def _sb_earlier_blocks(first_block, live, block_fn):
    def cond(carry):
        j, alive = carry
        return (j >= 0) & (alive != 0)

    def step(carry):
        j, _ = carry
        return j - 1, block_fn(j).astype(jnp.int32)

    lax.while_loop(cond, step, (first_block, live.astype(jnp.int32)))


def _attn_b_kernel(q_ref, k_ref, v_ref, tri_ref, o_ref, run_scr, er_scr, acc_scr,
                   *, n_heads, dh):
    qi = pl.program_id(0)
    tq = q_ref.shape[0]
    tk = tri_ref.shape[0]
    tri = tri_ref[...]
    run_scr[...] = jnp.zeros(run_scr.shape, F32)
    er_scr[...] = jnp.ones(er_scr.shape, F32)
    acc_scr[...] = jnp.zeros(acc_scr.shape, F32)

    def block(j, mask):
        rows = pl.ds(pl.multiple_of(j * tk, tk), tk)
        return _sb_heads(q_ref,
                         lambda h: k_ref[rows, h * dh:(h + 1) * dh],
                         lambda h: v_ref[rows, h * dh:(h + 1) * dh],
                         tri, mask, run_scr, er_scr, acc_scr, n_heads, dh)

    r = lax.broadcasted_iota(jnp.int32, (tq, tk), 0)
    c = lax.broadcasted_iota(jnp.int32, (tq, tk), 1)
    live = block(qi, c < r)
    _sb_earlier_blocks(qi - 1, live, lambda j: block(j, None))
    o_ref[...] = acc_scr[...].astype(BF16)


def _attn_b_prompt(qkv16, ob, tri, n_rows, tq, seg, n_heads, dh):
    assert tri.shape[0] == tq
    kern = functools.partial(_attn_b_kernel, n_heads=n_heads, dh=dh)
    resident = pl.Buffered(1)

    def kern_alias(q_ref, k_ref, v_ref, tri_ref, ob_in_ref, o_ref, *scr):
        del ob_in_ref
        kern(q_ref, k_ref, v_ref, tri_ref, o_ref, *scr)

    return pl.pallas_call(
        kern_alias,
        grid=(n_rows // tq,),
        in_specs=[
            pl.BlockSpec((tq, seg), lambda i: (i, 3)),
            pl.BlockSpec((n_rows, seg), lambda i: (0, 4), pipeline_mode=resident),
            pl.BlockSpec((n_rows, seg), lambda i: (0, 5), pipeline_mode=resident),
            pl.BlockSpec((tq, tq), lambda i: (0, 0), pipeline_mode=resident),
            pl.BlockSpec(memory_space=pl.ANY),
        ],
        out_specs=pl.BlockSpec((tq, seg), lambda i: (i, 0)),
        out_shape=jax.ShapeDtypeStruct(ob.shape, BF16),
        scratch_shapes=[
            pltpu.VMEM((n_heads, tq, LANES), F32),
            pltpu.VMEM((n_heads, tq, LANES), F32),
            pltpu.VMEM((tq, seg), F32),
        ],
        input_output_aliases={4: 0},
        compiler_params=_params("arbitrary"),
        name="attn_b_prompt",
    )(qkv16, qkv16, qkv16, tri, ob)


def _pad_rows(x, rows):
    n = x.shape[0]
    if n == rows:
        return x
    return jnp.concatenate([x, jnp.zeros((rows - n, x.shape[1]), x.dtype)], axis=0)


def _attn_a_sample_kernel(q_ref, kn_ref, vn_ref, kp_ref, vp_ref, lam_ref, gsub_ref, o_ref,
                          *, n_sub, dh, dv, lam_init):
    sq = q_ref.shape[0]
    past_len = kp_ref.shape[0] // n_sub
    kn = _pad_rows(kn_ref[...], LANES)
    vn = _pad_rows(vn_ref[...], LANES)
    r = lax.broadcasted_iota(jnp.int32, (sq, LANES), 0)
    c = lax.broadcasted_iota(jnp.int32, (sq, LANES), 1)
    mask_new = ((past_len + c) // CHUNK <= (past_len + r) // CHUNK) & (c < sq)
    lam = _lam(lam_ref, lam_init)
    gsub = gsub_ref[...]
    outs = []
    for h in range(n_sub):
        hs = slice(h * dh, (h + 1) * dh)
        vs = slice((h // 2) * dv, (h // 2 + 1) * dv)
        q = q_ref[:, hs]
        kp = kp_ref[pl.ds(h, past_len, stride=n_sub), :].astype(BF16)
        sp = lax.dot_general(q, kp, NT_DIMS, preferred_element_type=F32)
        sn = lax.dot_general(q, kn[:, hs], NT_DIMS, preferred_element_type=F32)
        sn = jnp.where(mask_new, sn, NEG_INF)
        m = jnp.maximum(jnp.max(sp, axis=1, keepdims=True), jnp.max(sn, axis=1, keepdims=True))
        pp = jnp.exp2(sp - m)
        pn = jnp.exp2(sn - m)
        l = jnp.sum(pp, axis=1, keepdims=True) + jnp.sum(pn, axis=1, keepdims=True)
        o = (jnp.dot(pp.astype(BF16), vp_ref[:, vs].astype(BF16), preferred_element_type=F32)
             + jnp.dot(pn.astype(BF16), vn[:, vs], preferred_element_type=F32))
        outs.append(o / l)
    for hd in range(n_sub // 2):
        o = _diff_combine(outs[2 * hd], outs[2 * hd + 1], lam, gsub, lam_init)
        o_ref[:, hd * dv:(hd + 1) * dv] = o.astype(BF16)


def _attn_a_sample(qkv16, oa, cache_k, cache_v, li, lam_vecs, gsub, row0, sq, seg,
                   n_sub, dh, dv, lam_init):
    _, nb, past_len, _ = cache_v.shape
    blk0 = row0 // sq
    kern = functools.partial(_attn_a_sample_kernel, n_sub=n_sub, dh=dh, dv=dv,
                             lam_init=lam_init)

    def kern_alias(q_ref, kn_ref, vn_ref, kp_ref, vp_ref, lam_ref, gsub_ref, oa_in_ref, o_ref):
        del oa_in_ref
        kern(q_ref, kn_ref, vn_ref, kp_ref, vp_ref, lam_ref, gsub_ref, o_ref)

    return pl.pallas_call(
        kern_alias,
        grid=(nb,),
        in_specs=[
            pl.BlockSpec((sq, seg), lambda b: (blk0 + b, 0)),
            pl.BlockSpec((sq, seg), lambda b: (blk0 + b, 1)),
            pl.BlockSpec((sq, seg), lambda b: (blk0 + b, 2)),
            pl.BlockSpec((None, None, past_len * n_sub, dh), lambda b: (li, b, 0, 0)),
            pl.BlockSpec((None, None, past_len, seg), lambda b: (li, b, 0, 0)),
            pl.BlockSpec((None, 4, dh), lambda b: (li, 0, 0)),
            pl.BlockSpec((None, 1, dv), lambda b: (li, 0, 0)),
            pl.BlockSpec(memory_space=pl.ANY),
        ],
        out_specs=pl.BlockSpec((sq, seg), lambda b: (blk0 + b, 0)),
        out_shape=jax.ShapeDtypeStruct(oa.shape, BF16),
        input_output_aliases={7: 0},
        compiler_params=_params("arbitrary"),
        name="attn_a_sample",
    )(qkv16, qkv16, qkv16, cache_k, cache_v, lam_vecs, gsub.reshape(gsub.shape[0], 1, dv), oa)


def _attn_b_sample_kernel(q_ref, kn_ref, vn_ref, kp_ref, vp_ref, tri_ref, o_ref,
                          run_scr, er_scr, acc_scr, *, n_heads, dh):
    sq = q_ref.shape[0]
    past_len = kp_ref.shape[0] // n_heads
    tk = tri_ref.shape[0]
    run_scr[...] = jnp.zeros(run_scr.shape, F32)
    er_scr[...] = jnp.ones(er_scr.shape, F32)
    acc_scr[...] = jnp.zeros(acc_scr.shape, F32)
    kn = _pad_rows(kn_ref[...], LANES)
    vn = _pad_rows(vn_ref[...], LANES)
    r = lax.broadcasted_iota(jnp.int32, (sq, LANES), 0)
    c = lax.broadcasted_iota(jnp.int32, (sq, LANES), 1)
    live = _sb_heads(q_ref,
                     lambda h: kn[:, h * dh:(h + 1) * dh],
                     lambda h: vn[:, h * dh:(h + 1) * dh],
                     tri_ref[0:LANES, 0:LANES], c < r, run_scr, er_scr, acc_scr, n_heads, dh)

    def block(j):
        row0 = pl.multiple_of(j * (tk * n_heads), tk * n_heads)

        def head_rows(ref, h):
            return ref[pl.ds(row0 + h, tk, stride=n_heads), :].astype(BF16)

        return _sb_heads(q_ref,
                         lambda h: head_rows(kp_ref, h),
                         lambda h: head_rows(vp_ref, h),
                         tri_ref[...], None, run_scr, er_scr, acc_scr, n_heads, dh)

    _sb_earlier_blocks(past_len // tk - 1, live, block)
    o_ref[...] = acc_scr[...].astype(BF16)


def _attn_b_sample(qkv16, ob, cache_k, cache_v, li, tri, row0, sq, seg, n_heads, dh):
    _, nb, cache_rows, _ = cache_k.shape
    past_len = cache_rows // n_heads
    tk = tri.shape[1]
    assert past_len % tk == 0 and tk >= LANES
    blk0 = row0 // sq
    kern = functools.partial(_attn_b_sample_kernel, n_heads=n_heads, dh=dh)

    def kern_alias(q_ref, kn_ref, vn_ref, kp_ref, vp_ref, tri_ref, ob_in_ref, o_ref, *scr):
        del ob_in_ref
        kern(q_ref, kn_ref, vn_ref, kp_ref, vp_ref, tri_ref, o_ref, *scr)

    return pl.pallas_call(
        kern_alias,
        grid=(nb,),
        in_specs=[
            pl.BlockSpec((sq, seg), lambda b: (blk0 + b, 3)),
            pl.BlockSpec((sq, seg), lambda b: (blk0 + b, 4)),
            pl.BlockSpec((sq, seg), lambda b: (blk0 + b, 5)),
            pl.BlockSpec((None, None, past_len * n_heads, dh), lambda b: (li, b, 0, 0)),
            pl.BlockSpec((None, None, past_len * n_heads, dh), lambda b: (li, b, 0, 0)),
            pl.BlockSpec((tk, tk), lambda b: (0, 0)),
            pl.BlockSpec(memory_space=pl.ANY),
        ],
        out_specs=pl.BlockSpec((sq, seg), lambda b: (blk0 + b, 0)),
        out_shape=jax.ShapeDtypeStruct(ob.shape, BF16),
        scratch_shapes=[
            pltpu.VMEM((n_heads, sq, LANES), F32),
            pltpu.VMEM((n_heads, sq, LANES), F32),
            pltpu.VMEM((sq, seg), F32),
        ],
        input_output_aliases={6: 0},
        compiler_params=_params("arbitrary"),
        name="attn_b_sample",
    )(qkv16, qkv16, qkv16, cache_k, cache_v, tri, ob)


def _merge_kernel(x_ref, oa_ref, ob_ref, ga_ref, gb_ref, wba_ref, wbb_ref, wo_ref, o_ref):
    @pl.when(pl.program_id(1) == 0)
    def _():
        o_ref[...] = x_ref[...]

    ta = jnp.dot(oa_ref[...], wba_ref[...], preferred_element_type=F32)
    tb = jnp.dot(ob_ref[...], wbb_ref[...], preferred_element_type=F32)
    merged = ga_ref[...] * ta + gb_ref[...] * tb
    o_ref[...] += jnp.dot(merged.astype(BF16), wo_ref[...], preferred_element_type=F32)


def _merge(x, oa, ob, proj32, wba, wbb, wo, li, tm, tn, gate_col0):
    m, d = x.shape
    ka = oa.shape[1]
    kb = ob.shape[1]
    nj = d // tn
    ga0 = gate_col0 // tn
    gb0 = ga0 + nj
    return pl.pallas_call(
        _merge_kernel,
        grid=(m // tm, nj),
        in_specs=[
            pl.BlockSpec((tm, d), lambda i, j: (i, 0)),
            pl.BlockSpec((tm, ka), lambda i, j: (i, 0)),
            pl.BlockSpec((tm, kb), lambda i, j: (i, 0)),
            pl.BlockSpec((tm, tn), lambda i, j: (i, ga0 + j)),
            pl.BlockSpec((tm, tn), lambda i, j: (i, gb0 + j)),
            pl.BlockSpec((None, ka, tn), lambda i, j: (li, 0, j)),
            pl.BlockSpec((None, kb, tn), lambda i, j: (li, 0, j)),
            pl.BlockSpec((None, tn, d), lambda i, j: (li, j, 0)),
        ],
        out_specs=pl.BlockSpec((tm, d), lambda i, j: (i, 0)),
        out_shape=jax.ShapeDtypeStruct((m, d), F32),
        compiler_params=_params("parallel", "arbitrary"),
        name="merge",
    )(x, oa, ob, proj32, proj32, wba, wbb, wo)


def _rope_tables(pos, dh):
    inv = 1.0 / (ROPE_THETA ** (jnp.arange(0, dh, 2, dtype=F32) / dh))
    ang = pos.astype(F32)[:, None] * inv[None, :]
    cos = jnp.cos(ang)
    sin = jnp.sin(ang)
    return jnp.concatenate([cos, cos], axis=1), jnp.concatenate([-sin, sin], axis=1)


def kernel(x_prompt, x_sample, cache_a_k, cache_a_v, cache_b_k, cache_b_v,
           ffn1_norm, ffn1_w_gate, ffn1_w_up, ffn1_w_down,
           mix_norm, w_in, a_q_norm, a_k_norm,
           a_lam_q1, a_lam_k1, a_lam_q2, a_lam_k2, a_sub_norm,
           w_branch_a, w_branch_b, w_out,
           ffn2_norm, ffn2_w_gate, ffn2_w_up, ffn2_w_down):
    bp, sp, d = x_prompt.shape
    bs, ss, _ = x_sample.shape
    depth, _, past_len, n_sub, dh_a = cache_a_k.shape
    h_a, dv_a = cache_a_v.shape[3:]
    h_b, dh_b = cache_b_k.shape[3:]
    seg = n_sub * dh_a
    assert bp == 1 and h_a * dv_a == seg and h_b * dh_b == seg and n_sub == 2 * h_a
    assert w_in.shape[2] == 6 * seg + 2 * d and d == 2 * seg

    n_p = bp * sp
    m = n_p + bs * ss
    tm = _pick(m, (768, 512, 384, 256, 128, 64, 32))
    tf = _pick(ffn1_w_gate.shape[2], (512, 256, 128))
    tn = _pick(d, (512, 256, 128))
    tm_p = _pick(n_p, (512, 256, 128, 64, 32))
    assert n_p % (m - n_p) == 0
    tq_a, tk_a = _pick(sp, (256, 128)), _pick(sp, (512, 256, 128))
    tq_b = tk_b = _pick(math.gcd(sp, past_len), (256, 128))
    assert n_p % ss == 0 and min(tq_a, tk_a, tq_b) % CHUNK == 0

    x = jnp.concatenate([x_prompt.reshape(n_p, d), x_sample.reshape(bs * ss, d)], axis=0)
    pos = jnp.concatenate([jnp.arange(sp)] * bp + [past_len + jnp.arange(ss)] * bs)
    cos, sin = _rope_tables(pos, dh_a)
    tri = jnp.asarray(np.tril(np.ones((tk_b, tk_b), np.float32), -1), BF16)
    lam_vecs = jnp.stack([a_lam_q1, a_lam_k1, a_lam_q2, a_lam_k2], axis=1)

    bf = lambda w: w.astype(BF16)
    f1g, f1u, f1d = ffn1_w_gate, ffn1_w_up, ffn1_w_down
    f2g, f2u, f2d = ffn2_w_gate, ffn2_w_up, ffn2_w_down
    w_in16, wba16, wbb16, wo16 = bf(w_in), bf(w_branch_a), bf(w_branch_b), bf(w_out)
    rows_of = lambda c: c.reshape(depth, bs, past_len * c.shape[3], c.shape[4])
    ck_a, ck_b, cv_b = rows_of(cache_a_k), rows_of(cache_b_k), rows_of(cache_b_v)
    cv_a = cache_a_v.reshape(depth, bs, past_len, seg)
    scales = (dh_a ** -0.5 * LOG2E, -(dh_b ** -0.5) * LOG2E)

    rows, kv_p, kv_s = [], None, None
    for li in range(depth):
        lam_init = 0.8 - 0.6 * math.exp(-0.3 * li)

        x = _ffn(x, ffn1_norm, f1g, f1u, f1d, li, tm, tf)
        proj32, qkv16, *kv_p = _proj(x, mix_norm, w_in16, cos, sin, a_q_norm, a_k_norm,
                                     None, kv_p, li, depth, 0, n_p, tm_p, seg, n_sub, dh_a,
                                     *scales)
        proj32, qkv16, *kv_s = _proj(x, mix_norm, w_in16, cos, sin, a_q_norm, a_k_norm,
                                     (proj32, qkv16), kv_s, li, depth, n_p, m - n_p, m - n_p,
                                     seg, n_sub, dh_a, *scales)

        oa = jnp.zeros((m, seg), BF16)
        oa = _attn_a_prompt(qkv16, oa, lam_vecs, a_sub_norm, li, n_p, tq_a, tk_a, seg,
                            n_sub, dh_a, dv_a, lam_init)
        oa = _attn_a_sample(qkv16, oa, ck_a, cv_a, li, lam_vecs, a_sub_norm, n_p, ss,
                            seg, n_sub, dh_a, dv_a, lam_init)
        ob = jnp.zeros((m, seg), BF16)
        ob = _attn_b_prompt(qkv16, ob, tri, n_p, tq_b, seg, h_b, dh_b)
        ob = _attn_b_sample(qkv16, ob, ck_b, cv_b, li, tri, n_p, ss, seg, h_b, dh_b)

        x = _merge(x, oa, ob, proj32, wba16, wbb16, wo16, li, tm, tn, seg)
        x = _ffn(x, ffn2_norm, f2g, f2u, f2d, li, tm, tf)
        rows.append(proj32)

    av_p = jnp.stack([r[:n_p, :seg] for r in rows], axis=0).reshape(depth, bp, sp, h_a, dv_a)
    av_s = jnp.stack([r[n_p:, :seg] for r in rows], axis=0).reshape(depth, bs, ss, h_a, dv_a)
    (ak_p, bk_p, bv_p), (ak_s, bk_s, bv_s) = kv_p, kv_s
    prompt = lambda a, nh, dh: a.reshape(depth, bp, sp, nh, dh)
    sample = lambda a, nh, dh: a.reshape(depth, bs, ss, nh, dh)
    ak_p, ak_s = prompt(ak_p, n_sub, dh_a), sample(ak_s, n_sub, dh_a)
    bk_p, bk_s = prompt(bk_p, h_b, dh_b), sample(bk_s, h_b, dh_b)
    bv_p, bv_s = prompt(bv_p, h_b, dh_b), sample(bv_s, h_b, dh_b)
    return (x[:n_p].reshape(bp, sp, d), x[n_p:].reshape(bs, ss, d),
            ak_p, av_p, bk_p, bv_p, ak_s, av_s, bk_s, bv_s)
```
